```python
import numpy as np
import jax
import jax.numpy as jnp
from jax import lax

D_MODEL = 1024
BATCH = 32
SEQ = 2048
DEPTH = 4

N_MIXERS = 3
N_NSA_LAYERS = (DEPTH + 2) // 3
N_MLSTM_LAYERS = (DEPTH + 1) // 3
N_HGRN_LAYERS = DEPTH // 3

NSA_HEADS = 16
NSA_KV_GROUPS = 4
NSA_HEAD_DIM = D_MODEL // NSA_HEADS
NSA_HPG = NSA_HEADS // NSA_KV_GROUPS
NSA_KV_WIDTH = NSA_KV_GROUPS * NSA_HEAD_DIM
NSA_IN = D_MODEL + 6 * NSA_KV_WIDTH + 3 * NSA_HEADS
CMP_BLOCK = 32
CMP_STRIDE = 16
CMP_HIDDEN = 4 * NSA_HEAD_DIM
SEL_BLOCK = 64
SEL_TOPK = 8
WINDOW = 512
NSA_QCHUNK = 32

MLSTM_HEADS = 4
MLSTM_QK_DIM = D_MODEL // (2 * MLSTM_HEADS)
MLSTM_V_DIM = D_MODEL // MLSTM_HEADS
MLSTM_QK_WIDTH = 2 * MLSTM_HEADS * MLSTM_QK_DIM
MLSTM_IN = MLSTM_QK_WIDTH + MLSTM_HEADS * MLSTM_V_DIM + D_MODEL + 2 * MLSTM_HEADS
MLSTM_CONV = 4
MLSTM_CHUNK = 64

HGRN_HEADS = 8
HGRN_DIM = D_MODEL // HGRN_HEADS
HGRN_CHUNK = 32

N_EXPERTS = 32
TOP_K = 4
D_FF = D_MODEL
SWIGLU_LIMIT = 7.0
SWIGLU_ALPHA = 1.702
MOE_BLOCK = 256

LN_EPS = 1e-5
RMS_EPS = 1e-6
DEEPNORM_ALPHA = (2 * DEPTH) ** 0.25
DEEPNORM_BETA = (8 * DEPTH) ** -0.25
NEG = -1e30

kernel_name = 'hybrid_nsa_mlstm_hgrn2_moe'


def layer_norm(x, g, b):
    xf = x.astype(jnp.float32)
    mu = jnp.mean(xf, axis=-1, keepdims=True)
    var = jnp.mean(jnp.square(xf - mu), axis=-1, keepdims=True)
    return ((xf - mu) * lax.rsqrt(var + LN_EPS)).astype(x.dtype) * g + b


def head_rms_norm(h, g):
    hf = h.astype(jnp.float32)
    hf = hf * lax.rsqrt(jnp.mean(jnp.square(hf), axis=-1, keepdims=True) + RMS_EPS)
    return hf.reshape(*h.shape[:-2], -1).astype(g.dtype) * g


def masked_softmax(s, mask):
    p = jax.nn.softmax(jnp.where(mask, s.astype(jnp.float32), NEG), axis=-1)
    return p * mask


def causal_depthwise_conv(x, w, b):
    width, ch = w.shape
    y = lax.conv_general_dilated(x, w[:, None, :], window_strides=(1,), padding=[(width - 1, 0)],
                                 dimension_numbers=('NWC', 'WIO', 'NWC'), feature_group_count=ch)
    return y + b


def nsa_mixer(x, w_in, cmp_pos, cmp_w1, cmp_b1, cmp_w2, cmp_b2, gate_b, w_out):
    B, T, _ = x.shape
    G, HPG, dh, QC = NSA_KV_GROUPS, NSA_HPG, NSA_HEAD_DIM, NSA_QCHUNK
    proj = x @ w_in
    splits = [D_MODEL + n * NSA_KV_WIDTH for n in range(7)]
    q, kc, vc, ks, vs, kw, vw, gates = jnp.split(proj, splits, axis=-1)
    q = q.reshape(B, T, G, HPG, dh).transpose(0, 2, 3, 1, 4) * dh ** -0.5

    def groups(a):
        return a.reshape(B, T, G, dh).transpose(0, 2, 1, 3)

    kc, vc, ks, vs, kw, vw = (groups(a) for a in (kc, vc, ks, vs, kw, vw))
    gates = jax.nn.sigmoid(gates + gate_b).reshape(B, T, G, HPG, 3).transpose(0, 2, 3, 1, 4)

    n_cmp = (T - CMP_BLOCK) // CMP_STRIDE + 1
    cmp_start = np.arange(n_cmp) * CMP_STRIDE
    cmp_idx = cmp_start[:, None] + np.arange(CMP_BLOCK)[None, :]

    def compress(a, j):
        blocks = (a[:, :, cmp_idx] + cmp_pos[j]).reshape(B, G, n_cmp, CMP_BLOCK * dh)
        return jax.nn.gelu(blocks @ cmp_w1[j] + cmp_b1[j]) @ cmp_w2[j] + cmp_b2[j]

    k_cmp, v_cmp = compress(kc, 0), compress(vc, 1)
    cmp_end = jnp.asarray(cmp_start + CMP_BLOCK - 1, jnp.int32)

    n_sel = T // SEL_BLOCK
    topk = min(SEL_TOPK, n_sel)
    sel_start = np.arange(n_sel) * SEL_BLOCK
    overlap = (np.minimum(cmp_start[:, None] + CMP_BLOCK, sel_start[None, :] + SEL_BLOCK)
               - np.maximum(cmp_start[:, None], sel_start[None, :]))
    cmp_to_sel = jnp.asarray((np.clip(overlap, 0, None) / CMP_STRIDE).astype(np.float32))
    ks_blk = ks.reshape(B, G, n_sel, SEL_BLOCK, dh)
    vs_blk = vs.reshape(B, G, n_sel, SEL_BLOCK, dh)

    kw_pad = jnp.pad(kw, ((0, 0), (0, 0), (WINDOW, 0), (0, 0)))
    vw_pad = jnp.pad(vw, ((0, 0), (0, 0), (WINDOW, 0), (0, 0)))

    b_ix = jnp.arange(B)[:, None, None, None]
    g_ix = jnp.arange(G)[None, :, None, None]
    blk = jnp.arange(n_sel)
    win_off = jnp.arange(WINDOW + QC)

    def chunk(c):
        t0 = c * QC
        t = t0 + jnp.arange(QC)
        qc = lax.dynamic_slice_in_dim(q, t0, QC, axis=3)
        gc = lax.dynamic_slice_in_dim(gates, t0, QC, axis=3)
        p_cmp = masked_softmax(jnp.einsum('bghqd,bgcd->bghqc', qc, k_cmp), cmp_end[None, :] <= t[:, None])
        o_cmp = jnp.einsum('bghqc,bgcd->bghqd', p_cmp.astype(v_cmp.dtype), v_cmp)
        imp = jnp.einsum('bghqc,cs->bgqs', p_cmp, cmp_to_sel)
        cur = (t // SEL_BLOCK)[:, None]
        forced = (blk == 0) | (blk == cur) | (blk == cur - 1)
        score = jnp.where(forced, -NEG, jnp.where(blk <= cur, imp, NEG))
        top_score, top_idx = lax.top_k(score, topk)
        k_sel = ks_blk[b_ix, g_ix, top_idx]
        v_sel = vs_blk[b_ix, g_ix, top_idx]
        key_pos = top_idx[..., None] * SEL_BLOCK + jnp.arange(SEL_BLOCK)
        m_sel = (key_pos <= t[:, None, None]) & (top_score > 0.5 * NEG)[..., None]
        s_sel = jnp.einsum('bghqd,bgqnld->bghqnl', qc, k_sel).reshape(B, G, HPG, QC, topk * SEL_BLOCK)
        p_sel = masked_softmax(s_sel, m_sel.reshape(B, G, 1, QC, topk * SEL_BLOCK))
        o_sel = jnp.einsum('bghqnl,bgqnld->bghqd',
                           p_sel.reshape(B, G, HPG, QC, topk, SEL_BLOCK).astype(v_sel.dtype), v_sel)
        k_win = lax.dynamic_slice_in_dim(kw_pad, t0, WINDOW + QC, axis=2)
        v_win = lax.dynamic_slice_in_dim(vw_pad, t0, WINDOW + QC, axis=2)
        win_pos = t0 - WINDOW + win_off
        lag = t[:, None] - win_pos[None, :]
        m_win = (lag >= 0) & (lag < WINDOW) & (win_pos >= 0)[None, :]
        p_win = masked_softmax(jnp.einsum('bghqd,bgkd->bghqk', qc, k_win), m_win)
        o_win = jnp.einsum('bghqk,bgkd->bghqd', p_win.astype(v_win.dtype), v_win)
        return gc[..., 0:1] * o_cmp + gc[..., 1:2] * o_sel + gc[..., 2:3] * o_win

    o = lax.map(chunk, jnp.arange(T // QC))
    o = o.transpose(1, 0, 4, 2, 3, 5).reshape(B, T, D_MODEL)
    return o @ w_out


def mlstm_mixer(x, w_in, conv_w, conv_b, gate_b, norm_g, w_out):
    B, T, _ = x.shape
    H, dk, dv, L = MLSTM_HEADS, MLSTM_QK_DIM, MLSTM_V_DIM, MLSTM_CHUNK
    nc = T // L
    f32 = jnp.float32
    qk, v, o_gate, gif = jnp.split(
        x @ w_in, [MLSTM_QK_WIDTH, MLSTM_QK_WIDTH + H * dv, MLSTM_QK_WIDTH + H * dv + D_MODEL], axis=-1)
    q, k = jnp.split(jax.nn.silu(causal_depthwise_conv(qk, conv_w, conv_b)), 2, axis=-1)

    def heads(a, d):
        return a.reshape(B, T, H, d).transpose(0, 2, 1, 3).astype(f32)

    q, k, v = heads(q, dk), heads(k, dk) * dk ** -0.5, heads(v, dv)
    gif = (gif + gate_b).astype(f32).transpose(0, 2, 1)
    log_i, log_f = gif[:, :H], jax.nn.log_sigmoid(gif[:, H:])

    def chunks(a):
        return jnp.moveaxis(a.reshape(B, H, nc, L, *a.shape[3:]), 2, 0)

    causal = jnp.tril(jnp.ones((L, L), dtype=bool))

    def step(carry, inp):
        C, n, m = carry
        qc, kc, vc, li, lf = inp
        b = jnp.cumsum(lf, axis=-1)
        d_mat = jnp.where(causal, b[..., :, None] - b[..., None, :] + li[..., None, :], NEG)
        m_t = jnp.maximum(b + m[..., None], jnp.max(d_mat, axis=-1))
        w_inter = jnp.exp(b + m[..., None] - m_t)
        s = jnp.einsum('bhtd,bhsd->bhts', qc, kc) * jnp.exp(d_mat - m_t[..., None])
        num = jnp.einsum('bhts,bhsv->bhtv', s, vc) + w_inter[..., None] * jnp.einsum('bhtk,bhvk->bhtv', qc, C)
        den = jnp.sum(s, axis=-1) + w_inter * jnp.einsum('bhtk,bhk->bht', qc, n)
        h = num / jnp.maximum(jnp.abs(den), jnp.exp(-m_t))[..., None]
        b_end = b[..., -1]
        decay = b_end[..., None] - b + li
        m_new = jnp.maximum(b_end + m, jnp.max(decay, axis=-1))
        w_k = jnp.exp(decay - m_new[..., None])
        carry_scale = jnp.exp(b_end + m - m_new)
        C_new = carry_scale[..., None, None] * C + jnp.einsum('bhsv,bhsk->bhvk', vc * w_k[..., None], kc)
        n_new = carry_scale[..., None] * n + jnp.einsum('bhs,bhsk->bhk', w_k, kc)
        return (C_new, n_new, m_new), h

    init = (jnp.zeros((B, H, dv, dk), f32), jnp.zeros((B, H, dk), f32), jnp.zeros((B, H), f32))
    _, h = lax.scan(step, init, (chunks(q), chunks(k), chunks(v), chunks(log_i), chunks(log_f)))
    h = jnp.moveaxis(h, 0, 2).reshape(B, H, T, dv).transpose(0, 2, 1, 3)
    h = head_rms_norm(h, norm_g) * jax.nn.sigmoid(o_gate)
    return h @ w_out


def hgrn2_mixer(x, layer_idx, w_in, lower_table, norm_g, w_out):
    B, T, _ = x.shape
    H, d, L = HGRN_HEADS, HGRN_DIM, HGRN_CHUNK
    nc = T // L
    f32 = jnp.float32
    q, f, i, g = jnp.split(x @ w_in, 4, axis=-1)
    lb_soft = jax.nn.softmax(lower_table.astype(f32), axis=0)
    lb = jnp.cumsum(lb_soft, axis=0)[layer_idx] - lb_soft[0]
    f = lb + (1.0 - lb) * jax.nn.sigmoid(f.astype(f32))

    def chunked(a):
        return a.astype(f32).reshape(B, nc, L, H, d).transpose(1, 0, 3, 2, 4)

    xs = (chunked(jax.nn.silu(q)), chunked(1.0 - f), chunked(i), chunked(jnp.log(f)))
    causal = jnp.tril(jnp.ones((L, L), dtype=bool))

    def step(S, inp):
        qc, kc, vc, lf = inp
        G = jnp.cumsum(lf, axis=2)
        q_dec = qc * jnp.exp(G)
        a_mat = jnp.where(causal, jnp.einsum('bhtk,bhsk->bhts', q_dec, kc * jnp.exp(-G)), 0.0)
        o = jnp.einsum('bhts,bhsv->bhtv', a_mat, vc) + jnp.einsum('bhtk,bhkv->bhtv', q_dec, S)
        G_end = G[:, :, -1]
        S_new = jnp.exp(G_end)[..., None] * S + jnp.einsum('bhsk,bhsv->bhkv', kc * jnp.exp(G_end[:, :, None] - G), vc)
        return S_new, o

    _, o = lax.scan(step, jnp.zeros((B, H, d, d), f32), xs)
    o = o.transpose(1, 0, 3, 2, 4).reshape(B, T, H, d)
    o = head_rms_norm(o, norm_g) * jax.nn.sigmoid(g)
    return o @ w_out


def moe_ffn(x, router_w, router_b, w_gu, b_gu, w_down, b_down):
    B, T, D = x.shape
    N = B * T
    A = N * TOP_K
    xf = x.reshape(N, D)
    logits = xf.astype(jnp.float32) @ router_w.astype(jnp.float32) + router_b.astype(jnp.float32)
    top_val, top_idx = lax.top_k(logits, TOP_K)
    gate = jax.nn.softmax(top_val, axis=-1)
    e_flat = top_idx.reshape(A)
    order = jnp.argsort(e_flat)
    e_sorted = e_flat[order]
    counts = jnp.bincount(e_flat, length=N_EXPERTS)
    padded = (counts + MOE_BLOCK - 1) // MOE_BLOCK * MOE_BLOCK
    start = jnp.cumsum(counts) - counts
    padded_end = jnp.cumsum(padded)
    padded_start = padded_end - padded
    dest = padded_start[e_sorted] + jnp.arange(A) - start[e_sorted]
    n_blocks = (A + N_EXPERTS * (MOE_BLOCK - 1) + MOE_BLOCK - 1) // MOE_BLOCK
    rows = n_blocks * MOE_BLOCK
    row_tok = jnp.full((rows,), N, jnp.int32).at[dest].set((order // TOP_K).astype(jnp.int32))
    row_gate = jnp.zeros((rows,), jnp.float32).at[dest].set(gate.reshape(A)[order])
    block_expert = jnp.minimum(
        jnp.searchsorted(padded_end, jnp.arange(n_blocks) * MOE_BLOCK, side='right'), N_EXPERTS - 1)
    x_pad = jnp.concatenate([xf, jnp.zeros((1, D), xf.dtype)], axis=0)

    def expert_block(args):
        e, tok, gw = args
        xb = x_pad[tok]
        h_gate, h_up = jnp.split(xb @ w_gu[e] + b_gu[e], 2, axis=-1)
        h_gate = jnp.minimum(h_gate, SWIGLU_LIMIT)
        h_up = jnp.clip(h_up, -SWIGLU_LIMIT, SWIGLU_LIMIT)
        h = (h_up + 1.0) * h_gate * jax.nn.sigmoid(SWIGLU_ALPHA * h_gate)
        return (h @ w_down[e] + b_down[e]) * gw[:, None].astype(xb.dtype)

    y = lax.map(expert_block, (block_expert, row_tok.reshape(n_blocks, MOE_BLOCK),
                               row_gate.reshape(n_blocks, MOE_BLOCK)))
    out = jax.ops.segment_sum(y.reshape(rows, D), row_tok, num_segments=N + 1)
    return out[:N].reshape(B, T, D)


def setup_inputs(seed: int = 0) -> dict:
    key = jax.random.key(seed)
    keys = iter(jax.random.split(key, 32))

    def rnd(shape, scale):
        return jax.random.normal(next(keys), shape, jnp.float32) * scale

    D, E, F = D_MODEL, N_EXPERTS, D_FF
    dh, H = NSA_HEAD_DIM, MLSTM_HEADS
    inv = D ** -0.5
    return {
        'x': rnd((BATCH, SEQ, D), 1.0),
        'ln_g': 1.0 + rnd((DEPTH, 2, D), 0.02),
        'ln_b': rnd((DEPTH, 2, D), 0.02),
        'nsa_w_in': rnd((N_NSA_LAYERS, D, NSA_IN), inv),
        'nsa_cmp_pos': rnd((N_NSA_LAYERS, 2, CMP_BLOCK, dh), 0.1),
        'nsa_cmp_w1': rnd((N_NSA_LAYERS, 2, CMP_BLOCK * dh, CMP_HIDDEN), (CMP_BLOCK * dh) ** -0.5),
        'nsa_cmp_b1': rnd((N_NSA_LAYERS, 2, CMP_HIDDEN), 0.02),
        'nsa_cmp_w2': rnd((N_NSA_LAYERS, 2, CMP_HIDDEN, dh), CMP_HIDDEN ** -0.5),
        'nsa_cmp_b2': rnd((N_NSA_LAYERS, 2, dh), 0.02),
        'nsa_gate_b': rnd((N_NSA_LAYERS, 3 * NSA_HEADS), 0.02),
        'nsa_w_out': rnd((N_NSA_LAYERS, D, D), inv * DEEPNORM_BETA),
        'ml_w_in': rnd((N_MLSTM_LAYERS, D, MLSTM_IN), inv),
        'ml_conv_w': rnd((N_MLSTM_LAYERS, MLSTM_CONV, MLSTM_QK_WIDTH), MLSTM_CONV ** -0.5),
        'ml_conv_b': rnd((N_MLSTM_LAYERS, MLSTM_QK_WIDTH), 0.02),
        'ml_gate_b': jnp.concatenate([rnd((N_MLSTM_LAYERS, H), 0.1),
                                      3.0 + rnd((N_MLSTM_LAYERS, H), 0.1)], axis=-1),
        'ml_norm_g': 1.0 + rnd((N_MLSTM_LAYERS, H * MLSTM_V_DIM), 0.02),
        'ml_w_out': rnd((N_MLSTM_LAYERS, D, D), inv * DEEPNORM_BETA),
        'hg_w_in': rnd((N_HGRN_LAYERS, D, 4 * D), inv),
        'hg_lower': rnd((DEPTH, D), 0.1),
        'hg_norm_g': 1.0 + rnd((N_HGRN_LAYERS, D), 0.02),
        'hg_w_out': rnd((N_HGRN_LAYERS, D, D), inv * DEEPNORM_BETA),
        'router_w': rnd((DEPTH, D, E), inv),
        'router_b': rnd((DEPTH, E), 0.01),
        'moe_w_gu': rnd((DEPTH, E, D, 2 * F), inv),
        'moe_b_gu': rnd((DEPTH, E, 2 * F), 0.02),
        'moe_w_down': rnd((DEPTH, E, F, D), F ** -0.5 * DEEPNORM_BETA),
        'moe_b_down': rnd((DEPTH, E, D), 0.02),
    }


def reference(x, ln_g, ln_b, nsa_w_in, nsa_cmp_pos, nsa_cmp_w1, nsa_cmp_b1, nsa_cmp_w2, nsa_cmp_b2,
              nsa_gate_b, nsa_w_out, ml_w_in, ml_conv_w, ml_conv_b, ml_gate_b, ml_norm_g, ml_w_out,
              hg_w_in, hg_lower, hg_norm_g, hg_w_out, router_w, router_b, moe_w_gu, moe_b_gu,
              moe_w_down, moe_b_down):
    for layer in range(DEPTH):
        kind, slot = layer % N_MIXERS, layer // N_MIXERS
        if kind == 0:
            h = nsa_mixer(x, nsa_w_in[slot], nsa_cmp_pos[slot], nsa_cmp_w1[slot], nsa_cmp_b1[slot],
                          nsa_cmp_w2[slot], nsa_cmp_b2[slot], nsa_gate_b[slot], nsa_w_out[slot])
        elif kind == 1:
            h = mlstm_mixer(x, ml_w_in[slot], ml_conv_w[slot], ml_conv_b[slot], ml_gate_b[slot],
                            ml_norm_g[slot], ml_w_out[slot])
        else:
            h = hgrn2_mixer(x, layer, hg_w_in[slot], hg_lower, hg_norm_g[slot], hg_w_out[slot])
        x = layer_norm(DEEPNORM_ALPHA * x + h, ln_g[layer, 0], ln_b[layer, 0])
        h = moe_ffn(x, router_w[layer], router_b[layer], moe_w_gu[layer], moe_b_gu[layer],
                    moe_w_down[layer], moe_b_down[layer])
        x = layer_norm(DEEPNORM_ALPHA * x + h, ln_g[layer, 1], ln_b[layer, 1])
    return x
```

```python
import functools

import numpy as np
import jax
import jax.numpy as jnp
from jax import lax
from jax.experimental import pallas as pl
from jax.experimental.pallas import tpu as pltpu

F32 = jnp.float32
BF16 = jnp.bfloat16
HIGHEST = lax.Precision.HIGHEST

D_MODEL = 1024
DEPTH = 4
N_MIXERS = 3

NSA_HEADS = 16
NSA_KV_GROUPS = 4
NSA_HEAD_DIM = 64
NSA_HPG = 4
NSA_KV_WIDTH = 256
CMP_BLOCK = 32
CMP_STRIDE = 16
CMP_HIDDEN = 256
SEL_BLOCK = 64
SEL_TOPK = 8
WINDOW = 512

MLSTM_HEADS = 4
MLSTM_QK_DIM = 128
MLSTM_V_DIM = 256
MLSTM_CONV = 4

HGRN_HEADS = 8
HGRN_DIM = 128
HGRN_CHUNK = 32

N_EXPERTS = 32
TOP_K = 4
D_FF = 1024
SWIGLU_LIMIT = 7.0
SWIGLU_ALPHA = 1.702

LN_EPS = 1e-5
RMS_EPS = 1e-6
DEEPNORM_ALPHA = (2 * DEPTH) ** 0.25
NEG = -1e30

LANE = 128
VMEM_LIMIT = 56 * 1024 * 1024

PROJ_TM = 512
ROUTER_TM = 512
MOE_BLOCK = 256
DISPATCH_TB = 512
COMBINE_TB = 256
NSA_TQ = 128
NSA_TK = 512
MLSTM_L = 128


def _cparams(sem):
    return pltpu.CompilerParams(dimension_semantics=sem, vmem_limit_bytes=VMEM_LIMIT)


def _dot(a, b):
    return jnp.dot(a, b, preferred_element_type=F32)


def _dot_nt(a, b, precision=None):
    return lax.dot_general(a, b, (((1,), (1,)), ((), ())), precision=precision, preferred_element_type=F32)


def _dot_tn(a, b):
    return lax.dot_general(a, b, (((0,), (0,)), ((), ())), preferred_element_type=F32)


def _layer_norm(z, g, b):
    mu = jnp.mean(z, axis=-1, keepdims=True)
    zc = z - mu
    var = jnp.mean(zc * zc, axis=-1, keepdims=True)
    return zc * lax.rsqrt(var + LN_EPS) * g + b


def _proj_kernel(x_ref, *refs, n_w, n_t):
    w_refs = refs[:n_w]
    wt_refs = refs[n_w:n_w + n_t]
    o_refs = refs[n_w + n_t:2 * n_w + n_t]
    ot_refs = refs[2 * n_w + n_t:]
    x = x_ref[...]
    xb = x.astype(BF16)
    for w_ref, o_ref in zip(w_refs, o_refs):
        ncol = w_ref.shape[1]
        for c0 in range(0, ncol, 512):
            c1 = min(c0 + 512, ncol)
            o_ref[:, c0:c1] = _dot(xb, w_ref[:, c0:c1]).astype(o_ref.dtype)
    for wt_ref, o_ref in zip(wt_refs, ot_refs):
        o_ref[...] = _dot_nt(wt_ref[...], x, precision=HIGHEST)


def _proj(x2, weights, out_dtypes, weights_t=()):
    n, d = x2.shape
    tm = min(PROJ_TM, n)
    in_specs = [pl.BlockSpec((tm, d), lambda i: (i, 0))]
    in_specs += [pl.BlockSpec(w.shape, lambda i: (0, 0)) for w in weights]
    in_specs += [pl.BlockSpec(w.shape, lambda i: (0, 0)) for w in weights_t]
    out_shape = [jax.ShapeDtypeStruct((n, w.shape[1]), dt) for w, dt in zip(weights, out_dtypes)]
    out_shape += [jax.ShapeDtypeStruct((w.shape[0], n), F32) for w in weights_t]
    out_specs = [pl.BlockSpec((tm, w.shape[1]), lambda i: (i, 0)) for w in weights]
    out_specs += [pl.BlockSpec((w.shape[0], tm), lambda i: (0, i)) for w in weights_t]
    return pl.pallas_call(
        functools.partial(_proj_kernel, n_w=len(weights), n_t=len(weights_t)),
        grid=(n // tm,),
        in_specs=in_specs,
        out_specs=out_specs,
        out_shape=out_shape,
        compiler_params=_cparams(("parallel",)),
        name="proj",
    )(x2, *weights, *weights_t)


def _outproj_ln_kernel(h_ref, w_ref, x_ref, g_ref, b_ref, o_ref):
    y = _dot(h_ref[...], w_ref[...])
    o_ref[...] = _layer_norm(DEEPNORM_ALPHA * x_ref[...] + y, g_ref[...], b_ref[...])


def _outproj_ln(h2, w, x2, g, b):
    n, d = x2.shape
    tm = min(PROJ_TM, n)
    return pl.pallas_call(
        _outproj_ln_kernel,
        grid=(n // tm,),
        in_specs=[
            pl.BlockSpec((tm, d), lambda i: (i, 0)),
            pl.BlockSpec((d, d), lambda i: (0, 0)),
            pl.BlockSpec((tm, d), lambda i: (i, 0)),
            pl.BlockSpec((1, d), lambda i: (0, 0)),
            pl.BlockSpec((1, d), lambda i: (0, 0)),
        ],
        out_specs=pl.BlockSpec((tm, d), lambda i: (i, 0)),
        out_shape=jax.ShapeDtypeStruct((n, d), F32),
        compiler_params=_cparams(("parallel",)),
        name="outproj_ln",
    )(h2, w, x2, g.reshape(1, d), b.reshape(1, d))


def _gelu_tanh(x):
    return 0.5 * x * (1.0 + jnp.tanh(np.sqrt(2.0 / np.pi).astype(np.float32) * (x + 0.044715 * (x * x * x))))


def _nsa_compress_kernel(kv_ref, wcat_ref, posa_ref, posb_ref, b1_ref, w2_ref, b2_ref, o_ref):
    n_row = kv_ref.shape[1] // CMP_STRIDE
    half = 2 * CMP_HIDDEN
    acc_a = jnp.zeros((n_row, half), F32)
    acc_b = jnp.zeros((n_row, half), F32)
    for r in range(CMP_STRIDE):
        xr = kv_ref[0, pl.ds(r, n_row, stride=CMP_STRIDE), :]
        w = wcat_ref[r]
        acc_a = acc_a + _dot((xr + posa_ref[r]).astype(BF16), w[:, :half])
        acc_b = acc_b + _dot((xr + posb_ref[r]).astype(BF16), w[:, half:])
    hidden = acc_a + pltpu.roll(acc_b, n_row - 1, 0) + b1_ref[...]
    hidden = _gelu_tanh(hidden)
    o_ref[0, 0] = (_dot(hidden.astype(BF16), w2_ref[...]) + b2_ref[...]).astype(o_ref.dtype)


def _nsa_compress(kvc, wcat, posa, posb, b1cat, w2cat, b2cat):
    bsz, t, _ = kvc.shape
    n_row = t // CMP_STRIDE
    return pl.pallas_call(
        _nsa_compress_kernel,
        grid=(bsz, NSA_KV_GROUPS),
        in_specs=[
            pl.BlockSpec((1, t, LANE), lambda b, g: (b, 0, g)),
            pl.BlockSpec(wcat.shape, lambda b, g: (0, 0, 0)),
            pl.BlockSpec(posa.shape, lambda b, g: (0, 0, 0)),
            pl.BlockSpec(posb.shape, lambda b, g: (0, 0, 0)),
            pl.BlockSpec(b1cat.shape, lambda b, g: (0, 0)),
            pl.BlockSpec(w2cat.shape, lambda b, g: (0, 0)),
            pl.BlockSpec(b2cat.shape, lambda b, g: (0, 0)),
        ],
        out_specs=pl.BlockSpec((1, 1, n_row, LANE), lambda b, g: (b, g, 0, 0)),
        out_shape=jax.ShapeDtypeStruct((bsz, NSA_KV_GROUPS, n_row, LANE), BF16),
        compiler_params=_cparams(("parallel", "parallel")),
        name="nsa_compress",
    )(kvc, wcat, posa, posb, b1cat, w2cat, b2cat)


def _nsa_attn_kernel(q_ref, kvc_ref, kvs_ref, kvw_ref, gate_ref, gb_ref, c2s_ref, o_ref, *, tq, tk, n_cmp):
    hpg = NSA_HPG
    t0 = pl.program_id(2) * tq
    q4 = jnp.concatenate([q_ref[0, :, h * LANE:(h + 1) * LANE] for h in range(hpg)], axis=0)
    q4 = q4 * jnp.asarray(NSA_HEAD_DIM ** -0.5, BF16)
    t_col = t0 + lax.broadcasted_iota(jnp.int32, (tq, 1), 0)

    kvc = kvc_ref[0, 0]
    n_c = kvc.shape[0]
    s_c = _dot_nt(q4, kvc)
    c_lane = lax.broadcasted_iota(jnp.int32, (1, n_c), 1)
    valid_c = (c_lane * CMP_STRIDE + (CMP_BLOCK - 1) <= t_col) & (c_lane < n_cmp)
    p_heads = []
    for h in range(hpg):
        sm = jnp.where(valid_c, s_c[h * tq:(h + 1) * tq], NEG)
        m = jnp.max(sm, axis=-1, keepdims=True)
        p = jnp.where(valid_c, jnp.exp(sm - m), 0.0)
        l = jnp.sum(p, axis=-1, keepdims=True)
        p_heads.append(p / jnp.maximum(l, 1e-30))
    o_c = _dot(jnp.concatenate(p_heads, axis=0).astype(BF16), kvc)
    p_sum = p_heads[0]
    for h in range(1, hpg):
        p_sum = p_sum + p_heads[h]
    imp = jnp.dot(p_sum, c2s_ref[...], precision=HIGHEST, preferred_element_type=F32)

    n_sel = imp.shape[1]
    blk = lax.broadcasted_iota(jnp.int32, (1, n_sel), 1)
    cur = t_col // SEL_BLOCK
    forced = (blk == 0) | (blk == cur) | (blk == cur - 1)
    score = jnp.where(forced, -NEG, jnp.where(blk <= cur, imp, NEG))
    rank = jnp.zeros((tq, n_sel), F32)
    for i in range(n_sel):
        s_i = score[:, i:i + 1]
        ahead = (s_i > score) | ((s_i == score) & (blk > i))
        rank = rank + ahead.astype(F32)
    member = ((rank < SEL_TOPK) & (score > 0.5 * NEG)).astype(BF16)

    blk_per_tile = tk // SEL_BLOCK
    sel_row = lax.broadcasted_iota(jnp.int32, (n_sel, 1), 0)
    key_lane = lax.broadcasted_iota(jnp.int32, (1, tk), 1)

    def sel_body(kt, carry):
        k0 = pl.multiple_of(kt * tk, tk)
        kv = kvs_ref[0, pl.ds(k0, tk), :]
        s = _dot_nt(q4, kv)
        expand = (key_lane // SEL_BLOCK + kt * blk_per_tile == sel_row).astype(BF16)
        valid = (_dot(member, expand) > 0.5) & (k0 + key_lane <= t_col)
        new = []
        pbs = []
        for h in range(hpg):
            m_old, l_old, acc_old = carry[h]
            sm = jnp.where(valid, s[h * tq:(h + 1) * tq], NEG)
            m_new = jnp.maximum(m_old, jnp.max(sm, axis=-1, keepdims=True))
            alpha = jnp.exp(m_old - m_new)
            p = jnp.where(valid, jnp.exp(sm - m_new), 0.0)
            pbs.append(p.astype(BF16))
            new.append((m_new, alpha * l_old + jnp.sum(p, axis=-1, keepdims=True), alpha * acc_old))
        pv = _dot(jnp.concatenate(pbs, axis=0), kv)
        return tuple((new[h][0], new[h][1], new[h][2] + pv[h * tq:(h + 1) * tq]) for h in range(hpg))

    init = tuple((jnp.full((tq, 1), NEG, F32), jnp.zeros((tq, 1), F32), jnp.zeros((tq, LANE), F32))
                 for _ in range(hpg))
    n_kt = (t0 + tq - 1) // tk + 1
    sel = lax.fori_loop(0, n_kt, sel_body, init)

    span = WINDOW + tq
    w0 = pl.multiple_of(jnp.maximum(t0 - WINDOW, 0), tq)
    kvw = kvw_ref[0, pl.ds(w0, span), :]
    s_w = _dot_nt(q4, kvw)
    lag = t_col - (w0 + lax.broadcasted_iota(jnp.int32, (1, span), 1))
    valid_w = (lag >= 0) & (lag < WINDOW)
    pws = []
    l_w = []
    for h in range(hpg):
        sm = jnp.where(valid_w, s_w[h * tq:(h + 1) * tq], NEG)
        m = jnp.max(sm, axis=-1, keepdims=True)
        p = jnp.where(valid_w, jnp.exp(sm - m), 0.0)
        l_w.append(jnp.sum(p, axis=-1, keepdims=True))
        pws.append(p.astype(BF16))
    o_w = _dot(jnp.concatenate(pws, axis=0), kvw)

    gates = jax.nn.sigmoid(gate_ref[0].astype(F32) + gb_ref[0])
    lane = lax.broadcasted_iota(jnp.int32, (1, LANE), 1)
    outs = []
    for h in range(hpg):
        rows = slice(h * tq, (h + 1) * tq)
        o_s = sel[h][2] / sel[h][1]
        outs.append(gates[:, 3 * h:3 * h + 1] * o_c[rows]
                    + gates[:, 3 * h + 1:3 * h + 2] * o_s
                    + gates[:, 3 * h + 2:3 * h + 3] * (o_w[rows] / l_w[h]))
    for j in range(hpg // 2):
        pair = jnp.where(lane < NSA_HEAD_DIM, pltpu.roll(outs[2 * j], NSA_HEAD_DIM, 1), outs[2 * j + 1])
        o_ref[0, :, j * LANE:(j + 1) * LANE] = pair.astype(o_ref.dtype)


def _nsa_attention(main, kvcmp, gate_b, c2s, bsz, t):
    tq = min(NSA_TQ, t)
    tk = min(NSA_TK, t)
    n_cmp = (t - CMP_BLOCK) // CMP_STRIDE + 1
    q_blocks = NSA_HEADS
    g = NSA_KV_GROUPS
    kernel = functools.partial(_nsa_attn_kernel, tq=tq, tk=tk, n_cmp=n_cmp)
    return pl.pallas_call(
        kernel,
        grid=(bsz, g, t // tq),
        in_specs=[
            pl.BlockSpec((1, tq, NSA_HPG * LANE), lambda b, gi, qi: (b, qi, gi)),
            pl.BlockSpec((1, 1) + kvcmp.shape[2:], lambda b, gi, qi: (b, gi, 0, 0)),
            pl.BlockSpec((1, t, LANE), lambda b, gi, qi: (b, 0, q_blocks + gi)),
            pl.BlockSpec((1, t, LANE), lambda b, gi, qi: (b, 0, q_blocks + g + gi)),
            pl.BlockSpec((1, tq, LANE), lambda b, gi, qi: (b, qi, q_blocks + 2 * g + gi)),
            pl.BlockSpec((1, 1, LANE), lambda b, gi, qi: (gi, 0, 0)),
            pl.BlockSpec(c2s.shape, lambda b, gi, qi: (0, 0)),
        ],
        out_specs=pl.BlockSpec((1, tq, NSA_HPG * NSA_HEAD_DIM), lambda b, gi, qi: (b, qi, gi)),
        out_shape=jax.ShapeDtypeStruct((bsz, t, D_MODEL), BF16),
        compiler_params=_cparams(("parallel", "parallel", "arbitrary")),
        name="nsa_attention",
    )(main, kvcmp, main, main, main, gate_b, c2s)


def _nsa_mixer(x2, bsz, t, w_in, cmp_pos, cmp_w1, cmp_b1, cmp_w2, cmp_b2, gate_b, w_out, ln_g, ln_b):
    d, g, dh, hpg = D_MODEL, NSA_KV_GROUPS, NSA_HEAD_DIM, NSA_HPG
    kvw_ = NSA_KV_WIDTH

    def pair(k0):
        k = w_in[:, k0:k0 + kvw_].reshape(d, g, 1, dh)
        v = w_in[:, k0 + kvw_:k0 + 2 * kvw_].reshape(d, g, 1, dh)
        return jnp.concatenate([k, v], axis=2).reshape(d, g * LANE)

    wq = jnp.pad(w_in[:, :d].reshape(d, NSA_HEADS, dh), ((0, 0), (0, 0), (0, LANE - dh))).reshape(d, NSA_HEADS * LANE)
    n_gate = 3 * hpg
    wg = jnp.pad(w_in[:, d + 6 * kvw_:].reshape(d, g, n_gate), ((0, 0), (0, 0), (0, LANE - n_gate))).reshape(d, g * LANE)
    w_main = jnp.concatenate([wq, pair(d + 2 * kvw_), pair(d + 4 * kvw_), wg], axis=1).astype(BF16)
    w_kvc = pair(d).astype(BF16)
    main, kvc = _proj(x2, [w_main, w_kvc], [BF16, F32])

    half = CMP_BLOCK // 2
    w1 = cmp_w1.reshape(2, CMP_BLOCK, dh, CMP_HIDDEN)
    z = jnp.zeros((half, dh, CMP_HIDDEN), F32)
    top = jnp.concatenate([w1[0, :half], z, w1[0, half:], z], axis=2)
    bot = jnp.concatenate([z, w1[1, :half], z, w1[1, half:]], axis=2)
    wcat = jnp.concatenate([top, bot], axis=1).astype(BF16)
    pos = jnp.concatenate([cmp_pos[0], cmp_pos[1]], axis=-1)
    posa, posb = pos[:half, None, :], pos[half:, None, :]
    b1cat = cmp_b1.reshape(1, 2 * CMP_HIDDEN)
    zz = jnp.zeros((CMP_HIDDEN, dh), F32)
    w2cat = jnp.concatenate([jnp.concatenate([cmp_w2[0], zz], axis=1),
                             jnp.concatenate([zz, cmp_w2[1]], axis=1)], axis=0).astype(BF16)
    b2cat = cmp_b2.reshape(1, 2 * dh)
    kvcmp = _nsa_compress(kvc.reshape(bsz, t, g * LANE), wcat, posa, posb, b1cat, w2cat, b2cat)

    n_row = t // CMP_STRIDE
    n_sel = t // SEL_BLOCK
    cmp_start = np.arange(n_row) * CMP_STRIDE
    sel_start = np.arange(n_sel) * SEL_BLOCK
    overlap = (np.minimum(cmp_start[:, None] + CMP_BLOCK, sel_start[None, :] + SEL_BLOCK)
               - np.maximum(cmp_start[:, None], sel_start[None, :]))
    c2s = jnp.asarray((np.clip(overlap, 0, None) / CMP_STRIDE).astype(np.float32))
    gb = jnp.pad(gate_b.reshape(g, 1, n_gate), ((0, 0), (0, 0), (0, LANE - n_gate)))
    o = _nsa_attention(main.reshape(bsz, t, -1), kvcmp, gb, c2s, bsz, t)
    return _outproj_ln(o.reshape(bsz * t, d), w_out.astype(BF16), x2, ln_g, ln_b)


def _mlstm_kernel(q_ref, k_ref, v_ref, og_ref, gif_ref, gb_ref, cwq_ref, cwk_ref, cbq_ref, cbk_ref, ng_ref,
                  o_ref, pad_s, q_s, k_s, ct_s, *, chunk):
    t = q_ref.shape[1]
    dk, dv = MLSTM_QK_DIM, MLSTM_V_DIM
    head = pl.program_id(1)
    halo = 8

    def conv_silu(x_ref, w_ref, b_ref, dst, scale):
        pad_s[0:halo, :] = jnp.zeros((halo, dk), F32)
        pad_s[halo:halo + t, :] = x_ref[0]
        rows = min(t, 256)
        for r0 in range(0, t, rows):
            y = b_ref[...] + w_ref[0:1, :] * pad_s[pl.ds(halo + r0 - (MLSTM_CONV - 1), rows), :]
            for j in range(1, MLSTM_CONV):
                y = y + w_ref[j:j + 1, :] * pad_s[pl.ds(halo + r0 - (MLSTM_CONV - 1) + j, rows), :]
            y = y * jax.nn.sigmoid(y)
            dst[r0:r0 + rows, :] = (y * scale).astype(dst.dtype)

    conv_silu(q_ref, cwq_ref, cbq_ref, q_s, 1.0)
    conv_silu(k_ref, cwk_ref, cbk_ref, k_s, dk ** -0.5)
    ct_s[...] = jnp.zeros((dk, dv), F32)

    row = lax.broadcasted_iota(jnp.int32, (chunk, chunk), 0)
    col = lax.broadcasted_iota(jnp.int32, (chunk, chunk), 1)
    causal = col <= row
    eye = col == row
    upper = (row <= col).astype(F32)
    b_i = gb_ref[pl.ds(head, 1), :]
    b_f = gb_ref[pl.ds(MLSTM_HEADS + head, 1), :]
    norm_g = ng_ref[...]

    def body(c, carry):
        n_row, m_prev = carry
        r0 = pl.multiple_of(c * chunk, chunk)
        qc = q_s[pl.ds(r0, chunk), :]
        kc = k_s[pl.ds(r0, chunk), :]
        vc = v_ref[0, pl.ds(r0, chunk), :]
        li_row = gif_ref[0, c, pl.ds(head, 1), :] + b_i
        zf = gif_ref[0, c, pl.ds(MLSTM_HEADS + head, 1), :] + b_f
        lf_row = jnp.minimum(zf, 0.0) - jnp.log1p(jnp.exp(-jnp.abs(zf)))
        b_row = jnp.dot(lf_row, upper, precision=HIGHEST, preferred_element_type=F32)
        b_col = jnp.sum(jnp.where(causal, lf_row, 0.0), axis=-1, keepdims=True)
        li_col = jnp.sum(jnp.where(eye, li_row, 0.0), axis=-1, keepdims=True)
        d_mat = jnp.where(causal, b_col - b_row + li_row, NEG)
        m_t = jnp.maximum(b_col + m_prev, jnp.max(d_mat, axis=-1, keepdims=True))
        w_inter = jnp.exp(b_col + m_prev - m_t)
        s = _dot_nt(qc, kc) * jnp.exp(d_mat - m_t)
        num = _dot(s.astype(BF16), vc) + w_inter * _dot(qc, ct_s[...].astype(BF16))
        den = jnp.sum(s, axis=-1, keepdims=True) + w_inter * jnp.sum(qc.astype(F32) * n_row, axis=-1, keepdims=True)
        h = num / jnp.maximum(jnp.abs(den), jnp.exp(-m_t))
        b_end = b_col[chunk - 1:chunk, :]
        decay = b_end - b_col + li_col
        m_new = jnp.maximum(b_end + m_prev, jnp.max(decay, axis=0, keepdims=True))
        w_k = jnp.exp(decay - m_new)
        scale = jnp.exp(b_end + m_prev - m_new)
        ct_s[...] = scale * ct_s[...] + _dot_tn(kc, (vc.astype(F32) * w_k).astype(BF16))
        n_new = scale * n_row + jnp.sum(w_k * kc.astype(F32), axis=0, keepdims=True)
        hn = h * lax.rsqrt(jnp.mean(h * h, axis=-1, keepdims=True) + RMS_EPS) * norm_g
        og = og_ref[0, pl.ds(r0, chunk), :].astype(F32)
        o_ref[0, pl.ds(r0, chunk), :] = (hn * jax.nn.sigmoid(og)).astype(o_ref.dtype)
        return n_new, m_new

    lax.fori_loop(0, t // chunk, body, (jnp.zeros((1, dk), F32), jnp.zeros((1, 1), F32)))


def _mlstm_core(qk, v, og, gif, gate_b, conv_w, conv_b, norm_g, bsz, t, chunk):
    hh, dk, dv = MLSTM_HEADS, MLSTM_QK_DIM, MLSTM_V_DIM
    nc = t // chunk
    kernel = functools.partial(_mlstm_kernel, chunk=chunk)
    return pl.pallas_call(
        kernel,
        grid=(bsz, hh),
        in_specs=[
            pl.BlockSpec((1, t, dk), lambda b, h: (b, 0, h)),
            pl.BlockSpec((1, t, dk), lambda b, h: (b, 0, hh + h)),
            pl.BlockSpec((1, t, dv), lambda b, h: (b, 0, h)),
            pl.BlockSpec((1, t, dv), lambda b, h: (b, 0, h)),
            pl.BlockSpec((1, nc, 2 * hh, chunk), lambda b, h: (b, 0, 0, 0)),
            pl.BlockSpec((2 * hh, 1), lambda b, h: (0, 0)),
            pl.BlockSpec((MLSTM_CONV, dk), lambda b, h: (0, h)),
            pl.BlockSpec((MLSTM_CONV, dk), lambda b, h: (0, hh + h)),
            pl.BlockSpec((1, dk), lambda b, h: (0, h)),
            pl.BlockSpec((1, dk), lambda b, h: (0, hh + h)),
            pl.BlockSpec((1, dv), lambda b, h: (0, h)),
        ],
        out_specs=pl.BlockSpec((1, t, dv), lambda b, h: (b, 0, h)),
        out_shape=jax.ShapeDtypeStruct((bsz, t, hh * dv), BF16),
        scratch_shapes=[
            pltpu.VMEM((t + 8, dk), F32),
            pltpu.VMEM((t, dk), BF16),
            pltpu.VMEM((t, dk), BF16),
            pltpu.VMEM((dk, dv), F32),
        ],
        compiler_params=_cparams(("parallel", "parallel")),
        name="mlstm",
    )(qk, qk, v, og, gif, gate_b.reshape(2 * hh, 1), conv_w, conv_w, conv_b.reshape(1, -1), conv_b.reshape(1, -1),
      norm_g.reshape(1, -1))


def _mlstm_mixer(x2, bsz, t, w_in, conv_w, conv_b, gate_b, norm_g, w_out, ln_g, ln_b):
    d, hh = D_MODEL, MLSTM_HEADS
    qkw = 2 * hh * MLSTM_QK_DIM
    vw = hh * MLSTM_V_DIM
    chunk = min(MLSTM_L, t)
    w_qk = w_in[:, :qkw].astype(BF16)
    w_v = w_in[:, qkw:qkw + vw].astype(BF16)
    w_og = w_in[:, qkw + vw:qkw + vw + d].astype(BF16)
    w_gif_t = w_in[:, qkw + vw + d:].T
    qk, v, og, gif_t = _proj(x2, [w_qk, w_v, w_og], [F32, BF16, BF16], [w_gif_t])
    gif = gif_t.reshape(2 * hh, bsz, t // chunk, chunk).transpose(1, 2, 0, 3)
    h = _mlstm_core(qk.reshape(bsz, t, qkw), v.reshape(bsz, t, vw), og.reshape(bsz, t, d), gif, gate_b,
                    conv_w, conv_b, norm_g, bsz, t, chunk)
    return _outproj_ln(h.reshape(bsz * t, d), w_out.astype(BF16), x2, ln_g, ln_b)


def _hgrn_kernel(q_ref, f_ref, i_ref, g_ref, low_ref, ng_ref, o_ref, st_s, *, layer_idx, heads):
    t = q_ref.shape[1]
    dh, chunk = HGRN_DIM, HGRN_CHUNK
    low = low_ref[...]
    e = jnp.exp(low - jnp.max(low, axis=0, keepdims=True))
    soft = e / jnp.sum(e, axis=0, keepdims=True)
    lb = jnp.zeros((1, heads * dh), F32)
    for r in range(1, layer_idx + 1):
        lb = lb + soft[r:r + 1, :]
    norm_g = ng_ref[...]
    st_s[...] = jnp.zeros(st_s.shape, F32)
    row = lax.broadcasted_iota(jnp.int32, (chunk, chunk), 0)
    col = lax.broadcasted_iota(jnp.int32, (chunk, chunk), 1)
    causal = col <= row
    lower = causal.astype(F32)

    def body(c, carry):
        r0 = pl.multiple_of(c * chunk, chunk)
        for h in range(heads):
            cols = slice(h * dh, (h + 1) * dh)
            lbh = lb[:, cols]
            f = lbh + (1.0 - lbh) * jax.nn.sigmoid(f_ref[0, pl.ds(r0, chunk), cols])
            gcum = jnp.dot(lower, jnp.log(f), precision=HIGHEST, preferred_element_type=F32)
            qv = q_ref[0, pl.ds(r0, chunk), cols].astype(F32)
            q_dec = (qv * jax.nn.sigmoid(qv) * jnp.exp(gcum)).astype(BF16)
            kk = 1.0 - f
            k_dec = (kk * jnp.exp(-gcum)).astype(BF16)
            vv = i_ref[0, pl.ds(r0, chunk), cols]
            a = jnp.where(causal, _dot_nt(q_dec, k_dec), 0.0)
            st = st_s[h]
            o = _dot(a.astype(BF16), vv) + _dot_nt(q_dec, st.astype(BF16))
            g_end = gcum[chunk - 1:chunk, :]
            k_end = (kk * jnp.exp(g_end - gcum)).astype(BF16)
            st_s[h] = st * jnp.exp(g_end) + _dot_tn(vv, k_end)
            on = o * lax.rsqrt(jnp.mean(o * o, axis=-1, keepdims=True) + RMS_EPS) * norm_g[:, cols]
            gate = jax.nn.sigmoid(g_ref[0, pl.ds(r0, chunk), cols].astype(F32))
            o_ref[0, pl.ds(r0, chunk), cols] = (on * gate).astype(o_ref.dtype)
        return carry

    lax.fori_loop(0, t // chunk, body, 0)


def _hgrn_core(q, f, i, g, lower, norm_g, bsz, t, layer_idx):
    heads = 4
    width = heads * HGRN_DIM
    n_grp = HGRN_HEADS // heads
    kernel = functools.partial(_hgrn_kernel, layer_idx=layer_idx, heads=heads)
    act = pl.BlockSpec((1, t, width), lambda b, j: (b, 0, j))
    return pl.pallas_call(
        kernel,
        grid=(bsz, n_grp),
        in_specs=[act, act, act, act,
                  pl.BlockSpec((DEPTH, width), lambda b, j: (0, j)),
                  pl.BlockSpec((1, width), lambda b, j: (0, j))],
        out_specs=act,
        out_shape=jax.ShapeDtypeStruct((bsz, t, D_MODEL), BF16),
        scratch_shapes=[pltpu.VMEM((heads, HGRN_DIM, HGRN_DIM), F32)],
        compiler_params=_cparams(("parallel", "parallel")),
        name="hgrn2",
    )(q, f, i, g, lower, norm_g.reshape(1, -1))


def _hgrn_mixer(x2, bsz, t, layer_idx, w_in, lower, norm_g, w_out, ln_g, ln_b):
    d = D_MODEL
    ws = [w_in[:, j * d:(j + 1) * d].astype(BF16) for j in range(4)]
    q, f, i, g = _proj(x2, ws, [BF16, F32, BF16, BF16])
    shp = (bsz, t, d)
    o = _hgrn_core(q.reshape(shp), f.reshape(shp), i.reshape(shp), g.reshape(shp), lower, norm_g, bsz, t, layer_idx)
    return _outproj_ln(o.reshape(bsz * t, d), w_out.astype(BF16), x2, ln_g, ln_b)


def _router_kernel(x_ref, rwt_ref, rb_ref, idx_ref, gate_ref, rank_ref, cnt_ref, carry_s):
    @pl.when(pl.program_id(0) == 0)
    def _():
        carry_s[...] = jnp.zeros(carry_s.shape, F32)

    tm = x_ref.shape[0]
    logits = _dot_nt(rwt_ref[...], x_ref[...], precision=HIGHEST) + rb_ref[...]
    e_iota = lax.broadcasted_iota(jnp.int32, (N_EXPERTS, tm), 0)
    work = logits
    vals, picks = [], []
    for k in range(TOP_K):
        mx = jnp.max(work, axis=0, keepdims=True)
        idx = jnp.min(jnp.where(work == mx, e_iota, N_EXPERTS), axis=0, keepdims=True)
        pick = e_iota == idx
        vals.append(mx)
        picks.append(pick)
        idx_ref[k:k + 1, :] = idx
        work = jnp.where(pick, -jnp.inf, work)
    exps = [jnp.exp(v - vals[0]) for v in vals]
    tot = exps[0]
    for k in range(1, TOP_K):
        tot = tot + exps[k]
    for k in range(TOP_K):
        gate_ref[k:k + 1, :] = exps[k] / tot
    hot = picks[0].astype(F32)
    for k in range(1, TOP_K):
        hot = hot + picks[k].astype(F32)
    before = (lax.broadcasted_iota(jnp.int32, (tm, tm), 0) < lax.broadcasted_iota(jnp.int32, (tm, tm), 1)).astype(BF16)
    prior = _dot(hot.astype(BF16), before) + carry_s[...]
    for k in range(TOP_K):
        rank_ref[k:k + 1, :] = jnp.sum(jnp.where(picks[k], prior, 0.0), axis=0, keepdims=True).astype(jnp.int32)
    total = carry_s[...] + jnp.sum(hot, axis=1, keepdims=True)
    carry_s[...] = total
    cnt_ref[...] = total.astype(jnp.int32)


def _router(x2, router_w, router_b):
    n, d = x2.shape
    tm = min(ROUTER_TM, n)
    row = pl.BlockSpec((TOP_K, tm), lambda i: (0, i))
    return pl.pallas_call(
        _router_kernel,
        grid=(n // tm,),
        in_specs=[
            pl.BlockSpec((tm, d), lambda i: (i, 0)),
            pl.BlockSpec((N_EXPERTS, d), lambda i: (0, 0)),
            pl.BlockSpec((N_EXPERTS, 1), lambda i: (0, 0)),
        ],
        out_specs=[row, row, row, pl.BlockSpec((N_EXPERTS, 1), lambda i: (0, 0))],
        out_shape=[
            jax.ShapeDtypeStruct((TOP_K, n), jnp.int32),
            jax.ShapeDtypeStruct((TOP_K, n), F32),
            jax.ShapeDtypeStruct((TOP_K, n), jnp.int32),
            jax.ShapeDtypeStruct((N_EXPERTS, 1), jnp.int32),
        ],
        scratch_shapes=[pltpu.VMEM((N_EXPERTS, 1), F32)],
        compiler_params=_cparams(("arbitrary",)),
        name="router",
    )(x2, router_w.T, router_b.reshape(N_EXPERTS, 1))


def _row_copy(src, s, dst, d, sem):
    return pltpu.make_async_copy(src.at[pl.ds(s, 1), :], dst.at[pl.ds(d, 1), :], sem)


def _dispatch_kernel(pstart_ref, cnt_ref, idx_ref, rank_ref, x_ref, xs_ref, zero_s, sem):
    tb = x_ref.shape[0]

    @pl.when(pl.program_id(0) == 0)
    def _():
        zero_s[...] = jnp.zeros(zero_s.shape, F32)
        for fill in (True, False):
            def per_expert(e, _):
                def per_row(r, _):
                    cp = _row_copy(zero_s, 0, xs_ref, r, sem)
                    cp.start() if fill else cp.wait()
                    return 0
                return lax.fori_loop(pstart_ref[e] + cnt_ref[e], pstart_ref[e + 1], per_row, 0)
            lax.fori_loop(0, N_EXPERTS, per_expert, 0)

    def issue(t, _):
        for k in range(TOP_K):
            dest = pstart_ref[idx_ref[k, t]] + rank_ref[k, t]
            _row_copy(x_ref, t, xs_ref, dest, sem).start()
        return 0

    def drain(t, _):
        for k in range(TOP_K):
            _row_copy(x_ref, t, xs_ref, 0, sem).wait()
        return 0

    lax.fori_loop(0, tb, issue, 0)
    lax.fori_loop(0, tb, drain, 0)


def _dispatch(x2, idx, rank, pstart, counts, rows):
    n, d = x2.shape
    tb = min(DISPATCH_TB, n)
    smem_row = pl.BlockSpec((TOP_K, tb), lambda i, *_: (0, i), memory_space=pltpu.SMEM)
    return pl.pallas_call(
        _dispatch_kernel,
        grid_spec=pltpu.PrefetchScalarGridSpec(
            num_scalar_prefetch=2,
            grid=(n // tb,),
            in_specs=[smem_row, smem_row, pl.BlockSpec((tb, d), lambda i, *_: (i, 0))],
            out_specs=pl.BlockSpec(memory_space=pl.ANY),
            scratch_shapes=[pltpu.VMEM((8, d), F32), pltpu.SemaphoreType.DMA(())],
        ),
        out_shape=jax.ShapeDtypeStruct((rows, d), F32),
        compiler_params=_cparams(("arbitrary",)),
        name="moe_dispatch",
    )(pstart, counts, idx, rank, x2)


def _expert_kernel(be_ref, nused_ref, x_ref, wgu_ref, bgu_ref, wd_ref, bd_ref, y_ref, wgu_s, wd_s):
    i = pl.program_id(0)
    used = i < nused_ref[0]

    @pl.when(used & ((i == 0) | (be_ref[i] != be_ref[jnp.maximum(i - 1, 0)])))
    def _():
        wgu_s[...] = wgu_ref[0].astype(BF16)
        wd_s[...] = wd_ref[0].astype(BF16)

    @pl.when(used)
    def _():
        xb = x_ref[...].astype(BF16)
        h = _dot(xb, wgu_s[...]) + bgu_ref[0]
        h_gate = jnp.minimum(h[:, :D_FF], SWIGLU_LIMIT)
        h_up = jnp.clip(h[:, D_FF:], -SWIGLU_LIMIT, SWIGLU_LIMIT)
        act = (h_up + 1.0) * h_gate * jax.nn.sigmoid(SWIGLU_ALPHA * h_gate)
        y_ref[...] = _dot(act.astype(BF16), wd_s[...]) + bd_ref[0]

    @pl.when(jnp.logical_not(used))
    def _():
        y_ref[...] = jnp.zeros(y_ref.shape, F32)


def _experts(xs, block_expert, n_used, w_gu, b_gu, w_down, b_down):
    rows, d = xs.shape
    n_blocks = rows // MOE_BLOCK
    f2 = w_gu.shape[2]
    xmap = lambda i, be, nu: (jnp.minimum(i, nu[0] - 1), 0)
    return pl.pallas_call(
        _expert_kernel,
        grid_spec=pltpu.PrefetchScalarGridSpec(
            num_scalar_prefetch=2,
            grid=(n_blocks,),
            in_specs=[
                pl.BlockSpec((MOE_BLOCK, d), xmap),
                pl.BlockSpec((1, d, f2), lambda i, be, nu: (be[i], 0, 0)),
                pl.BlockSpec((1, 1, f2), lambda i, be, nu: (be[i], 0, 0)),
                pl.BlockSpec((1, D_FF, d), lambda i, be, nu: (be[i], 0, 0)),
                pl.BlockSpec((1, 1, d), lambda i, be, nu: (be[i], 0, 0)),
            ],
            out_specs=pl.BlockSpec((MOE_BLOCK, d), lambda i, be, nu: (i, 0)),
            scratch_shapes=[pltpu.VMEM((d, f2), BF16), pltpu.VMEM((D_FF, d), BF16)],
        ),
        out_shape=jax.ShapeDtypeStruct((rows, d), F32),
        compiler_params=_cparams(("arbitrary",)),
        name="moe_experts",
    )(block_expert, n_used, xs, w_gu, b_gu.reshape(N_EXPERTS, 1, f2), w_down, b_down.reshape(N_EXPERTS, 1, d))


def _combine_kernel(pstart_ref, idx_ref, rank_ref, gate_ref, x_ref, g_ref, b_ref, y_ref, o_ref, buf_s, sem):
    tb = x_ref.shape[0]

    def issue(t, _):
        for k in range(TOP_K):
            src = pstart_ref[idx_ref[k, t]] + rank_ref[k, t]
            _row_copy(y_ref, src, buf_s.at[k], t, sem).start()
        return 0

    def drain(t, _):
        for k in range(TOP_K):
            _row_copy(y_ref, 0, buf_s.at[k], t, sem).wait()
        return 0

    lax.fori_loop(0, tb, issue, 0)
    lax.fori_loop(0, tb, drain, 0)
    gates = gate_ref[...]
    moe = gates[:, 0:1] * buf_s[0]
    for k in range(1, TOP_K):
        moe = moe + gates[:, k:k + 1] * buf_s[k]
    o_ref[...] = _layer_norm(DEEPNORM_ALPHA * x_ref[...] + moe, g_ref[...], b_ref[...])


def _combine(y, x2, idx, rank, gate_col, pstart, ln_g, ln_b):
    n, d = x2.shape
    tb = min(COMBINE_TB, n)
    smem_row = pl.BlockSpec((TOP_K, tb), lambda i, *_: (0, i), memory_space=pltpu.SMEM)
    return pl.pallas_call(
        _combine_kernel,
        grid_spec=pltpu.PrefetchScalarGridSpec(
            num_scalar_prefetch=1,
            grid=(n // tb,),
            in_specs=[
                smem_row, smem_row,
                pl.BlockSpec((tb, TOP_K), lambda i, *_: (i, 0)),
                pl.BlockSpec((tb, d), lambda i, *_: (i, 0)),
                pl.BlockSpec((1, d), lambda i, *_: (0, 0)),
                pl.BlockSpec((1, d), lambda i, *_: (0, 0)),
                pl.BlockSpec(memory_space=pl.ANY),
            ],
            out_specs=pl.BlockSpec((tb, d), lambda i, *_: (i, 0)),
            scratch_shapes=[pltpu.VMEM((TOP_K, tb, d), F32), pltpu.SemaphoreType.DMA(())],
        ),
        out_shape=jax.ShapeDtypeStruct((n, d), F32),
        compiler_params=_cparams(("arbitrary",)),
        name="moe_combine",
    )(pstart, idx, rank, gate_col, x2, ln_g.reshape(1, d), ln_b.reshape(1, d), y)


def _moe_ffn_ln(x2, router_w, router_b, w_gu, b_gu, w_down, b_down, ln_g, ln_b):
    n, _ = x2.shape
    idx, gate, rank, counts = _router(x2, router_w, router_b)
    counts = counts.reshape(N_EXPERTS)
    n_blocks = (n * TOP_K + N_EXPERTS * (MOE_BLOCK - 1) + MOE_BLOCK - 1) // MOE_BLOCK
    padded = (counts + MOE_BLOCK - 1) // MOE_BLOCK * MOE_BLOCK
    padded_end = jnp.cumsum(padded)
    pstart = jnp.concatenate([jnp.zeros((1,), jnp.int32), padded_end]).astype(jnp.int32)
    n_used = (padded_end[-1:] // MOE_BLOCK).astype(jnp.int32)
    block_start = jnp.arange(n_blocks, dtype=jnp.int32) * MOE_BLOCK
    block_expert = jnp.sum(block_start[:, None] >= padded_end[None, :], axis=1).astype(jnp.int32)
    last_expert = jnp.sum(padded_end[-1] - 1 >= padded_end).astype(jnp.int32)
    block_expert = jnp.minimum(block_expert, last_expert)
    xs = _dispatch(x2, idx, rank, pstart, counts, n_blocks * MOE_BLOCK)
    y = _experts(xs, block_expert, n_used, w_gu, b_gu, w_down, b_down)
    return _combine(y, x2, idx, rank, gate.T, pstart, ln_g, ln_b)


def kernel(x, ln_g, ln_b, nsa_w_in, nsa_cmp_pos, nsa_cmp_w1, nsa_cmp_b1, nsa_cmp_w2, nsa_cmp_b2, nsa_gate_b, nsa_w_out, ml_w_in, ml_conv_w, ml_conv_b, ml_gate_b, ml_norm_g, ml_w_out, hg_w_in, hg_lower, hg_norm_g, hg_w_out, router_w, router_b, moe_w_gu, moe_b_gu, moe_w_down, moe_b_down):
    bsz, t, d = x.shape
    x2 = x.reshape(bsz * t, d)
    for layer in range(DEPTH):
        kind, slot = layer % N_MIXERS, layer // N_MIXERS
        if kind == 0:
            x2 = _nsa_mixer(x2, bsz, t, nsa_w_in[slot], nsa_cmp_pos[slot], nsa_cmp_w1[slot], nsa_cmp_b1[slot],
                            nsa_cmp_w2[slot], nsa_cmp_b2[slot], nsa_gate_b[slot], nsa_w_out[slot],
                            ln_g[layer, 0], ln_b[layer, 0])
        elif kind == 1:
            x2 = _mlstm_mixer(x2, bsz, t, ml_w_in[slot], ml_conv_w[slot], ml_conv_b[slot], ml_gate_b[slot],
                              ml_norm_g[slot], ml_w_out[slot], ln_g[layer, 0], ln_b[layer, 0])
        else:
            x2 = _hgrn_mixer(x2, bsz, t, layer, hg_w_in[slot], hg_lower, hg_norm_g[slot], hg_w_out[slot],
                             ln_g[layer, 0], ln_b[layer, 0])
        x2 = _moe_ffn_ln(x2, router_w[layer], router_b[layer], moe_w_gu[layer], moe_b_gu[layer],
                         moe_w_down[layer], moe_b_down[layer], ln_g[layer, 1], ln_b[layer, 1])
    return x2.reshape(bsz, t, d)
```

```python
import functools

import numpy as np
import jax
import jax.numpy as jnp
from jax import lax
from jax.experimental import pallas as pl
from jax.experimental.pallas import tpu as pltpu

F32 = jnp.float32
BF16 = jnp.bfloat16
HIGHEST = lax.Precision.HIGHEST

D_MODEL = 1024
DEPTH = 4
N_MIXERS = 3

NSA_HEADS = 16
NSA_KV_GROUPS = 4
NSA_HEAD_DIM = 64
NSA_HPG = 4
NSA_KV_WIDTH = 256
CMP_BLOCK = 32
CMP_STRIDE = 16
CMP_HIDDEN = 256
SEL_BLOCK = 64
SEL_TOPK = 8
WINDOW = 512

MLSTM_HEADS = 4
MLSTM_QK_DIM = 128
MLSTM_V_DIM = 256
MLSTM_CONV = 4

HGRN_HEADS = 8
HGRN_DIM = 128
HGRN_CHUNK = 32

N_EXPERTS = 32
TOP_K = 4
D_FF = 1024
SWIGLU_LIMIT = 7.0
SWIGLU_ALPHA = 1.702

LN_EPS = 1e-5
RMS_EPS = 1e-6
DEEPNORM_ALPHA = (2 * DEPTH) ** 0.25
NEG = -1e30
M_FLOOR = -1e20

LANE = 128
VMEM_LIMIT = 56 * 1024 * 1024

PROJ_TM = 512
MOE_TILE = 512
MOE_BLOCK = 256
ROW_ALIGN = 8
SORT_ROWS = 256
NSA_TQ = 128
NSA_TK = 512
MLSTM_L = 128


def _cparams(sem):
    return pltpu.CompilerParams(dimension_semantics=sem, vmem_limit_bytes=VMEM_LIMIT)


def _dot(a, b):
    return jnp.dot(a, b, preferred_element_type=F32)


def _dot_nt(a, b, precision=None):
    return lax.dot_general(a, b, (((1,), (1,)), ((), ())), precision=precision, preferred_element_type=F32)


def _dot_tn(a, b):
    return lax.dot_general(a, b, (((0,), (0,)), ((), ())), preferred_element_type=F32)


def _layer_norm(z, g, b):
    mu = jnp.mean(z, axis=-1, keepdims=True)
    zc = z - mu
    var = jnp.mean(zc * zc, axis=-1, keepdims=True)
    return zc * lax.rsqrt(var + LN_EPS) * g + b


def _proj_kernel(x_ref, *refs, n_w, n_t):
    w_refs = refs[:n_w]
    wt_refs = refs[n_w:n_w + n_t]
    o_refs = refs[n_w + n_t:2 * n_w + n_t]
    ot_refs = refs[2 * n_w + n_t:]
    x = x_ref[...]
    xb = x.astype(BF16)
    for w_ref, o_ref in zip(w_refs, o_refs):
        ncol = w_ref.shape[1]
        for c0 in range(0, ncol, 512):
            c1 = min(c0 + 512, ncol)
            o_ref[:, c0:c1] = _dot(xb, w_ref[:, c0:c1]).astype(o_ref.dtype)
    for wt_ref, o_ref in zip(wt_refs, ot_refs):
        o_ref[...] = _dot_nt(wt_ref[...], x, precision=HIGHEST)


def _proj(x2, weights, out_dtypes, weights_t=()):
    n, d = x2.shape
    tm = min(PROJ_TM, n)
    in_specs = [pl.BlockSpec((tm, d), lambda i: (i, 0))]
    in_specs += [pl.BlockSpec(w.shape, lambda i: (0, 0)) for w in weights]
    in_specs += [pl.BlockSpec(w.shape, lambda i: (0, 0)) for w in weights_t]
    out_shape = [jax.ShapeDtypeStruct((n, w.shape[1]), dt) for w, dt in zip(weights, out_dtypes)]
    out_shape += [jax.ShapeDtypeStruct((w.shape[0], n), F32) for w in weights_t]
    out_specs = [pl.BlockSpec((tm, w.shape[1]), lambda i: (i, 0)) for w in weights]
    out_specs += [pl.BlockSpec((w.shape[0], tm), lambda i: (0, i)) for w in weights_t]
    return pl.pallas_call(
        functools.partial(_proj_kernel, n_w=len(weights), n_t=len(weights_t)),
        grid=(n // tm,),
        in_specs=in_specs,
        out_specs=out_specs,
        out_shape=out_shape,
        compiler_params=_cparams(("parallel",)),
        name="proj",
    )(x2, *weights, *weights_t)


def _outproj_ln_kernel(h_ref, w_ref, x_ref, g_ref, b_ref, o_ref):
    y = _dot(h_ref[...], w_ref[...])
    o_ref[...] = _layer_norm(DEEPNORM_ALPHA * x_ref[...] + y, g_ref[...], b_ref[...])


def _outproj_ln(h2, w, x2, g, b):
    n, d = x2.shape
    tm = min(PROJ_TM, n)
    return pl.pallas_call(
        _outproj_ln_kernel,
        grid=(n // tm,),
        in_specs=[
            pl.BlockSpec((tm, d), lambda i: (i, 0)),
            pl.BlockSpec((d, d), lambda i: (0, 0)),
            pl.BlockSpec((tm, d), lambda i: (i, 0)),
            pl.BlockSpec((1, d), lambda i: (0, 0)),
            pl.BlockSpec((1, d), lambda i: (0, 0)),
        ],
        out_specs=pl.BlockSpec((tm, d), lambda i: (i, 0)),
        out_shape=jax.ShapeDtypeStruct((n, d), F32),
        compiler_params=_cparams(("parallel",)),
        name="outproj_ln",
    )(h2, w, x2, g.reshape(1, d), b.reshape(1, d))


def _gelu_tanh(x):
    return 0.5 * x * (1.0 + jnp.tanh(np.sqrt(2.0 / np.pi).astype(np.float32) * (x + 0.044715 * (x * x * x))))


def _nsa_compress_kernel(kv_ref, wcat_ref, posa_ref, posb_ref, b1_ref, w2_ref, b2_ref, o_ref):
    n_row = kv_ref.shape[1] // CMP_STRIDE
    half = 2 * CMP_HIDDEN
    acc_a = jnp.zeros((n_row, half), F32)
    acc_b = jnp.zeros((n_row, half), F32)
    for r in range(CMP_STRIDE):
        xr = kv_ref[0, pl.ds(r, n_row, stride=CMP_STRIDE), :]
        w = wcat_ref[r]
        acc_a = acc_a + _dot((xr + posa_ref[r]).astype(BF16), w[:, :half])
        acc_b = acc_b + _dot((xr + posb_ref[r]).astype(BF16), w[:, half:])
    hidden = acc_a + pltpu.roll(acc_b, n_row - 1, 0) + b1_ref[...]
    hidden = _gelu_tanh(hidden)
    o_ref[0, 0] = (_dot(hidden.astype(BF16), w2_ref[...]) + b2_ref[...]).astype(o_ref.dtype)


def _nsa_compress(kvc, wcat, posa, posb, b1cat, w2cat, b2cat):
    bsz, t, _ = kvc.shape
    n_row = t // CMP_STRIDE
    return pl.pallas_call(
        _nsa_compress_kernel,
        grid=(bsz, NSA_KV_GROUPS),
        in_specs=[
            pl.BlockSpec((1, t, LANE), lambda b, g: (b, 0, g)),
            pl.BlockSpec(wcat.shape, lambda b, g: (0, 0, 0)),
            pl.BlockSpec(posa.shape, lambda b, g: (0, 0, 0)),
            pl.BlockSpec(posb.shape, lambda b, g: (0, 0, 0)),
            pl.BlockSpec(b1cat.shape, lambda b, g: (0, 0)),
            pl.BlockSpec(w2cat.shape, lambda b, g: (0, 0)),
            pl.BlockSpec(b2cat.shape, lambda b, g: (0, 0)),
        ],
        out_specs=pl.BlockSpec((1, 1, n_row, LANE), lambda b, g: (b, g, 0, 0)),
        out_shape=jax.ShapeDtypeStruct((bsz, NSA_KV_GROUPS, n_row, LANE), BF16),
        compiler_params=_cparams(("parallel", "parallel")),
        name="nsa_compress",
    )(kvc, wcat, posa, posb, b1cat, w2cat, b2cat)


def _nsa_attn_kernel(q_ref, kvc_ref, kvs_ref, kvw_ref, gate_ref, gb_ref, c2s_ref, o_ref, *, tq, tk, n_cmp):
    hpg = NSA_HPG
    t0 = pl.program_id(2) * tq
    q4 = jnp.concatenate([q_ref[0, :, h * LANE:(h + 1) * LANE] for h in range(hpg)], axis=0)
    q4 = q4 * jnp.asarray(NSA_HEAD_DIM ** -0.5, BF16)
    t_col = t0 + lax.broadcasted_iota(jnp.int32, (tq, 1), 0)

    kvc = kvc_ref[0, 0]
    n_c = kvc.shape[0]
    s_c = _dot_nt(q4, kvc)
    c_lane = lax.broadcasted_iota(jnp.int32, (1, n_c), 1)
    valid_c = (c_lane * CMP_STRIDE + (CMP_BLOCK - 1) <= t_col) & (c_lane < n_cmp)
    p_heads = []
    for h in range(hpg):
        sm = jnp.where(valid_c, s_c[h * tq:(h + 1) * tq], NEG)
        m = jnp.max(sm, axis=-1, keepdims=True)
        p = jnp.where(valid_c, jnp.exp(sm - m), 0.0)
        l = jnp.sum(p, axis=-1, keepdims=True)
        p_heads.append(p / jnp.maximum(l, 1e-30))
    o_c = _dot(jnp.concatenate(p_heads, axis=0).astype(BF16), kvc)
    p_sum = p_heads[0]
    for h in range(1, hpg):
        p_sum = p_sum + p_heads[h]
    imp = _dot_nt(c2s_ref[...], p_sum, precision=HIGHEST)

    n_sel = imp.shape[0]
    blk = lax.broadcasted_iota(jnp.int32, (n_sel, 1), 0)
    cur = (t0 + lax.broadcasted_iota(jnp.int32, (1, tq), 1)) // SEL_BLOCK
    forced = (blk == 0) | (blk == cur) | (blk == cur - 1)
    score = jnp.where(forced, -NEG, jnp.where(blk <= cur, imp, NEG))
    rank = jnp.zeros((n_sel, tq), F32)
    for i in range(n_sel):
        s_i = score[i:i + 1, :]
        ahead = (s_i > score) | ((s_i == score) & (blk > i))
        rank = rank + ahead.astype(F32)
    member = ((rank < SEL_TOPK) & (score > 0.5 * NEG)).astype(BF16)

    lane = lax.broadcasted_iota(jnp.int32, (1, LANE), 1)
    ones_lane = (lane == 0).astype(BF16)

    def with_ones(kv):
        return jnp.where(lane < NSA_HEAD_DIM, ones_lane, kv)

    blk_per_tile = tk // SEL_BLOCK
    key_lane = lax.broadcasted_iota(jnp.int32, (1, tk), 1)

    def sel_body(kt, carry):
        k0 = pl.multiple_of(kt * tk, tk)
        kv = kvs_ref[0, pl.ds(k0, tk), :]
        s = _dot_nt(q4, kv)
        expand = (key_lane // SEL_BLOCK + kt * blk_per_tile == blk).astype(BF16)
        valid = (_dot_tn(member, expand) > 0.5) & (k0 + key_lane <= t_col)
        new = []
        pbs = []
        for h in range(hpg):
            m_old, acc_old = carry[h]
            sm = jnp.where(valid, s[h * tq:(h + 1) * tq], NEG)
            m_new = jnp.maximum(m_old, jnp.max(sm, axis=-1, keepdims=True))
            pbs.append(jnp.exp(sm - m_new).astype(BF16))
            new.append((m_new, jnp.exp(m_old - m_new) * acc_old))
        pv = _dot(jnp.concatenate(pbs, axis=0), with_ones(kv))
        return tuple((new[h][0], new[h][1] + pv[h * tq:(h + 1) * tq]) for h in range(hpg))

    init = tuple((jnp.full((tq, 1), M_FLOOR, F32), jnp.zeros((tq, LANE), F32)) for _ in range(hpg))
    n_kt = (t0 + tq - 1) // tk + 1
    sel = lax.fori_loop(0, n_kt, sel_body, init)

    span = WINDOW + tq
    w0 = pl.multiple_of(jnp.maximum(t0 - WINDOW, 0), tq)
    kvw = kvw_ref[0, pl.ds(w0, span), :]
    s_w = _dot_nt(q4, kvw)
    lag = t_col - (w0 + lax.broadcasted_iota(jnp.int32, (1, span), 1))
    valid_w = (lag >= 0) & (lag < WINDOW)
    pws = []
    for h in range(hpg):
        sm = jnp.where(valid_w, s_w[h * tq:(h + 1) * tq], NEG)
        m = jnp.max(sm, axis=-1, keepdims=True)
        pws.append(jnp.exp(sm - m).astype(BF16))
    o_w = _dot(jnp.concatenate(pws, axis=0), with_ones(kvw))

    gates = jax.nn.sigmoid(gate_ref[0].astype(F32) + gb_ref[0])
    outs = []
    for h in range(hpg):
        rows = slice(h * tq, (h + 1) * tq)
        o_s = sel[h][1]
        outs.append(gates[:, 3 * h:3 * h + 1] * o_c[rows]
                    + gates[:, 3 * h + 1:3 * h + 2] * (o_s / o_s[:, 0:1])
                    + gates[:, 3 * h + 2:3 * h + 3] * (o_w[rows] / o_w[rows][:, 0:1]))
    for j in range(hpg // 2):
        pair = jnp.where(lane < NSA_HEAD_DIM, pltpu.roll(outs[2 * j], NSA_HEAD_DIM, 1), outs[2 * j + 1])
        o_ref[0, :, j * LANE:(j + 1) * LANE] = pair.astype(o_ref.dtype)


def _nsa_attention(main, kvcmp, gate_b, c2s, bsz, t):
    tq = min(NSA_TQ, t)
    tk = min(NSA_TK, t)
    n_cmp = (t - CMP_BLOCK) // CMP_STRIDE + 1
    q_blocks = NSA_HEADS
    g = NSA_KV_GROUPS
    kernel = functools.partial(_nsa_attn_kernel, tq=tq, tk=tk, n_cmp=n_cmp)
    return pl.pallas_call(
        kernel,
        grid=(bsz, g, t // tq),
        in_specs=[
            pl.BlockSpec((1, tq, NSA_HPG * LANE), lambda b, gi, qi: (b, qi, gi)),
            pl.BlockSpec((1, 1) + kvcmp.shape[2:], lambda b, gi, qi: (b, gi, 0, 0)),
            pl.BlockSpec((1, t, LANE), lambda b, gi, qi: (b, 0, q_blocks + gi)),
            pl.BlockSpec((1, t, LANE), lambda b, gi, qi: (b, 0, q_blocks + g + gi)),
            pl.BlockSpec((1, tq, LANE), lambda b, gi, qi: (b, qi, q_blocks + 2 * g + gi)),
            pl.BlockSpec((1, 1, LANE), lambda b, gi, qi: (gi, 0, 0)),
            pl.BlockSpec(c2s.shape, lambda b, gi, qi: (0, 0)),
        ],
        out_specs=pl.BlockSpec((1, tq, NSA_HPG * NSA_HEAD_DIM), lambda b, gi, qi: (b, qi, gi)),
        out_shape=jax.ShapeDtypeStruct((bsz, t, D_MODEL), BF16),
        compiler_params=_cparams(("parallel", "parallel", "arbitrary")),
        name="nsa_attention",
    )(main, kvcmp, main, main, main, gate_b, c2s)


def _nsa_mixer(x2, bsz, t, w_in, cmp_pos, cmp_w1, cmp_b1, cmp_w2, cmp_b2, gate_b, w_out, ln_g, ln_b):
    d, g, dh, hpg = D_MODEL, NSA_KV_GROUPS, NSA_HEAD_DIM, NSA_HPG
    kvw_ = NSA_KV_WIDTH

    def pair(k0):
        k = w_in[:, k0:k0 + kvw_].reshape(d, g, 1, dh)
        v = w_in[:, k0 + kvw_:k0 + 2 * kvw_].reshape(d, g, 1, dh)
        return jnp.concatenate([k, v], axis=2).reshape(d, g * LANE)

    wq = jnp.pad(w_in[:, :d].reshape(d, NSA_HEADS, dh), ((0, 0), (0, 0), (0, LANE - dh))).reshape(d, NSA_HEADS * LANE)
    n_gate = 3 * hpg
    wg = jnp.pad(w_in[:, d + 6 * kvw_:].reshape(d, g, n_gate), ((0, 0), (0, 0), (0, LANE - n_gate))).reshape(d, g * LANE)
    w_main = jnp.concatenate([wq, pair(d + 2 * kvw_), pair(d + 4 * kvw_), wg], axis=1).astype(BF16)
    w_kvc = pair(d).astype(BF16)
    main, kvc = _proj(x2, [w_main, w_kvc], [BF16, F32])

    half = CMP_BLOCK // 2
    w1 = cmp_w1.reshape(2, CMP_BLOCK, dh, CMP_HIDDEN)
    z = jnp.zeros((half, dh, CMP_HIDDEN), F32)
    top = jnp.concatenate([w1[0, :half], z, w1[0, half:], z], axis=2)
    bot = jnp.concatenate([z, w1[1, :half], z, w1[1, half:]], axis=2)
    wcat = jnp.concatenate([top, bot], axis=1).astype(BF16)
    pos = jnp.concatenate([cmp_pos[0], cmp_pos[1]], axis=-1)
    posa, posb = pos[:half, None, :], pos[half:, None, :]
    b1cat = cmp_b1.reshape(1, 2 * CMP_HIDDEN)
    zz = jnp.zeros((CMP_HIDDEN, dh), F32)
    w2cat = jnp.concatenate([jnp.concatenate([cmp_w2[0], zz], axis=1),
                             jnp.concatenate([zz, cmp_w2[1]], axis=1)], axis=0).astype(BF16)
    b2cat = cmp_b2.reshape(1, 2 * dh)
    kvcmp = _nsa_compress(kvc.reshape(bsz, t, g * LANE), wcat, posa, posb, b1cat, w2cat, b2cat)

    n_row = t // CMP_STRIDE
    n_sel = t // SEL_BLOCK
    cmp_start = np.arange(n_row) * CMP_STRIDE
    sel_start = np.arange(n_sel) * SEL_BLOCK
    overlap = (np.minimum(cmp_start[:, None] + CMP_BLOCK, sel_start[None, :] + SEL_BLOCK)
               - np.maximum(cmp_start[:, None], sel_start[None, :]))
    c2s = jnp.asarray((np.clip(overlap, 0, None) / CMP_STRIDE).astype(np.float32).T)
    gb = jnp.pad(gate_b.reshape(g, 1, n_gate), ((0, 0), (0, 0), (0, LANE - n_gate)))
    o = _nsa_attention(main.reshape(bsz, t, -1), kvcmp, gb, c2s, bsz, t)
    return _outproj_ln(o.reshape(bsz * t, d), w_out.astype(BF16), x2, ln_g, ln_b)


def _mlstm_kernel(q_ref, k_ref, v_ref, og_ref, gif_ref, gb_ref, cwq_ref, cwk_ref, cbq_ref, cbk_ref, ng_ref,
                  o_ref, pad_s, q_s, k_s, ct_s, *, chunk):
    t = q_ref.shape[1]
    dk, dv = MLSTM_QK_DIM, MLSTM_V_DIM
    head = pl.program_id(1)
    halo = 8

    def conv_silu(x_ref, w_ref, b_ref, dst, scale):
        pad_s[0:halo, :] = jnp.zeros((halo, dk), F32)
        pad_s[halo:halo + t, :] = x_ref[0]
        rows = min(t, 256)
        for r0 in range(0, t, rows):
            y = b_ref[...] + w_ref[0:1, :] * pad_s[pl.ds(halo + r0 - (MLSTM_CONV - 1), rows), :]
            for j in range(1, MLSTM_CONV):
                y = y + w_ref[j:j + 1, :] * pad_s[pl.ds(halo + r0 - (MLSTM_CONV - 1) + j, rows), :]
            y = y * jax.nn.sigmoid(y)
            dst[r0:r0 + rows, :] = (y * scale).astype(dst.dtype)

    conv_silu(q_ref, cwq_ref, cbq_ref, q_s, 1.0)
    conv_silu(k_ref, cwk_ref, cbk_ref, k_s, dk ** -0.5)
    ct_s[...] = jnp.zeros((dk, dv), F32)

    row = lax.broadcasted_iota(jnp.int32, (chunk, chunk), 0)
    col = lax.broadcasted_iota(jnp.int32, (chunk, chunk), 1)
    causal = col <= row
    eye = col == row
    upper = (row <= col).astype(F32)
    b_i = gb_ref[pl.ds(head, 1), :]
    b_f = gb_ref[pl.ds(MLSTM_HEADS + head, 1), :]
    norm_g = ng_ref[...]

    def body(c, carry):
        n_row, m_prev = carry
        r0 = pl.multiple_of(c * chunk, chunk)
        qc = q_s[pl.ds(r0, chunk), :]
        kc = k_s[pl.ds(r0, chunk), :]
        vc = v_ref[0, pl.ds(r0, chunk), :]
        li_row = gif_ref[0, c, pl.ds(head, 1), :] + b_i
        zf = gif_ref[0, c, pl.ds(MLSTM_HEADS + head, 1), :] + b_f
        lf_row = jnp.minimum(zf, 0.0) - jnp.log1p(jnp.exp(-jnp.abs(zf)))
        b_row = jnp.dot(lf_row, upper, precision=HIGHEST, preferred_element_type=F32)
        b_col = jnp.sum(jnp.where(causal, lf_row, 0.0), axis=-1, keepdims=True)
        li_col = jnp.sum(jnp.where(eye, li_row, 0.0), axis=-1, keepdims=True)
        d_mat = jnp.where(causal, b_col - b_row + li_row, NEG)
        m_t = jnp.maximum(b_col + m_prev, jnp.max(d_mat, axis=-1, keepdims=True))
        w_inter = jnp.exp(b_col + m_prev - m_t)
        s = _dot_nt(qc, kc) * jnp.exp(d_mat - m_t)
        num = _dot(s.astype(BF16), vc) + w_inter * _dot(qc, ct_s[...].astype(BF16))
        den = jnp.sum(s, axis=-1, keepdims=True) + w_inter * jnp.sum(qc.astype(F32) * n_row, axis=-1, keepdims=True)
        h = num / jnp.maximum(jnp.abs(den), jnp.exp(-m_t))
        b_end = b_col[chunk - 1:chunk, :]
        decay = b_end - b_col + li_col
        m_new = jnp.maximum(b_end + m_prev, jnp.max(decay, axis=0, keepdims=True))
        w_k = jnp.exp(decay - m_new)
        scale = jnp.exp(b_end + m_prev - m_new)
        ct_s[...] = scale * ct_s[...] + _dot_tn(kc, (vc.astype(F32) * w_k).astype(BF16))
        n_new = scale * n_row + jnp.sum(w_k * kc.astype(F32), axis=0, keepdims=True)
        hn = h * lax.rsqrt(jnp.mean(h * h, axis=-1, keepdims=True) + RMS_EPS) * norm_g
        og = og_ref[0, pl.ds(r0, chunk), :].astype(F32)
        o_ref[0, pl.ds(r0, chunk), :] = (hn * jax.nn.sigmoid(og)).astype(o_ref.dtype)
        return n_new, m_new

    lax.fori_loop(0, t // chunk, body, (jnp.zeros((1, dk), F32), jnp.zeros((1, 1), F32)))


def _mlstm_core(qk, v, og, gif, gate_b, conv_w, conv_b, norm_g, bsz, t, chunk):
    hh, dk, dv = MLSTM_HEADS, MLSTM_QK_DIM, MLSTM_V_DIM
    nc = t // chunk
    kernel = functools.partial(_mlstm_kernel, chunk=chunk)
    return pl.pallas_call(
        kernel,
        grid=(bsz, hh),
        in_specs=[
            pl.BlockSpec((1, t, dk), lambda b, h: (b, 0, h)),
            pl.BlockSpec((1, t, dk), lambda b, h: (b, 0, hh + h)),
            pl.BlockSpec((1, t, dv), lambda b, h: (b, 0, h)),
            pl.BlockSpec((1, t, dv), lambda b, h: (b, 0, h)),
            pl.BlockSpec((1, nc, 2 * hh, chunk), lambda b, h: (b, 0, 0, 0)),
            pl.BlockSpec((2 * hh, 1), lambda b, h: (0, 0)),
            pl.BlockSpec((MLSTM_CONV, dk), lambda b, h: (0, h)),
            pl.BlockSpec((MLSTM_CONV, dk), lambda b, h: (0, hh + h)),
            pl.BlockSpec((1, dk), lambda b, h: (0, h)),
            pl.BlockSpec((1, dk), lambda b, h: (0, hh + h)),
            pl.BlockSpec((1, dv), lambda b, h: (0, h)),
        ],
        out_specs=pl.BlockSpec((1, t, dv), lambda b, h: (b, 0, h)),
        out_shape=jax.ShapeDtypeStruct((bsz, t, hh * dv), BF16),
        scratch_shapes=[
            pltpu.VMEM((t + 8, dk), F32),
            pltpu.VMEM((t, dk), BF16),
            pltpu.VMEM((t, dk), BF16),
            pltpu.VMEM((dk, dv), F32),
        ],
        compiler_params=_cparams(("parallel", "parallel")),
        name="mlstm",
    )(qk, qk, v, og, gif, gate_b.reshape(2 * hh, 1), conv_w, conv_w, conv_b.reshape(1, -1), conv_b.reshape(1, -1),
      norm_g.reshape(1, -1))


def _mlstm_mixer(x2, bsz, t, w_in, conv_w, conv_b, gate_b, norm_g, w_out, ln_g, ln_b):
    d, hh = D_MODEL, MLSTM_HEADS
    qkw = 2 * hh * MLSTM_QK_DIM
    vw = hh * MLSTM_V_DIM
    chunk = min(MLSTM_L, t)
    w_qk = w_in[:, :qkw].astype(BF16)
    w_v = w_in[:, qkw:qkw + vw].astype(BF16)
    w_og = w_in[:, qkw + vw:qkw + vw + d].astype(BF16)
    w_gif_t = w_in[:, qkw + vw + d:].T
    qk, v, og, gif_t = _proj(x2, [w_qk, w_v, w_og], [F32, BF16, BF16], [w_gif_t])
    gif = gif_t.reshape(2 * hh, bsz, t // chunk, chunk).transpose(1, 2, 0, 3)
    h = _mlstm_core(qk.reshape(bsz, t, qkw), v.reshape(bsz, t, vw), og.reshape(bsz, t, d), gif, gate_b,
                    conv_w, conv_b, norm_g, bsz, t, chunk)
    return _outproj_ln(h.reshape(bsz * t, d), w_out.astype(BF16), x2, ln_g, ln_b)


def _hgrn_kernel(q_ref, f_ref, i_ref, g_ref, low_ref, ng_ref, o_ref, st_s, *, layer_idx, heads):
    t = q_ref.shape[1]
    dh, chunk = HGRN_DIM, HGRN_CHUNK
    low = low_ref[...]
    e = jnp.exp(low - jnp.max(low, axis=0, keepdims=True))
    soft = e / jnp.sum(e, axis=0, keepdims=True)
    lb = jnp.zeros((1, heads * dh), F32)
    for r in range(1, layer_idx + 1):
        lb = lb + soft[r:r + 1, :]
    norm_g = ng_ref[...]
    st_s[...] = jnp.zeros(st_s.shape, F32)
    row = lax.broadcasted_iota(jnp.int32, (chunk, chunk), 0)
    col = lax.broadcasted_iota(jnp.int32, (chunk, chunk), 1)
    causal = col <= row
    lower = causal.astype(F32)

    def body(c, carry):
        r0 = pl.multiple_of(c * chunk, chunk)
        for h in range(heads):
            cols = slice(h * dh, (h + 1) * dh)
            lbh = lb[:, cols]
            f = lbh + (1.0 - lbh) * jax.nn.sigmoid(f_ref[0, pl.ds(r0, chunk), cols])
            gcum = jnp.dot(lower, jnp.log(f), precision=HIGHEST, preferred_element_type=F32)
            qv = q_ref[0, pl.ds(r0, chunk), cols].astype(F32)
            q_dec = (qv * jax.nn.sigmoid(qv) * jnp.exp(gcum)).astype(BF16)
            kk = 1.0 - f
            k_dec = (kk * jnp.exp(-gcum)).astype(BF16)
            vv = i_ref[0, pl.ds(r0, chunk), cols]
            a = jnp.where(causal, _dot_nt(q_dec, k_dec), 0.0)
            st = st_s[h]
            o = _dot(a.astype(BF16), vv) + _dot_nt(q_dec, st.astype(BF16))
            g_end = gcum[chunk - 1:chunk, :]
            k_end = (kk * jnp.exp(g_end - gcum)).astype(BF16)
            st_s[h] = st * jnp.exp(g_end) + _dot_tn(vv, k_end)
            on = o * lax.rsqrt(jnp.mean(o * o, axis=-1, keepdims=True) + RMS_EPS) * norm_g[:, cols]
            gate = jax.nn.sigmoid(g_ref[0, pl.ds(r0, chunk), cols].astype(F32))
            o_ref[0, pl.ds(r0, chunk), cols] = (on * gate).astype(o_ref.dtype)
        return carry

    lax.fori_loop(0, t // chunk, body, 0)


def _hgrn_core(q, f, i, g, lower, norm_g, bsz, t, layer_idx):
    heads = 4
    width = heads * HGRN_DIM
    n_grp = HGRN_HEADS // heads
    kernel = functools.partial(_hgrn_kernel, layer_idx=layer_idx, heads=heads)
    act = pl.BlockSpec((1, t, width), lambda b, j: (b, 0, j))
    return pl.pallas_call(
        kernel,
        grid=(bsz, n_grp),
        in_specs=[act, act, act, act,
                  pl.BlockSpec((DEPTH, width), lambda b, j: (0, j)),
                  pl.BlockSpec((1, width), lambda b, j: (0, j))],
        out_specs=act,
        out_shape=jax.ShapeDtypeStruct((bsz, t, D_MODEL), BF16),
        scratch_shapes=[pltpu.VMEM((heads, HGRN_DIM, HGRN_DIM), F32)],
        compiler_params=_cparams(("parallel", "parallel")),
        name="hgrn2",
    )(q, f, i, g, lower, norm_g.reshape(1, -1))


def _hgrn_mixer(x2, bsz, t, layer_idx, w_in, lower, norm_g, w_out, ln_g, ln_b):
    d = D_MODEL
    ws = [w_in[:, j * d:(j + 1) * d].astype(BF16) for j in range(4)]
    q, f, i, g = _proj(x2, ws, [BF16, F32, BF16, BF16])
    shp = (bsz, t, d)
    o = _hgrn_core(q.reshape(shp), f.reshape(shp), i.reshape(shp), g.reshape(shp), lower, norm_g, bsz, t, layer_idx)
    return _outproj_ln(o.reshape(bsz * t, d), w_out.astype(BF16), x2, ln_g, ln_b)


def _router_kernel(x_ref, rwt_ref, rb_ref, spos_ref, gate_ref, tinfo_ref, seg_ref, carry_s):
    @pl.when(pl.program_id(0) == 0)
    def _():
        carry_s[...] = jnp.zeros(carry_s.shape, F32)

    tm = x_ref.shape[0]
    ne = N_EXPERTS
    logits = _dot_nt(rwt_ref[...], x_ref[...], precision=HIGHEST) + rb_ref[...]
    e_iota = lax.broadcasted_iota(jnp.int32, (ne, tm), 0)
    work = logits
    vals, picks = [], []
    for k in range(TOP_K):
        mx = jnp.max(work, axis=0, keepdims=True)
        idx = jnp.min(jnp.where(work == mx, e_iota, ne), axis=0, keepdims=True)
        pick = e_iota == idx
        vals.append(mx)
        picks.append(pick)
        work = jnp.where(pick, -jnp.inf, work)
    exps = [jnp.exp(v - vals[0]) for v in vals]
    tot = exps[0]
    for k in range(1, TOP_K):
        tot = tot + exps[k]
    for k in range(TOP_K):
        gate_ref[k:k + 1, :] = exps[k] / tot
    hot = picks[0].astype(F32)
    for k in range(1, TOP_K):
        hot = hot + picks[k].astype(F32)
    before = (lax.broadcasted_iota(jnp.int32, (tm, tm), 0) < lax.broadcasted_iota(jnp.int32, (tm, tm), 1)).astype(BF16)
    prior = _dot(hot.astype(BF16), before)
    cnt = jnp.sum(hot, axis=1, keepdims=True)
    run = jnp.floor((cnt + (ROW_ALIGN - 1)) * (1.0 / ROW_ALIGN)) * ROW_ALIGN
    sub = lax.broadcasted_iota(jnp.int32, (ne, ne), 0)
    lan = lax.broadcasted_iota(jnp.int32, (ne, ne), 1)
    run_row = jnp.sum(jnp.where(sub == lan, run, 0.0), axis=0, keepdims=True)
    soff = jnp.sum(jnp.where(lan < sub, run_row, 0.0), axis=1, keepdims=True)
    for k in range(TOP_K):
        pos = jnp.sum(jnp.where(picks[k], prior + soff, 0.0), axis=0, keepdims=True)
        spos_ref[k:k + 1, :] = pos.astype(jnp.int32)
    sub = lax.broadcasted_iota(jnp.int32, (ne, LANE), 0)
    lan = lax.broadcasted_iota(jnp.int32, (ne, LANE), 1)
    field, le = lan // ne, lan % ne
    carry = carry_s[...]
    info = (jnp.where((field == 0) & (sub == le), carry, 0.0) + jnp.where((field == 1) & (sub == le), run, 0.0)
            + jnp.where((field == 2) & (sub < le), run, 0.0))
    tinfo_ref[0] = jnp.sum(info, axis=0, keepdims=True).astype(jnp.int32)
    total = carry + run
    carry_s[...] = total
    seg_ref[...] = total.astype(jnp.int32)


def _router(x2, router_w, router_b):
    n, d = x2.shape
    tm = min(MOE_TILE, n)
    n_tiles = n // tm
    row = pl.BlockSpec((TOP_K, tm), lambda i: (0, i))
    return pl.pallas_call(
        _router_kernel,
        grid=(n_tiles,),
        in_specs=[
            pl.BlockSpec((tm, d), lambda i: (i, 0)),
            pl.BlockSpec((N_EXPERTS, d), lambda i: (0, 0)),
            pl.BlockSpec((N_EXPERTS, 1), lambda i: (0, 0)),
        ],
        out_specs=[row, row, pl.BlockSpec((1, 1, LANE), lambda i: (i, 0, 0)),
                   pl.BlockSpec((N_EXPERTS, 1), lambda i: (0, 0))],
        out_shape=[
            jax.ShapeDtypeStruct((TOP_K, n), jnp.int32),
            jax.ShapeDtypeStruct((TOP_K, n), F32),
            jax.ShapeDtypeStruct((n_tiles, 1, LANE), jnp.int32),
            jax.ShapeDtypeStruct((N_EXPERTS, 1), jnp.int32),
        ],
        scratch_shapes=[pltpu.VMEM((N_EXPERTS, 1), F32)],
        compiler_params=_cparams(("arbitrary",)),
        name="router",
    )(x2, router_w.T, router_b.reshape(N_EXPERTS, 1))


def _group_copy(src, s, dst, d, sem):
    return pltpu.make_async_copy(src.at[pl.ds(pl.multiple_of(s, ROW_ALIGN), ROW_ALIGN), :],
                                 dst.at[pl.ds(pl.multiple_of(d, ROW_ALIGN), ROW_ALIGN), :], sem)


def _for_each_group(tinfo_ref, base_ref, fn):
    def per_expert(e, _):
        seg_row = base_ref[e] + tinfo_ref[0, 0, e]
        groups = tinfo_ref[0, 0, N_EXPERTS + e] // ROW_ALIGN
        sorted_row = tinfo_ref[0, 0, 2 * N_EXPERTS + e]

        def per_group(j, _):
            fn(sorted_row + j * ROW_ALIGN, seg_row + j * ROW_ALIGN)
            return 0
        return lax.fori_loop(0, groups, per_group, 0)
    lax.fori_loop(0, N_EXPERTS, per_expert, 0)


def _dispatch_kernel(base_ref, seg_ref, tinfo_ref, spos_ref, gate_ref, x_ref, xs_ref, sort_s, zero_s, sem):
    tb, d = x_ref.shape
    s_rows = sort_s.shape[0]

    @pl.when(pl.program_id(0) == 0)
    def _():
        zero_s[...] = jnp.zeros(zero_s.shape, F32)
        for fill in (True, False):
            def per_expert(e, _):
                def per_group(r, _):
                    cp = _group_copy(zero_s, 0, xs_ref, r * ROW_ALIGN, sem)
                    cp.start() if fill else cp.wait()
                    return 0
                first = jnp.where(e < N_EXPERTS, base_ref[jnp.minimum(e, N_EXPERTS - 1)] + seg_ref[jnp.minimum(e, N_EXPERTS - 1)],
                                  base_ref[N_EXPERTS])
                last = jnp.where(e < N_EXPERTS, base_ref[jnp.minimum(e + 1, N_EXPERTS)], xs_ref.shape[0])
                return lax.fori_loop(first // ROW_ALIGN, last // ROW_ALIGN, per_group, 0)
            lax.fori_loop(0, N_EXPERTS + 1, per_expert, 0)

    xb = x_ref[...].astype(BF16)
    spos = spos_ref[...]
    gate = gate_ref[...]
    for r0 in range(0, s_rows, SORT_ROWS):
        r_iota = r0 + lax.broadcasted_iota(jnp.int32, (SORT_ROWS, tb), 0)
        hits = [spos[k:k + 1, :] == r_iota for k in range(TOP_K)]
        onehot = jnp.where(hits[0] | hits[1] | hits[2] | hits[3], 1.0, 0.0).astype(BF16)
        sort_s[r0:r0 + SORT_ROWS, :d] = _dot(onehot, xb)
        g_sel = jnp.where(hits[0], gate[0:1, :], 0.0)
        for k in range(1, TOP_K):
            g_sel = g_sel + jnp.where(hits[k], gate[k:k + 1, :], 0.0)
        sort_s[r0:r0 + SORT_ROWS, d:] = jnp.broadcast_to(jnp.sum(g_sel, axis=1, keepdims=True), (SORT_ROWS, LANE))

    _for_each_group(tinfo_ref, base_ref, lambda s, r: _group_copy(sort_s, s, xs_ref, r, sem).start())
    _for_each_group(tinfo_ref, base_ref, lambda s, r: _group_copy(sort_s, s, xs_ref, r, sem).wait())


def _tile_info_spec():
    return pl.BlockSpec((1, 1, LANE), lambda i, *_: (i, 0, 0), memory_space=pltpu.SMEM)


def _dispatch(x2, spos, gate, tinfo, base, seg, rows):
    n, d = x2.shape
    tb = min(MOE_TILE, n)
    row = pl.BlockSpec((TOP_K, tb), lambda i, *_: (0, i))
    return pl.pallas_call(
        _dispatch_kernel,
        grid_spec=pltpu.PrefetchScalarGridSpec(
            num_scalar_prefetch=2,
            grid=(n // tb,),
            in_specs=[_tile_info_spec(), row, row, pl.BlockSpec((tb, d), lambda i, *_: (i, 0))],
            out_specs=pl.BlockSpec(memory_space=pl.ANY),
            scratch_shapes=[pltpu.VMEM((_sorted_rows(tb), d + LANE), F32), pltpu.VMEM((ROW_ALIGN, d + LANE), F32),
                            pltpu.SemaphoreType.DMA(())],
        ),
        out_shape=jax.ShapeDtypeStruct((rows, d + LANE), F32),
        compiler_params=_cparams(("arbitrary",)),
        name="moe_dispatch",
    )(base, seg, tinfo, spos, gate, x2)


def _expert_kernel(be_ref, nused_ref, x_ref, wgu_ref, bgu_ref, wd_ref, bd_ref, y_ref, wgu_s, wd_s):
    i = pl.program_id(0)
    used = i < nused_ref[0]

    @pl.when(used & ((i == 0) | (be_ref[i] != be_ref[jnp.maximum(i - 1, 0)])))
    def _():
        wgu_s[...] = wgu_ref[0, 0].astype(BF16)
        wd_s[...] = wd_ref[0, 0].astype(BF16)

    @pl.when(used)
    def _():
        d = y_ref.shape[1]
        xb = x_ref[:, :d].astype(BF16)
        h = _dot(xb, wgu_s[...]) + bgu_ref[0, 0]
        h_gate = jnp.minimum(h[:, :D_FF], SWIGLU_LIMIT)
        h_up = jnp.clip(h[:, D_FF:], -SWIGLU_LIMIT, SWIGLU_LIMIT)
        act = (h_up + 1.0) * h_gate * jax.nn.sigmoid(SWIGLU_ALPHA * h_gate)
        y_ref[...] = (_dot(act.astype(BF16), wd_s[...]) + bd_ref[0, 0]) * x_ref[:, d:d + 1]

    @pl.when(jnp.logical_not(used))
    def _():
        y_ref[...] = jnp.zeros(y_ref.shape, F32)


def _experts(xs, block_expert, n_used, layer, w_gu, b_gu, w_down, b_down):
    rows = xs.shape[0]
    d = D_MODEL
    n_blocks = rows // MOE_BLOCK
    f2 = w_gu.shape[3]
    depth = w_gu.shape[0]
    xmap = lambda i, be, nu: (jnp.minimum(i, nu[0] - 1), 0)
    return pl.pallas_call(
        _expert_kernel,
        grid_spec=pltpu.PrefetchScalarGridSpec(
            num_scalar_prefetch=2,
            grid=(n_blocks,),
            in_specs=[
                pl.BlockSpec((MOE_BLOCK, d + LANE), xmap),
                pl.BlockSpec((1, 1, d, f2), lambda i, be, nu: (layer, be[i], 0, 0)),
                pl.BlockSpec((1, 1, 1, f2), lambda i, be, nu: (layer, be[i], 0, 0)),
                pl.BlockSpec((1, 1, D_FF, d), lambda i, be, nu: (layer, be[i], 0, 0)),
                pl.BlockSpec((1, 1, 1, d), lambda i, be, nu: (layer, be[i], 0, 0)),
            ],
            out_specs=pl.BlockSpec((MOE_BLOCK, d), lambda i, be, nu: (i, 0)),
            scratch_shapes=[pltpu.VMEM((d, f2), BF16), pltpu.VMEM((D_FF, d), BF16)],
        ),
        out_shape=jax.ShapeDtypeStruct((rows, d), F32),
        compiler_params=_cparams(("arbitrary",)),
        name="moe_experts",
    )(block_expert, n_used, xs, w_gu, b_gu.reshape(depth, N_EXPERTS, 1, f2), w_down,
      b_down.reshape(depth, N_EXPERTS, 1, d))


def _combine_kernel(base_ref, tinfo_ref, spos_ref, x_ref, g_ref, b_ref, y_ref, o_ref, ybuf_s, yhi_s, ylo_s, p_s, sem):
    tb = x_ref.shape[0]
    s_rows = ybuf_s.shape[0]

    @pl.when(pl.program_id(0) == 0)
    def _():
        ybuf_s[...] = jnp.zeros(ybuf_s.shape, F32)

    _for_each_group(tinfo_ref, base_ref, lambda s, r: _group_copy(y_ref, r, ybuf_s, s, sem).start())
    _for_each_group(tinfo_ref, base_ref, lambda s, r: _group_copy(y_ref, r, ybuf_s, s, sem).wait())

    spos = spos_ref[...]
    for c0 in range(0, s_rows, SORT_ROWS):
        c_iota = c0 + lax.broadcasted_iota(jnp.int32, (tb, SORT_ROWS), 1)
        hit = spos[:, 0:1] == c_iota
        for k in range(1, TOP_K):
            hit = hit | (spos[:, k:k + 1] == c_iota)
        p_s[:, c0:c0 + SORT_ROWS] = jnp.where(hit, 1.0, 0.0).astype(BF16)
        yc = ybuf_s[c0:c0 + SORT_ROWS, :]
        hi = yc.astype(BF16)
        yhi_s[c0:c0 + SORT_ROWS, :] = hi
        ylo_s[c0:c0 + SORT_ROWS, :] = (yc - hi.astype(F32)).astype(BF16)
    p = p_s[...]
    moe = _dot(p, yhi_s[...]) + _dot(p, ylo_s[...])
    o_ref[...] = _layer_norm(DEEPNORM_ALPHA * x_ref[...] + moe, g_ref[...], b_ref[...])


def _combine(y, x2, spos_col, tinfo, base, ln_g, ln_b):
    n, d = x2.shape
    tb = min(MOE_TILE, n)
    s_rows = _sorted_rows(tb)
    return pl.pallas_call(
        _combine_kernel,
        grid_spec=pltpu.PrefetchScalarGridSpec(
            num_scalar_prefetch=1,
            grid=(n // tb,),
            in_specs=[
                _tile_info_spec(),
                pl.BlockSpec((tb, TOP_K), lambda i, *_: (i, 0)),
                pl.BlockSpec((tb, d), lambda i, *_: (i, 0)),
                pl.BlockSpec((1, d), lambda i, *_: (0, 0)),
                pl.BlockSpec((1, d), lambda i, *_: (0, 0)),
                pl.BlockSpec(memory_space=pl.ANY),
            ],
            out_specs=pl.BlockSpec((tb, d), lambda i, *_: (i, 0)),
            scratch_shapes=[pltpu.VMEM((s_rows, d), F32), pltpu.VMEM((s_rows, d), BF16), pltpu.VMEM((s_rows, d), BF16),
                            pltpu.VMEM((tb, s_rows), BF16), pltpu.SemaphoreType.DMA(())],
        ),
        out_shape=jax.ShapeDtypeStruct((n, d), F32),
        compiler_params=_cparams(("arbitrary",)),
        name="moe_combine",
    )(base, tinfo, spos_col, x2, ln_g.reshape(1, d), ln_b.reshape(1, d), y)


def _sorted_rows(tb):
    return TOP_K * tb + N_EXPERTS * ROW_ALIGN


def _moe_ffn_ln(x2, layer, router_w, router_b, w_gu, b_gu, w_down, b_down, ln_g, ln_b):
    n, _ = x2.shape
    n_tiles = n // min(MOE_TILE, n)
    spos, gate, tinfo, seg = _router(x2, router_w, router_b)
    seg = seg.reshape(N_EXPERTS)
    max_rows = n * TOP_K + n_tiles * N_EXPERTS * (ROW_ALIGN - 1) + N_EXPERTS * (MOE_BLOCK - 1)
    n_blocks = (max_rows + MOE_BLOCK - 1) // MOE_BLOCK
    padded = (seg + MOE_BLOCK - 1) // MOE_BLOCK * MOE_BLOCK
    padded_end = jnp.cumsum(padded)
    base = jnp.concatenate([jnp.zeros((1,), jnp.int32), padded_end]).astype(jnp.int32)
    n_used = (padded_end[-1:] // MOE_BLOCK).astype(jnp.int32)
    block_start = jnp.arange(n_blocks, dtype=jnp.int32) * MOE_BLOCK
    block_expert = jnp.sum(block_start[:, None] >= padded_end[None, :], axis=1).astype(jnp.int32)
    last_expert = jnp.sum(padded_end[-1] - 1 >= padded_end).astype(jnp.int32)
    block_expert = jnp.minimum(block_expert, last_expert)
    xs = _dispatch(x2, spos, gate, tinfo, base, seg, n_blocks * MOE_BLOCK)
    y = _experts(xs, block_expert, n_used, layer, w_gu, b_gu, w_down, b_down)
    return _combine(y, x2, spos.T, tinfo, base, ln_g, ln_b)


def kernel(x, ln_g, ln_b, nsa_w_in, nsa_cmp_pos, nsa_cmp_w1, nsa_cmp_b1, nsa_cmp_w2, nsa_cmp_b2, nsa_gate_b, nsa_w_out, ml_w_in, ml_conv_w, ml_conv_b, ml_gate_b, ml_norm_g, ml_w_out, hg_w_in, hg_lower, hg_norm_g, hg_w_out, router_w, router_b, moe_w_gu, moe_b_gu, moe_w_down, moe_b_down):
    bsz, t, d = x.shape
    x2 = x.reshape(bsz * t, d)
    for layer in range(DEPTH):
        kind, slot = layer % N_MIXERS, layer // N_MIXERS
        if kind == 0:
            x2 = _nsa_mixer(x2, bsz, t, nsa_w_in[slot], nsa_cmp_pos[slot], nsa_cmp_w1[slot], nsa_cmp_b1[slot],
                            nsa_cmp_w2[slot], nsa_cmp_b2[slot], nsa_gate_b[slot], nsa_w_out[slot],
                            ln_g[layer, 0], ln_b[layer, 0])
        elif kind == 1:
            x2 = _mlstm_mixer(x2, bsz, t, ml_w_in[slot], ml_conv_w[slot], ml_conv_b[slot], ml_gate_b[slot],
                              ml_norm_g[slot], ml_w_out[slot], ln_g[layer, 0], ln_b[layer, 0])
        else:
            x2 = _hgrn_mixer(x2, bsz, t, layer, hg_w_in[slot], hg_lower, hg_norm_g[slot], hg_w_out[slot],
                             ln_g[layer, 0], ln_b[layer, 0])
        x2 = _moe_ffn_ln(x2, layer, router_w[layer], router_b[layer], moe_w_gu, moe_b_gu, moe_w_down, moe_b_down,
                         ln_g[layer, 1], ln_b[layer, 1])
    return x2.reshape(bsz, t, d)
```

```python
import functools

import numpy as np
import jax
import jax.numpy as jnp
from jax import lax
from jax.experimental import pallas as pl
from jax.experimental.pallas import tpu as pltpu

F32 = jnp.float32
BF16 = jnp.bfloat16
HIGHEST = lax.Precision.HIGHEST

D_MODEL = 1024
DEPTH = 4
N_MIXERS = 3

NSA_HEADS = 16
NSA_KV_GROUPS = 4
NSA_HEAD_DIM = 64
NSA_HPG = 4
NSA_KV_WIDTH = 256
CMP_BLOCK = 32
CMP_STRIDE = 16
CMP_HIDDEN = 256
SEL_BLOCK = 64
SEL_TOPK = 8
WINDOW = 512

MLSTM_HEADS = 4
MLSTM_QK_DIM = 128
MLSTM_V_DIM = 256
MLSTM_CONV = 4

HGRN_HEADS = 8
HGRN_DIM = 128
HGRN_CHUNK = 32

N_EXPERTS = 32
TOP_K = 4
D_FF = 1024
SWIGLU_LIMIT = 7.0
SWIGLU_ALPHA = 1.702

LN_EPS = 1e-5
RMS_EPS = 1e-6
DEEPNORM_ALPHA = (2 * DEPTH) ** 0.25
NEG = -1e30
M_FLOOR = -1e20

LANE = 128
VMEM_LIMIT = 56 * 1024 * 1024

PROJ_TM = 512
MOE_TILE = 512
MOE_BLOCK = 256
ROW_ALIGN = 8
SORT_ROWS = 256
NSA_TQ = 128
NSA_TK = 512
MLSTM_L = 512
HGRN_SUPER = 128


def _cparams(sem):
    return pltpu.CompilerParams(dimension_semantics=sem, vmem_limit_bytes=VMEM_LIMIT)


def _dot(a, b):
    return jnp.dot(a, b, preferred_element_type=F32)


def _dot_nt(a, b, precision=None):
    return lax.dot_general(a, b, (((1,), (1,)), ((), ())), precision=precision, preferred_element_type=F32)


def _dot_tn(a, b):
    return lax.dot_general(a, b, (((0,), (0,)), ((), ())), preferred_element_type=F32)


def _layer_norm(z, g, b):
    mu = jnp.mean(z, axis=-1, keepdims=True)
    zc = z - mu
    var = jnp.mean(zc * zc, axis=-1, keepdims=True)
    return zc * lax.rsqrt(var + LN_EPS) * g + b


def _proj_kernel(x_ref, *refs, n_w, n_t):
    w_refs = refs[:n_w]
    wt_refs = refs[n_w:n_w + n_t]
    o_refs = refs[n_w + n_t:2 * n_w + n_t]
    ot_refs = refs[2 * n_w + n_t:]
    x = x_ref[...]
    xb = x.astype(BF16)
    for w_ref, o_ref in zip(w_refs, o_refs):
        ncol = w_ref.shape[1]
        for c0 in range(0, ncol, 512):
            c1 = min(c0 + 512, ncol)
            o_ref[:, c0:c1] = _dot(xb, w_ref[:, c0:c1]).astype(o_ref.dtype)
    for wt_ref, o_ref in zip(wt_refs, ot_refs):
        o_ref[...] = _dot_nt(wt_ref[...], x, precision=HIGHEST)


def _proj(x2, weights, out_dtypes, weights_t=()):
    n, d = x2.shape
    tm = min(PROJ_TM, n)
    in_specs = [pl.BlockSpec((tm, d), lambda i: (i, 0))]
    in_specs += [pl.BlockSpec(w.shape, lambda i: (0, 0)) for w in weights]
    in_specs += [pl.BlockSpec(w.shape, lambda i: (0, 0)) for w in weights_t]
    out_shape = [jax.ShapeDtypeStruct((n, w.shape[1]), dt) for w, dt in zip(weights, out_dtypes)]
    out_shape += [jax.ShapeDtypeStruct((w.shape[0], n), F32) for w in weights_t]
    out_specs = [pl.BlockSpec((tm, w.shape[1]), lambda i: (i, 0)) for w in weights]
    out_specs += [pl.BlockSpec((w.shape[0], tm), lambda i: (0, i)) for w in weights_t]
    return pl.pallas_call(
        functools.partial(_proj_kernel, n_w=len(weights), n_t=len(weights_t)),
        grid=(n // tm,),
        in_specs=in_specs,
        out_specs=out_specs,
        out_shape=out_shape,
        compiler_params=_cparams(("parallel",)),
        name="proj",
    )(x2, *weights, *weights_t)


def _outproj_ln_kernel(h_ref, w_ref, x_ref, g_ref, b_ref, o_ref):
    y = _dot(h_ref[...], w_ref[...])
    o_ref[...] = _layer_norm(DEEPNORM_ALPHA * x_ref[...] + y, g_ref[...], b_ref[...])


def _outproj_ln(h2, w, x2, g, b):
    n, d = x2.shape
    tm = min(PROJ_TM, n)
    return pl.pallas_call(
        _outproj_ln_kernel,
        grid=(n // tm,),
        in_specs=[
            pl.BlockSpec((tm, d), lambda i: (i, 0)),
            pl.BlockSpec((d, d), lambda i: (0, 0)),
            pl.BlockSpec((tm, d), lambda i: (i, 0)),
            pl.BlockSpec((1, d), lambda i: (0, 0)),
            pl.BlockSpec((1, d), lambda i: (0, 0)),
        ],
        out_specs=pl.BlockSpec((tm, d), lambda i: (i, 0)),
        out_shape=jax.ShapeDtypeStruct((n, d), F32),
        compiler_params=_cparams(("parallel",)),
        name="outproj_ln",
    )(h2, w, x2, g.reshape(1, d), b.reshape(1, d))


def _gelu_tanh(x):
    return 0.5 * x * (1.0 + jnp.tanh(np.sqrt(2.0 / np.pi).astype(np.float32) * (x + 0.044715 * (x * x * x))))


def _nsa_compress_kernel(kv_ref, wcat_ref, posa_ref, posb_ref, b1_ref, w2_ref, b2_ref, o_ref):
    n_row = kv_ref.shape[1] // CMP_STRIDE
    half = 2 * CMP_HIDDEN
    acc_a = jnp.zeros((n_row, half), F32)
    acc_b = jnp.zeros((n_row, half), F32)
    for r in range(CMP_STRIDE):
        xr = kv_ref[0, pl.ds(r, n_row, stride=CMP_STRIDE), :]
        w = wcat_ref[r]
        acc_a = acc_a + _dot((xr + posa_ref[r]).astype(BF16), w[:, :half])
        acc_b = acc_b + _dot((xr + posb_ref[r]).astype(BF16), w[:, half:])
    hidden = acc_a + pltpu.roll(acc_b, n_row - 1, 0) + b1_ref[...]
    hidden = _gelu_tanh(hidden)
    o_ref[0, 0] = (_dot(hidden.astype(BF16), w2_ref[...]) + b2_ref[...]).astype(o_ref.dtype)


def _nsa_compress(kvc, wcat, posa, posb, b1cat, w2cat, b2cat):
    bsz, t, _ = kvc.shape
    n_row = t // CMP_STRIDE
    return pl.pallas_call(
        _nsa_compress_kernel,
        grid=(bsz, NSA_KV_GROUPS),
        in_specs=[
            pl.BlockSpec((1, t, LANE), lambda b, g: (b, 0, g)),
            pl.BlockSpec(wcat.shape, lambda b, g: (0, 0, 0)),
            pl.BlockSpec(posa.shape, lambda b, g: (0, 0, 0)),
            pl.BlockSpec(posb.shape, lambda b, g: (0, 0, 0)),
            pl.BlockSpec(b1cat.shape, lambda b, g: (0, 0)),
            pl.BlockSpec(w2cat.shape, lambda b, g: (0, 0)),
            pl.BlockSpec(b2cat.shape, lambda b, g: (0, 0)),
        ],
        out_specs=pl.BlockSpec((1, 1, n_row, LANE), lambda b, g: (b, g, 0, 0)),
        out_shape=jax.ShapeDtypeStruct((bsz, NSA_KV_GROUPS, n_row, LANE), BF16),
        compiler_params=_cparams(("parallel", "parallel")),
        name="nsa_compress",
    )(kvc, wcat, posa, posb, b1cat, w2cat, b2cat)


def _nsa_attn_kernel(q_ref, kvc_ref, kvs_ref, kvw_ref, gate_ref, gb_ref, c2s_ref, o_ref, *, tq, tk, n_cmp):
    hpg = NSA_HPG
    t0 = pl.program_id(2) * tq
    q4 = jnp.concatenate([q_ref[0, :, h * LANE:(h + 1) * LANE] for h in range(hpg)], axis=0)
    q4 = q4 * jnp.asarray(NSA_HEAD_DIM ** -0.5, BF16)
    t_col = t0 + lax.broadcasted_iota(jnp.int32, (tq, 1), 0)
    lane = lax.broadcasted_iota(jnp.int32, (1, LANE), 1)
    ones_lane = (lane == 0).astype(BF16)

    def with_ones(kv):
        return jnp.where(lane < NSA_HEAD_DIM, ones_lane, kv)

    span = WINDOW + tq
    w0 = pl.multiple_of(jnp.maximum(t0 - WINDOW, 0), tq)
    kvw = kvw_ref[0, pl.ds(w0, span), :]
    kvw1 = with_ones(kvw)
    kvc = kvc_ref[0, 0]
    n_c = kvc.shape[0]
    q_heads = [q4[h * tq:(h + 1) * tq] for h in range(hpg)]
    s_c = [_dot_nt(q_heads[h], kvc) for h in range(hpg)]
    s_w = [_dot_nt(q_heads[h], kvw).astype(BF16) for h in range(hpg)]

    c_lane = lax.broadcasted_iota(jnp.int32, (1, n_c), 1)
    valid_c = (c_lane * CMP_STRIDE + (CMP_BLOCK - 1) <= t_col) & (c_lane < n_cmp)
    p_heads = []
    for h in range(hpg):
        sm = jnp.where(valid_c, s_c[h], NEG)
        m = jnp.max(sm, axis=-1, keepdims=True)
        p = jnp.where(valid_c, jnp.exp(sm - m), 0.0)
        l = jnp.sum(p, axis=-1, keepdims=True)
        p_heads.append(p / jnp.maximum(l, 1e-30))
    p_sum = p_heads[0]
    for h in range(1, hpg):
        p_sum = p_sum + p_heads[h]
    imp = _dot_nt(c2s_ref[...], p_sum, precision=HIGHEST)
    o_c = [_dot(p_heads[h].astype(BF16), kvc) for h in range(hpg)]

    lag = t_col - (w0 + lax.broadcasted_iota(jnp.int32, (1, span), 1))
    bias_w = jnp.where((lag >= 0) & (lag < WINDOW), 0.0, NEG).astype(BF16)
    o_w = []
    for h in range(hpg):
        sm = s_w[h] + bias_w
        o_w.append(_dot(jnp.exp(sm - jnp.max(sm, axis=-1, keepdims=True)), kvw1))

    n_sel = imp.shape[0]
    blk = lax.broadcasted_iota(jnp.int32, (n_sel, 1), 0)
    cur = (t0 + lax.broadcasted_iota(jnp.int32, (1, tq), 1)) // SEL_BLOCK
    forced = (blk == 0) | (blk == cur) | (blk == cur - 1)
    score = jnp.where(forced, -NEG, jnp.where(blk <= cur, imp, NEG))
    rank = jnp.zeros((n_sel, tq), F32)
    for i in range(n_sel):
        s_i = score[i:i + 1, :]
        ahead = (s_i > score) | ((s_i == score) & (blk > i))
        rank = rank + ahead.astype(F32)
    member = ((rank < SEL_TOPK) & (score > 0.5 * NEG)).astype(BF16)

    blk_per_tile = tk // SEL_BLOCK
    key_lane = lax.broadcasted_iota(jnp.int32, (1, tk), 1)

    def sel_body(kt, carry):
        k0 = pl.multiple_of(kt * tk, tk)
        kv = kvs_ref[0, pl.ds(k0, tk), :]
        kv1 = with_ones(kv)
        expand = (key_lane // SEL_BLOCK + kt * blk_per_tile == blk).astype(BF16)
        valid = (_dot_tn(member, expand) > 0.5) & (k0 + key_lane <= t_col)
        bias = jnp.where(valid, 0.0, NEG).astype(BF16)
        s = [_dot_nt(q_heads[h], kv).astype(BF16) for h in range(hpg)]
        new = []
        for h in range(hpg):
            m_old, acc_old = carry[h]
            sm = s[h] + bias
            m_new = jnp.maximum(m_old, jnp.max(sm, axis=-1, keepdims=True).astype(F32))
            pv = _dot(jnp.exp(sm - m_new.astype(BF16)), kv1)
            new.append((m_new, jnp.exp(m_old - m_new) * acc_old + pv))
        return tuple(new)

    init = tuple((jnp.full((tq, 1), M_FLOOR, F32), jnp.zeros((tq, LANE), F32)) for _ in range(hpg))
    n_kt = (t0 + tq - 1) // tk + 1
    sel = lax.fori_loop(0, n_kt, sel_body, init)

    gates = jax.nn.sigmoid(gate_ref[0].astype(F32) + gb_ref[0])
    outs = []
    for h in range(hpg):
        o_s = sel[h][1]
        outs.append(gates[:, 3 * h:3 * h + 1] * o_c[h]
                    + gates[:, 3 * h + 1:3 * h + 2] * (o_s / o_s[:, 0:1])
                    + gates[:, 3 * h + 2:3 * h + 3] * (o_w[h] / o_w[h][:, 0:1]))
    for j in range(hpg // 2):
        pair = jnp.where(lane < NSA_HEAD_DIM, pltpu.roll(outs[2 * j], NSA_HEAD_DIM, 1), outs[2 * j + 1])
        o_ref[0, :, j * LANE:(j + 1) * LANE] = pair.astype(o_ref.dtype)


def _nsa_attention(main, kvcmp, gate_b, c2s, bsz, t):
    tq = min(NSA_TQ, t)
    tk = min(NSA_TK, t)
    n_cmp = (t - CMP_BLOCK) // CMP_STRIDE + 1
    q_blocks = NSA_HEADS
    g = NSA_KV_GROUPS
    kernel = functools.partial(_nsa_attn_kernel, tq=tq, tk=tk, n_cmp=n_cmp)
    return pl.pallas_call(
        kernel,
        grid=(bsz, g, t // tq),
        in_specs=[
            pl.BlockSpec((1, tq, NSA_HPG * LANE), lambda b, gi, qi: (b, qi, gi)),
            pl.BlockSpec((1, 1) + kvcmp.shape[2:], lambda b, gi, qi: (b, gi, 0, 0)),
            pl.BlockSpec((1, t, LANE), lambda b, gi, qi: (b, 0, q_blocks + gi)),
            pl.BlockSpec((1, t, LANE), lambda b, gi, qi: (b, 0, q_blocks + g + gi)),
            pl.BlockSpec((1, tq, LANE), lambda b, gi, qi: (b, qi, q_blocks + 2 * g + gi)),
            pl.BlockSpec((1, 1, LANE), lambda b, gi, qi: (gi, 0, 0)),
            pl.BlockSpec(c2s.shape, lambda b, gi, qi: (0, 0)),
        ],
        out_specs=pl.BlockSpec((1, tq, NSA_HPG * NSA_HEAD_DIM), lambda b, gi, qi: (b, qi, gi)),
        out_shape=jax.ShapeDtypeStruct((bsz, t, D_MODEL), BF16),
        compiler_params=_cparams(("parallel", "parallel", "arbitrary")),
        name="nsa_attention",
    )(main, kvcmp, main, main, main, gate_b, c2s)


def _nsa_mixer(x2, bsz, t, w_in, cmp_pos, cmp_w1, cmp_b1, cmp_w2, cmp_b2, gate_b, w_out, ln_g, ln_b):
    d, g, dh, hpg = D_MODEL, NSA_KV_GROUPS, NSA_HEAD_DIM, NSA_HPG
    kvw_ = NSA_KV_WIDTH

    def pair(k0):
        k = w_in[:, k0:k0 + kvw_].reshape(d, g, 1, dh)
        v = w_in[:, k0 + kvw_:k0 + 2 * kvw_].reshape(d, g, 1, dh)
        return jnp.concatenate([k, v], axis=2).reshape(d, g * LANE)

    wq = jnp.pad(w_in[:, :d].reshape(d, NSA_HEADS, dh), ((0, 0), (0, 0), (0, LANE - dh))).reshape(d, NSA_HEADS * LANE)
    n_gate = 3 * hpg
    wg = jnp.pad(w_in[:, d + 6 * kvw_:].reshape(d, g, n_gate), ((0, 0), (0, 0), (0, LANE - n_gate))).reshape(d, g * LANE)
    w_main = jnp.concatenate([wq, pair(d + 2 * kvw_), pair(d + 4 * kvw_), wg], axis=1).astype(BF16)
    w_kvc = pair(d).astype(BF16)
    main, kvc = _proj(x2, [w_main, w_kvc], [BF16, F32])

    half = CMP_BLOCK // 2
    w1 = cmp_w1.reshape(2, CMP_BLOCK, dh, CMP_HIDDEN)
    z = jnp.zeros((half, dh, CMP_HIDDEN), F32)
    top = jnp.concatenate([w1[0, :half], z, w1[0, half:], z], axis=2)
    bot = jnp.concatenate([z, w1[1, :half], z, w1[1, half:]], axis=2)
    wcat = jnp.concatenate([top, bot], axis=1).astype(BF16)
    pos = jnp.concatenate([cmp_pos[0], cmp_pos[1]], axis=-1)
    posa, posb = pos[:half, None, :], pos[half:, None, :]
    b1cat = cmp_b1.reshape(1, 2 * CMP_HIDDEN)
    zz = jnp.zeros((CMP_HIDDEN, dh), F32)
    w2cat = jnp.concatenate([jnp.concatenate([cmp_w2[0], zz], axis=1),
                             jnp.concatenate([zz, cmp_w2[1]], axis=1)], axis=0).astype(BF16)
    b2cat = cmp_b2.reshape(1, 2 * dh)
    kvcmp = _nsa_compress(kvc.reshape(bsz, t, g * LANE), wcat, posa, posb, b1cat, w2cat, b2cat)

    n_row = t // CMP_STRIDE
    n_sel = t // SEL_BLOCK
    cmp_start = np.arange(n_row) * CMP_STRIDE
    sel_start = np.arange(n_sel) * SEL_BLOCK
    overlap = (np.minimum(cmp_start[:, None] + CMP_BLOCK, sel_start[None, :] + SEL_BLOCK)
               - np.maximum(cmp_start[:, None], sel_start[None, :]))
    c2s = jnp.asarray((np.clip(overlap, 0, None) / CMP_STRIDE).astype(np.float32).T)
    gb = jnp.pad(gate_b.reshape(g, 1, n_gate), ((0, 0), (0, 0), (0, LANE - n_gate)))
    o = _nsa_attention(main.reshape(bsz, t, -1), kvcmp, gb, c2s, bsz, t)
    return _outproj_ln(o.reshape(bsz * t, d), w_out.astype(BF16), x2, ln_g, ln_b)


def _mlstm_kernel(q_ref, k_ref, v_ref, og_ref, gif_ref, gb_ref, cwq_ref, cwk_ref, cbq_ref, cbk_ref, ng_ref,
                  o_ref, pad_s, q_s, k_s, ct_s, *, chunk):
    t = q_ref.shape[1]
    dk, dv = MLSTM_QK_DIM, MLSTM_V_DIM
    head = pl.program_id(1)
    halo = 8

    def conv_silu(x_ref, w_ref, b_ref, dst, scale):
        pad_s[0:halo, :] = jnp.zeros((halo, dk), F32)
        pad_s[halo:halo + t, :] = x_ref[0]
        rows = min(t, 256)
        for r0 in range(0, t, rows):
            y = b_ref[...] + w_ref[0:1, :] * pad_s[pl.ds(halo + r0 - (MLSTM_CONV - 1), rows), :]
            for j in range(1, MLSTM_CONV):
                y = y + w_ref[j:j + 1, :] * pad_s[pl.ds(halo + r0 - (MLSTM_CONV - 1) + j, rows), :]
            y = y * jax.nn.sigmoid(y)
            dst[r0:r0 + rows, :] = (y * scale).astype(dst.dtype)

    conv_silu(q_ref, cwq_ref, cbq_ref, q_s, 1.0)
    conv_silu(k_ref, cwk_ref, cbk_ref, k_s, dk ** -0.5)
    ct_s[...] = jnp.zeros((dk, dv), F32)

    row = lax.broadcasted_iota(jnp.int32, (chunk, chunk), 0)
    col = lax.broadcasted_iota(jnp.int32, (chunk, chunk), 1)
    causal = col <= row
    eye = col == row
    upper = (row <= col).astype(F32)
    b_i = gb_ref[pl.ds(head, 1), :]
    b_f = gb_ref[pl.ds(MLSTM_HEADS + head, 1), :]
    norm_g = ng_ref[...]

    def body(c, carry):
        n_row, m_prev = carry
        r0 = pl.multiple_of(c * chunk, chunk)
        qc = q_s[pl.ds(r0, chunk), :]
        kc = k_s[pl.ds(r0, chunk), :]
        vc = v_ref[0, pl.ds(r0, chunk), :]
        li_row = gif_ref[0, c, pl.ds(head, 1), :] + b_i
        zf = gif_ref[0, c, pl.ds(MLSTM_HEADS + head, 1), :] + b_f
        lf_row = jnp.minimum(zf, 0.0) - jnp.log1p(jnp.exp(-jnp.abs(zf)))
        b_row = jnp.dot(lf_row, upper, precision=HIGHEST, preferred_element_type=F32)
        b_col = jnp.sum(jnp.where(causal, lf_row, 0.0), axis=-1, keepdims=True)
        li_col = jnp.sum(jnp.where(eye, li_row, 0.0), axis=-1, keepdims=True)
        d_mat = jnp.where(causal, b_col - b_row + li_row, NEG)
        m_t = jnp.maximum(b_col + m_prev, jnp.max(d_mat, axis=-1, keepdims=True))
        w_inter = jnp.exp(b_col + m_prev - m_t)
        s = _dot_nt(qc, kc) * jnp.exp(d_mat - m_t)
        num = _dot(s.astype(BF16), vc) + w_inter * _dot(qc, ct_s[...].astype(BF16))
        den = jnp.sum(s, axis=-1, keepdims=True) + w_inter * jnp.sum(qc.astype(F32) * n_row, axis=-1, keepdims=True)
        h = num / jnp.maximum(jnp.abs(den), jnp.exp(-m_t))
        b_end = b_col[chunk - 1:chunk, :]
        decay = b_end - b_col + li_col
        m_new = jnp.maximum(b_end + m_prev, jnp.max(decay, axis=0, keepdims=True))
        w_k = jnp.exp(decay - m_new)
        scale = jnp.exp(b_end + m_prev - m_new)
        ct_s[...] = scale * ct_s[...] + _dot_tn(kc, (vc.astype(F32) * w_k).astype(BF16))
        n_new = scale * n_row + jnp.sum(w_k * kc.astype(F32), axis=0, keepdims=True)
        hn = h * lax.rsqrt(jnp.mean(h * h, axis=-1, keepdims=True) + RMS_EPS) * norm_g
        og = og_ref[0, pl.ds(r0, chunk), :].astype(F32)
        o_ref[0, pl.ds(r0, chunk), :] = (hn * jax.nn.sigmoid(og)).astype(o_ref.dtype)
        return n_new, m_new

    lax.fori_loop(0, t // chunk, body, (jnp.zeros((1, dk), F32), jnp.zeros((1, 1), F32)))


def _mlstm_core(qk, v, og, gif, gate_b, conv_w, conv_b, norm_g, bsz, t, chunk):
    hh, dk, dv = MLSTM_HEADS, MLSTM_QK_DIM, MLSTM_V_DIM
    nc = t // chunk
    kernel = functools.partial(_mlstm_kernel, chunk=chunk)
    return pl.pallas_call(
        kernel,
        grid=(bsz, hh),
        in_specs=[
            pl.BlockSpec((1, t, dk), lambda b, h: (b, 0, h)),
            pl.BlockSpec((1, t, dk), lambda b, h: (b, 0, hh + h)),
            pl.BlockSpec((1, t, dv), lambda b, h: (b, 0, h)),
            pl.BlockSpec((1, t, dv), lambda b, h: (b, 0, h)),
            pl.BlockSpec((1, nc, 2 * hh, chunk), lambda b, h: (b, 0, 0, 0)),
            pl.BlockSpec((2 * hh, 1), lambda b, h: (0, 0)),
            pl.BlockSpec((MLSTM_CONV, dk), lambda b, h: (0, h)),
            pl.BlockSpec((MLSTM_CONV, dk), lambda b, h: (0, hh + h)),
            pl.BlockSpec((1, dk), lambda b, h: (0, h)),
            pl.BlockSpec((1, dk), lambda b, h: (0, hh + h)),
            pl.BlockSpec((1, dv), lambda b, h: (0, h)),
        ],
        out_specs=pl.BlockSpec((1, t, dv), lambda b, h: (b, 0, h)),
        out_shape=jax.ShapeDtypeStruct((bsz, t, hh * dv), BF16),
        scratch_shapes=[
            pltpu.VMEM((t + 8, dk), F32),
            pltpu.VMEM((t, dk), BF16),
            pltpu.VMEM((t, dk), BF16),
            pltpu.VMEM((dk, dv), F32),
        ],
        compiler_params=_cparams(("parallel", "parallel")),
        name="mlstm",
    )(qk, qk, v, og, gif, gate_b.reshape(2 * hh, 1), conv_w, conv_w, conv_b.reshape(1, -1), conv_b.reshape(1, -1),
      norm_g.reshape(1, -1))


def _mlstm_mixer(x2, bsz, t, w_in, conv_w, conv_b, gate_b, norm_g, w_out, ln_g, ln_b):
    d, hh = D_MODEL, MLSTM_HEADS
    qkw = 2 * hh * MLSTM_QK_DIM
    vw = hh * MLSTM_V_DIM
    chunk = min(MLSTM_L, t)
    w_qk = w_in[:, :qkw].astype(BF16)
    w_v = w_in[:, qkw:qkw + vw].astype(BF16)
    w_og = w_in[:, qkw + vw:qkw + vw + d].astype(BF16)
    w_gif_t = w_in[:, qkw + vw + d:].T
    qk, v, og, gif_t = _proj(x2, [w_qk, w_v, w_og], [F32, BF16, BF16], [w_gif_t])
    gif = gif_t.reshape(2 * hh, bsz, t // chunk, chunk).transpose(1, 2, 0, 3)
    h = _mlstm_core(qk.reshape(bsz, t, qkw), v.reshape(bsz, t, vw), og.reshape(bsz, t, d), gif, gate_b,
                    conv_w, conv_b, norm_g, bsz, t, chunk)
    return _outproj_ln(h.reshape(bsz * t, d), w_out.astype(BF16), x2, ln_g, ln_b)


def _hgrn_kernel(q_ref, f_ref, i_ref, g_ref, low_ref, ng_ref, o_ref, st_s, qd_s, kd_s, ke_s, dec_s, *,
                 layer_idx, heads):
    t = q_ref.shape[1]
    dh, chunk = HGRN_DIM, HGRN_CHUNK
    low = low_ref[...]
    e = jnp.exp(low - jnp.max(low, axis=0, keepdims=True))
    soft = e / jnp.sum(e, axis=0, keepdims=True)
    lb = jnp.zeros((1, heads * dh), F32)
    for r in range(1, layer_idx + 1):
        lb = lb + soft[r:r + 1, :]
    norm_g = ng_ref[...]
    st_s[...] = jnp.zeros(st_s.shape, F32)
    width = heads * dh
    sup = min(HGRN_SUPER, t)
    n_sub = sup // chunk

    in_chunk = lax.broadcasted_iota(jnp.int32, (sup, 1), 0) % chunk

    def prep(s, carry):
        r0 = pl.multiple_of(s * sup, sup)
        f = lb + (1.0 - lb) * jax.nn.sigmoid(f_ref[0, pl.ds(r0, sup), :])
        gcum = jnp.log(f)
        shift = 1
        while shift < chunk:
            gcum = gcum + jnp.where(in_chunk >= shift, pltpu.roll(gcum, shift, 0), 0.0)
            shift *= 2
        g_end = jnp.concatenate(
            [jnp.broadcast_to(gcum[(j + 1) * chunk - 1:(j + 1) * chunk, :], (chunk, width)) for j in range(n_sub)], axis=0)
        qv = q_ref[0, pl.ds(r0, sup), :].astype(F32)
        kk = 1.0 - f
        qd_s[pl.ds(r0, sup), :] = (qv * jax.nn.sigmoid(qv) * jnp.exp(gcum)).astype(BF16)
        kd_s[pl.ds(r0, sup), :] = (kk * jnp.exp(-gcum)).astype(BF16)
        ke_s[pl.ds(r0, sup), :] = (kk * jnp.exp(g_end - gcum)).astype(BF16)
        for j in range(n_sub):
            dec_s[s, j:j + 1, :] = jnp.exp(gcum[(j + 1) * chunk - 1:(j + 1) * chunk, :])
        return carry

    lax.fori_loop(0, t // sup, prep, 0)

    row = lax.broadcasted_iota(jnp.int32, (sup, sup), 0)
    col = lax.broadcasted_iota(jnp.int32, (sup, sup), 1)
    keep = (col <= row) & (col // chunk == row // chunk)

    def body(s, carry):
        r0 = pl.multiple_of(s * sup, sup)
        decay = dec_s[s]
        col_of = [slice(h * dh, (h + 1) * dh) for h in range(heads)]
        sub_of = [slice(j * chunk, (j + 1) * chunk) for j in range(n_sub)]
        q_dec = [qd_s[pl.ds(r0, sup), c] for c in col_of]
        vv = [i_ref[0, pl.ds(r0, sup), c] for c in col_of]
        score = [_dot_nt(q_dec[h], kd_s[pl.ds(r0, sup), col_of[h]]) for h in range(heads)]
        k_end = [ke_s[pl.ds(r0, sup), c] for c in col_of]
        kv = [[_dot_tn(vv[h][r], k_end[h][r]) for r in sub_of] for h in range(heads)]
        intra = [_dot(jnp.where(keep, score[h], 0.0).astype(BF16), vv[h]) for h in range(heads)]
        states = []
        for h in range(heads):
            st = st_s[h]
            per_sub = []
            for j in range(n_sub):
                per_sub.append(st.astype(BF16))
                st = st * decay[j:j + 1, col_of[h]] + kv[h][j]
            st_s[h] = st
            states.append(per_sub)
        for h in range(heads):
            inter = [_dot_nt(q_dec[h][sub_of[j]], states[h][j]) for j in range(n_sub)]
            o = intra[h] + jnp.concatenate(inter, axis=0)
            on = o * lax.rsqrt(jnp.mean(o * o, axis=-1, keepdims=True) + RMS_EPS) * norm_g[:, col_of[h]]
            gate = jax.nn.sigmoid(g_ref[0, pl.ds(r0, sup), col_of[h]].astype(F32))
            o_ref[0, pl.ds(r0, sup), col_of[h]] = (on * gate).astype(o_ref.dtype)
        return carry

    lax.fori_loop(0, t // sup, body, 0)


def _hgrn_core(q, f, i, g, lower, norm_g, bsz, t, layer_idx):
    heads = 4
    width = heads * HGRN_DIM
    n_grp = HGRN_HEADS // heads
    kernel = functools.partial(_hgrn_kernel, layer_idx=layer_idx, heads=heads)
    act = pl.BlockSpec((1, t, width), lambda b, j: (b, 0, j))
    return pl.pallas_call(
        kernel,
        grid=(bsz, n_grp),
        in_specs=[act, act, act, act,
                  pl.BlockSpec((DEPTH, width), lambda b, j: (0, j)),
                  pl.BlockSpec((1, width), lambda b, j: (0, j))],
        out_specs=act,
        out_shape=jax.ShapeDtypeStruct((bsz, t, D_MODEL), BF16),
        scratch_shapes=[pltpu.VMEM((heads, HGRN_DIM, HGRN_DIM), F32), pltpu.VMEM((t, width), BF16),
                        pltpu.VMEM((t, width), BF16), pltpu.VMEM((t, width), BF16),
                        pltpu.VMEM((max(t // HGRN_SUPER, 1), min(HGRN_SUPER, t) // HGRN_CHUNK, width), F32)],
        compiler_params=_cparams(("parallel", "parallel")),
        name="hgrn2",
    )(q, f, i, g, lower, norm_g.reshape(1, -1))


def _hgrn_mixer(x2, bsz, t, layer_idx, w_in, lower, norm_g, w_out, ln_g, ln_b):
    d = D_MODEL
    ws = [w_in[:, j * d:(j + 1) * d].astype(BF16) for j in range(4)]
    q, f, i, g = _proj(x2, ws, [BF16, F32, BF16, BF16])
    shp = (bsz, t, d)
    o = _hgrn_core(q.reshape(shp), f.reshape(shp), i.reshape(shp), g.reshape(shp), lower, norm_g, bsz, t, layer_idx)
    return _outproj_ln(o.reshape(bsz * t, d), w_out.astype(BF16), x2, ln_g, ln_b)


def _router_kernel(x_ref, rwt_ref, rb_ref, spos_ref, gate_ref, tinfo_ref, seg_ref, carry_s):
    @pl.when(pl.program_id(0) == 0)
    def _():
        carry_s[...] = jnp.zeros(carry_s.shape, F32)

    tm = x_ref.shape[0]
    ne = N_EXPERTS
    logits = _dot_nt(rwt_ref[...], x_ref[...], precision=HIGHEST) + rb_ref[...]
    e_iota = lax.broadcasted_iota(jnp.int32, (ne, tm), 0)
    work = logits
    vals, picks = [], []
    for k in range(TOP_K):
        mx = jnp.max(work, axis=0, keepdims=True)
        idx = jnp.min(jnp.where(work == mx, e_iota, ne), axis=0, keepdims=True)
        pick = e_iota == idx
        vals.append(mx)
        picks.append(pick)
        work = jnp.where(pick, -jnp.inf, work)
    exps = [jnp.exp(v - vals[0]) for v in vals]
    tot = exps[0]
    for k in range(1, TOP_K):
        tot = tot + exps[k]
    for k in range(TOP_K):
        gate_ref[k:k + 1, :] = exps[k] / tot
    hot = picks[0].astype(F32)
    for k in range(1, TOP_K):
        hot = hot + picks[k].astype(F32)
    before = (lax.broadcasted_iota(jnp.int32, (tm, tm), 0) < lax.broadcasted_iota(jnp.int32, (tm, tm), 1)).astype(BF16)
    prior = _dot(hot.astype(BF16), before)
    cnt = jnp.sum(hot, axis=1, keepdims=True)
    run = jnp.floor((cnt + (ROW_ALIGN - 1)) * (1.0 / ROW_ALIGN)) * ROW_ALIGN
    sub = lax.broadcasted_iota(jnp.int32, (ne, ne), 0)
    lan = lax.broadcasted_iota(jnp.int32, (ne, ne), 1)
    run_row = jnp.sum(jnp.where(sub == lan, run, 0.0), axis=0, keepdims=True)
    soff = jnp.sum(jnp.where(lan < sub, run_row, 0.0), axis=1, keepdims=True)
    for k in range(TOP_K):
        pos = jnp.sum(jnp.where(picks[k], prior + soff, 0.0), axis=0, keepdims=True)
        spos_ref[k:k + 1, :] = pos.astype(jnp.int32)
    sub = lax.broadcasted_iota(jnp.int32, (ne, LANE), 0)
    lan = lax.broadcasted_iota(jnp.int32, (ne, LANE), 1)
    field, le = lan // ne, lan % ne
    carry = carry_s[...]
    info = (jnp.where((field == 0) & (sub == le), carry, 0.0) + jnp.where((field == 1) & (sub == le), run, 0.0)
            + jnp.where((field == 2) & (sub < le), run, 0.0))
    tinfo_ref[0] = jnp.sum(info, axis=0, keepdims=True).astype(jnp.int32)
    total = carry + run
    carry_s[...] = total
    seg_ref[...] = total.astype(jnp.int32)


def _router(x2, router_w, router_b):
    n, d = x2.shape
    tm = min(MOE_TILE, n)
    n_tiles = n // tm
    row = pl.BlockSpec((TOP_K, tm), lambda i: (0, i))
    return pl.pallas_call(
        _router_kernel,
        grid=(n_tiles,),
        in_specs=[
            pl.BlockSpec((tm, d), lambda i: (i, 0)),
            pl.BlockSpec((N_EXPERTS, d), lambda i: (0, 0)),
            pl.BlockSpec((N_EXPERTS, 1), lambda i: (0, 0)),
        ],
        out_specs=[row, row, pl.BlockSpec((1, 1, LANE), lambda i: (i, 0, 0)),
                   pl.BlockSpec((N_EXPERTS, 1), lambda i: (0, 0))],
        out_shape=[
            jax.ShapeDtypeStruct((TOP_K, n), jnp.int32),
            jax.ShapeDtypeStruct((TOP_K, n), F32),
            jax.ShapeDtypeStruct((n_tiles, 1, LANE), jnp.int32),
            jax.ShapeDtypeStruct((N_EXPERTS, 1), jnp.int32),
        ],
        scratch_shapes=[pltpu.VMEM((N_EXPERTS, 1), F32)],
        compiler_params=_cparams(("arbitrary",)),
        name="router",
    )(x2, router_w.T, router_b.reshape(N_EXPERTS, 1))


def _group_copy(src, s, dst, d, sem):
    return pltpu.make_async_copy(src.at[pl.ds(pl.multiple_of(s, ROW_ALIGN), ROW_ALIGN), :],
                                 dst.at[pl.ds(pl.multiple_of(d, ROW_ALIGN), ROW_ALIGN), :], sem)


def _for_each_group(tinfo_ref, base_ref, fn):
    def per_expert(e, _):
        seg_row = base_ref[e] + tinfo_ref[0, 0, e]
        groups = tinfo_ref[0, 0, N_EXPERTS + e] // ROW_ALIGN
        sorted_row = tinfo_ref[0, 0, 2 * N_EXPERTS + e]

        def per_group(j, _):
            fn(sorted_row + j * ROW_ALIGN, seg_row + j * ROW_ALIGN)
            return 0
        return lax.fori_loop(0, groups, per_group, 0)
    lax.fori_loop(0, N_EXPERTS, per_expert, 0)


def _dispatch_kernel(base_ref, seg_ref, tinfo_ref, spos_ref, gate_ref, x_ref, xs_ref, sort_s, zero_s, sem):
    tb, d = x_ref.shape
    s_rows = sort_s.shape[0]

    @pl.when(pl.program_id(0) == 0)
    def _():
        zero_s[...] = jnp.zeros(zero_s.shape, F32)
        for fill in (True, False):
            def per_expert(e, _):
                def per_group(r, _):
                    cp = _group_copy(zero_s, 0, xs_ref, r * ROW_ALIGN, sem)
                    cp.start() if fill else cp.wait()
                    return 0
                first = jnp.where(e < N_EXPERTS, base_ref[jnp.minimum(e, N_EXPERTS - 1)] + seg_ref[jnp.minimum(e, N_EXPERTS - 1)],
                                  base_ref[N_EXPERTS])
                last = jnp.where(e < N_EXPERTS, base_ref[jnp.minimum(e + 1, N_EXPERTS)], xs_ref.shape[0])
                return lax.fori_loop(first // ROW_ALIGN, last // ROW_ALIGN, per_group, 0)
            lax.fori_loop(0, N_EXPERTS + 1, per_expert, 0)

    xb = x_ref[...].astype(BF16)
    spos = spos_ref[...]
    gate = gate_ref[...]
    for r0 in range(0, s_rows, SORT_ROWS):
        r_iota = r0 + lax.broadcasted_iota(jnp.int32, (SORT_ROWS, tb), 0)
        hits = [spos[k:k + 1, :] == r_iota for k in range(TOP_K)]
        onehot = jnp.where(hits[0] | hits[1] | hits[2] | hits[3], 1.0, 0.0).astype(BF16)
        sort_s[r0:r0 + SORT_ROWS, :d] = _dot(onehot, xb)
        g_sel = jnp.where(hits[0], gate[0:1, :], 0.0)
        for k in range(1, TOP_K):
            g_sel = g_sel + jnp.where(hits[k], gate[k:k + 1, :], 0.0)
        sort_s[r0:r0 + SORT_ROWS, d:] = jnp.broadcast_to(jnp.sum(g_sel, axis=1, keepdims=True), (SORT_ROWS, LANE))

    _for_each_group(tinfo_ref, base_ref, lambda s, r: _group_copy(sort_s, s, xs_ref, r, sem).start())
    _for_each_group(tinfo_ref, base_ref, lambda s, r: _group_copy(sort_s, s, xs_ref, r, sem).wait())


def _tile_info_spec():
    return pl.BlockSpec((1, 1, LANE), lambda i, *_: (i, 0, 0), memory_space=pltpu.SMEM)


def _dispatch(x2, spos, gate, tinfo, base, seg, rows):
    n, d = x2.shape
    tb = min(MOE_TILE, n)
    row = pl.BlockSpec((TOP_K, tb), lambda i, *_: (0, i))
    return pl.pallas_call(
        _dispatch_kernel,
        grid_spec=pltpu.PrefetchScalarGridSpec(
            num_scalar_prefetch=2,
            grid=(n // tb,),
            in_specs=[_tile_info_spec(), row, row, pl.BlockSpec((tb, d), lambda i, *_: (i, 0))],
            out_specs=pl.BlockSpec(memory_space=pl.ANY),
            scratch_shapes=[pltpu.VMEM((_sorted_rows(tb), d + LANE), F32), pltpu.VMEM((ROW_ALIGN, d + LANE), F32),
                            pltpu.SemaphoreType.DMA(())],
        ),
        out_shape=jax.ShapeDtypeStruct((rows, d + LANE), F32),
        compiler_params=_cparams(("arbitrary",)),
        name="moe_dispatch",
    )(base, seg, tinfo, spos, gate, x2)


def _expert_kernel(be_ref, nused_ref, x_ref, wgu_ref, bgu_ref, wd_ref, bd_ref, y_ref, wgu_s, wd_s):
    i = pl.program_id(0)
    used = i < nused_ref[0]

    @pl.when(used & ((i == 0) | (be_ref[i] != be_ref[jnp.maximum(i - 1, 0)])))
    def _():
        wgu_s[...] = wgu_ref[0, 0].astype(BF16)
        wd_s[...] = wd_ref[0, 0].astype(BF16)

    @pl.when(used)
    def _():
        d = y_ref.shape[1]
        xb = x_ref[:, :d].astype(BF16)
        h = _dot(xb, wgu_s[...]) + bgu_ref[0, 0]
        h_gate = jnp.minimum(h[:, :D_FF], SWIGLU_LIMIT)
        h_up = jnp.clip(h[:, D_FF:], -SWIGLU_LIMIT, SWIGLU_LIMIT)
        act = (h_up + 1.0) * h_gate * jax.nn.sigmoid(SWIGLU_ALPHA * h_gate)
        y_ref[...] = (_dot(act.astype(BF16), wd_s[...]) + bd_ref[0, 0]) * x_ref[:, d:d + 1]

    @pl.when(jnp.logical_not(used))
    def _():
        y_ref[...] = jnp.zeros(y_ref.shape, F32)


def _experts(xs, block_expert, n_used, layer, w_gu, b_gu, w_down, b_down):
    rows = xs.shape[0]
    d = D_MODEL
    n_blocks = rows // MOE_BLOCK
    f2 = w_gu.shape[3]
    depth = w_gu.shape[0]
    xmap = lambda i, be, nu: (jnp.minimum(i, nu[0] - 1), 0)
    return pl.pallas_call(
        _expert_kernel,
        grid_spec=pltpu.PrefetchScalarGridSpec(
            num_scalar_prefetch=2,
            grid=(n_blocks,),
            in_specs=[
                pl.BlockSpec((MOE_BLOCK, d + LANE), xmap),
                pl.BlockSpec((1, 1, d, f2), lambda i, be, nu: (layer, be[i], 0, 0)),
                pl.BlockSpec((1, 1, 1, f2), lambda i, be, nu: (layer, be[i], 0, 0)),
                pl.BlockSpec((1, 1, D_FF, d), lambda i, be, nu: (layer, be[i], 0, 0)),
                pl.BlockSpec((1, 1, 1, d), lambda i, be, nu: (layer, be[i], 0, 0)),
            ],
            out_specs=pl.BlockSpec((MOE_BLOCK, d), lambda i, be, nu: (i, 0)),
            scratch_shapes=[pltpu.VMEM((d, f2), BF16), pltpu.VMEM((D_FF, d), BF16)],
        ),
        out_shape=jax.ShapeDtypeStruct((rows, d), F32),
        compiler_params=_cparams(("arbitrary",)),
        name="moe_experts",
    )(block_expert, n_used, xs, w_gu, b_gu.reshape(depth, N_EXPERTS, 1, f2), w_down,
      b_down.reshape(depth, N_EXPERTS, 1, d))


def _combine_kernel(base_ref, tinfo_ref, spos_ref, x_ref, g_ref, b_ref, y_ref, o_ref, ybuf_s, yhi_s, ylo_s, p_s, sem):
    tb = x_ref.shape[0]
    s_rows = ybuf_s.shape[0]

    @pl.when(pl.program_id(0) == 0)
    def _():
        ybuf_s[...] = jnp.zeros(ybuf_s.shape, F32)

    _for_each_group(tinfo_ref, base_ref, lambda s, r: _group_copy(y_ref, r, ybuf_s, s, sem).start())
    _for_each_group(tinfo_ref, base_ref, lambda s, r: _group_copy(y_ref, r, ybuf_s, s, sem).wait())

    spos = spos_ref[...]
    for c0 in range(0, s_rows, SORT_ROWS):
        c_iota = c0 + lax.broadcasted_iota(jnp.int32, (tb, SORT_ROWS), 1)
        hit = spos[:, 0:1] == c_iota
        for k in range(1, TOP_K):
            hit = hit | (spos[:, k:k + 1] == c_iota)
        p_s[:, c0:c0 + SORT_ROWS] = jnp.where(hit, 1.0, 0.0).astype(BF16)
        yc = ybuf_s[c0:c0 + SORT_ROWS, :]
        hi = yc.astype(BF16)
        yhi_s[c0:c0 + SORT_ROWS, :] = hi
        ylo_s[c0:c0 + SORT_ROWS, :] = (yc - hi.astype(F32)).astype(BF16)
    p = p_s[...]
    moe = _dot(p, yhi_s[...]) + _dot(p, ylo_s[...])
    o_ref[...] = _layer_norm(DEEPNORM_ALPHA * x_ref[...] + moe, g_ref[...], b_ref[...])


def _combine(y, x2, spos_col, tinfo, base, ln_g, ln_b):
    n, d = x2.shape
    tb = min(MOE_TILE, n)
    s_rows = _sorted_rows(tb)
    return pl.pallas_call(
        _combine_kernel,
        grid_spec=pltpu.PrefetchScalarGridSpec(
            num_scalar_prefetch=1,
            grid=(n // tb,),
            in_specs=[
                _tile_info_spec(),
                pl.BlockSpec((tb, TOP_K), lambda i, *_: (i, 0)),
                pl.BlockSpec((tb, d), lambda i, *_: (i, 0)),
                pl.BlockSpec((1, d), lambda i, *_: (0, 0)),
                pl.BlockSpec((1, d), lambda i, *_: (0, 0)),
                pl.BlockSpec(memory_space=pl.ANY),
            ],
            out_specs=pl.BlockSpec((tb, d), lambda i, *_: (i, 0)),
            scratch_shapes=[pltpu.VMEM((s_rows, d), F32), pltpu.VMEM((s_rows, d), BF16), pltpu.VMEM((s_rows, d), BF16),
                            pltpu.VMEM((tb, s_rows), BF16), pltpu.SemaphoreType.DMA(())],
        ),
        out_shape=jax.ShapeDtypeStruct((n, d), F32),
        compiler_params=_cparams(("arbitrary",)),
        name="moe_combine",
    )(base, tinfo, spos_col, x2, ln_g.reshape(1, d), ln_b.reshape(1, d), y)


def _sorted_rows(tb):
    return TOP_K * tb + N_EXPERTS * ROW_ALIGN


def _moe_ffn_ln(x2, layer, router_w, router_b, w_gu, b_gu, w_down, b_down, ln_g, ln_b):
    n, _ = x2.shape
    n_tiles = n // min(MOE_TILE, n)
    spos, gate, tinfo, seg = _router(x2, router_w, router_b)
    seg = seg.reshape(N_EXPERTS)
    max_rows = n * TOP_K + n_tiles * N_EXPERTS * (ROW_ALIGN - 1) + N_EXPERTS * (MOE_BLOCK - 1)
    n_blocks = (max_rows + MOE_BLOCK - 1) // MOE_BLOCK
    padded = (seg + MOE_BLOCK - 1) // MOE_BLOCK * MOE_BLOCK
    padded_end = jnp.cumsum(padded)
    base = jnp.concatenate([jnp.zeros((1,), jnp.int32), padded_end]).astype(jnp.int32)
    n_used = (padded_end[-1:] // MOE_BLOCK).astype(jnp.int32)
    block_start = jnp.arange(n_blocks, dtype=jnp.int32) * MOE_BLOCK
    block_expert = jnp.sum(block_start[:, None] >= padded_end[None, :], axis=1).astype(jnp.int32)
    last_expert = jnp.sum(padded_end[-1] - 1 >= padded_end).astype(jnp.int32)
    block_expert = jnp.minimum(block_expert, last_expert)
    xs = _dispatch(x2, spos, gate, tinfo, base, seg, n_blocks * MOE_BLOCK)
    y = _experts(xs, block_expert, n_used, layer, w_gu, b_gu, w_down, b_down)
    return _combine(y, x2, spos.T, tinfo, base, ln_g, ln_b)


def kernel(x, ln_g, ln_b, nsa_w_in, nsa_cmp_pos, nsa_cmp_w1, nsa_cmp_b1, nsa_cmp_w2, nsa_cmp_b2, nsa_gate_b, nsa_w_out, ml_w_in, ml_conv_w, ml_conv_b, ml_gate_b, ml_norm_g, ml_w_out, hg_w_in, hg_lower, hg_norm_g, hg_w_out, router_w, router_b, moe_w_gu, moe_b_gu, moe_w_down, moe_b_down):
    bsz, t, d = x.shape
    x2 = x.reshape(bsz * t, d)
    for layer in range(DEPTH):
        kind, slot = layer % N_MIXERS, layer // N_MIXERS
        if kind == 0:
            x2 = _nsa_mixer(x2, bsz, t, nsa_w_in[slot], nsa_cmp_pos[slot], nsa_cmp_w1[slot], nsa_cmp_b1[slot],
                            nsa_cmp_w2[slot], nsa_cmp_b2[slot], nsa_gate_b[slot], nsa_w_out[slot],
                            ln_g[layer, 0], ln_b[layer, 0])
        elif kind == 1:
            x2 = _mlstm_mixer(x2, bsz, t, ml_w_in[slot], ml_conv_w[slot], ml_conv_b[slot], ml_gate_b[slot],
                              ml_norm_g[slot], ml_w_out[slot], ln_g[layer, 0], ln_b[layer, 0])
        else:
            x2 = _hgrn_mixer(x2, bsz, t, layer, hg_w_in[slot], hg_lower, hg_norm_g[slot], hg_w_out[slot],
                             ln_g[layer, 0], ln_b[layer, 0])
        x2 = _moe_ffn_ln(x2, layer, router_w[layer], router_b[layer], moe_w_gu, moe_b_gu, moe_w_down, moe_b_down,
                         ln_g[layer, 1], ln_b[layer, 1])
    return x2.reshape(bsz, t, d)
```

```python
import functools

import numpy as np
import jax
import jax.numpy as jnp
from jax import lax
from jax.experimental import pallas as pl
from jax.experimental.pallas import tpu as pltpu

F32 = jnp.float32
BF16 = jnp.bfloat16
HIGHEST = lax.Precision.HIGHEST

D_MODEL = 1024
DEPTH = 4
N_MIXERS = 3

NSA_HEADS = 16
NSA_KV_GROUPS = 4
NSA_HEAD_DIM = 64
NSA_HPG = 4
NSA_KV_WIDTH = 256
CMP_BLOCK = 32
CMP_STRIDE = 16
CMP_HIDDEN = 256
SEL_BLOCK = 64
SEL_TOPK = 8
WINDOW = 512

MLSTM_HEADS = 4
MLSTM_QK_DIM = 128
MLSTM_V_DIM = 256
MLSTM_CONV = 4

HGRN_HEADS = 8
HGRN_DIM = 128
HGRN_CHUNK = 32

N_EXPERTS = 32
TOP_K = 4
D_FF = 1024
SWIGLU_LIMIT = 7.0
SWIGLU_ALPHA = 1.702

LN_EPS = 1e-5
RMS_EPS = 1e-6
DEEPNORM_ALPHA = (2 * DEPTH) ** 0.25
NEG = -1e30
M_FLOOR = -1e20

LANE = 128
VMEM_LIMIT = 56 * 1024 * 1024

PROJ_TM = 512
MOE_TILE = 512
MOE_BLOCK = 256
ROW_ALIGN = 8
SORT_ROWS = 256
NSA_TQ = 128
NSA_TK = 512
MLSTM_L = 512
HGRN_SUPER = 128


def _cparams(sem):
    return pltpu.CompilerParams(dimension_semantics=sem, vmem_limit_bytes=VMEM_LIMIT)


def _dot(a, b):
    return jnp.dot(a, b, preferred_element_type=F32)


def _dot_nt(a, b, precision=None):
    return lax.dot_general(a, b, (((1,), (1,)), ((), ())), precision=precision, preferred_element_type=F32)


def _dot_tn(a, b):
    return lax.dot_general(a, b, (((0,), (0,)), ((), ())), preferred_element_type=F32)


def _layer_norm(z, g, b):
    mu = jnp.mean(z, axis=-1, keepdims=True)
    zc = z - mu
    var = jnp.mean(zc * zc, axis=-1, keepdims=True)
    return zc * lax.rsqrt(var + LN_EPS) * g + b


def _proj_kernel(x_ref, *refs, n_w, n_t):
    w_refs = refs[:n_w]
    wt_refs = refs[n_w:n_w + n_t]
    o_refs = refs[n_w + n_t:2 * n_w + n_t]
    ot_refs = refs[2 * n_w + n_t:]
    x = x_ref[...]
    xb = x.astype(BF16)
    for w_ref, o_ref in zip(w_refs, o_refs):
        ncol = w_ref.shape[1]
        for c0 in range(0, ncol, 512):
            c1 = min(c0 + 512, ncol)
            o_ref[:, c0:c1] = _dot(xb, w_ref[:, c0:c1]).astype(o_ref.dtype)
    for wt_ref, o_ref in zip(wt_refs, ot_refs):
        o_ref[...] = _dot_nt(wt_ref[...], x, precision=HIGHEST)


def _proj(x2, weights, out_dtypes, weights_t=()):
    n, d = x2.shape
    tm = min(PROJ_TM, n)
    in_specs = [pl.BlockSpec((tm, d), lambda i: (i, 0))]
    in_specs += [pl.BlockSpec(w.shape, lambda i: (0, 0)) for w in weights]
    in_specs += [pl.BlockSpec(w.shape, lambda i: (0, 0)) for w in weights_t]
    out_shape = [jax.ShapeDtypeStruct((n, w.shape[1]), dt) for w, dt in zip(weights, out_dtypes)]
    out_shape += [jax.ShapeDtypeStruct((w.shape[0], n), F32) for w in weights_t]
    out_specs = [pl.BlockSpec((tm, w.shape[1]), lambda i: (i, 0)) for w in weights]
    out_specs += [pl.BlockSpec((w.shape[0], tm), lambda i: (0, i)) for w in weights_t]
    return pl.pallas_call(
        functools.partial(_proj_kernel, n_w=len(weights), n_t=len(weights_t)),
        grid=(n // tm,),
        in_specs=in_specs,
        out_specs=out_specs,
        out_shape=out_shape,
        compiler_params=_cparams(("parallel",)),
        name="proj",
    )(x2, *weights, *weights_t)


def _outproj_ln_kernel(h_ref, w_ref, x_ref, g_ref, b_ref, o_ref):
    y = _dot(h_ref[...], w_ref[...])
    o_ref[...] = _layer_norm(DEEPNORM_ALPHA * x_ref[...] + y, g_ref[...], b_ref[...])


def _outproj_ln(h2, w, x2, g, b):
    n, d = x2.shape
    tm = min(PROJ_TM, n)
    return pl.pallas_call(
        _outproj_ln_kernel,
        grid=(n // tm,),
        in_specs=[
            pl.BlockSpec((tm, d), lambda i: (i, 0)),
            pl.BlockSpec((d, d), lambda i: (0, 0)),
            pl.BlockSpec((tm, d), lambda i: (i, 0)),
            pl.BlockSpec((1, d), lambda i: (0, 0)),
            pl.BlockSpec((1, d), lambda i: (0, 0)),
        ],
        out_specs=pl.BlockSpec((tm, d), lambda i: (i, 0)),
        out_shape=jax.ShapeDtypeStruct((n, d), F32),
        compiler_params=_cparams(("parallel",)),
        name="outproj_ln",
    )(h2, w, x2, g.reshape(1, d), b.reshape(1, d))


def _gelu_tanh(x):
    return 0.5 * x * (1.0 + jnp.tanh(np.sqrt(2.0 / np.pi).astype(np.float32) * (x + 0.044715 * (x * x * x))))


def _nsa_compress_kernel(kv_ref, wcat_ref, posa_ref, posb_ref, b1_ref, w2_ref, b2_ref, o_ref):
    n_row = kv_ref.shape[1] // CMP_STRIDE
    half = 2 * CMP_HIDDEN
    acc_a = jnp.zeros((n_row, half), F32)
    acc_b = jnp.zeros((n_row, half), F32)
    for r in range(CMP_STRIDE):
        xr = kv_ref[0, pl.ds(r, n_row, stride=CMP_STRIDE), :]
        w = wcat_ref[r]
        acc_a = acc_a + _dot((xr + posa_ref[r]).astype(BF16), w[:, :half])
        acc_b = acc_b + _dot((xr + posb_ref[r]).astype(BF16), w[:, half:])
    hidden = acc_a + pltpu.roll(acc_b, n_row - 1, 0) + b1_ref[...]
    hidden = _gelu_tanh(hidden)
    o_ref[0, 0] = (_dot(hidden.astype(BF16), w2_ref[...]) + b2_ref[...]).astype(o_ref.dtype)


def _nsa_compress(kvc, wcat, posa, posb, b1cat, w2cat, b2cat):
    bsz, t, _ = kvc.shape
    n_row = t // CMP_STRIDE
    return pl.pallas_call(
        _nsa_compress_kernel,
        grid=(bsz, NSA_KV_GROUPS),
        in_specs=[
            pl.BlockSpec((1, t, LANE), lambda b, g: (b, 0, g)),
            pl.BlockSpec(wcat.shape, lambda b, g: (0, 0, 0)),
            pl.BlockSpec(posa.shape, lambda b, g: (0, 0, 0)),
            pl.BlockSpec(posb.shape, lambda b, g: (0, 0, 0)),
            pl.BlockSpec(b1cat.shape, lambda b, g: (0, 0)),
            pl.BlockSpec(w2cat.shape, lambda b, g: (0, 0)),
            pl.BlockSpec(b2cat.shape, lambda b, g: (0, 0)),
        ],
        out_specs=pl.BlockSpec((1, 1, n_row, LANE), lambda b, g: (b, g, 0, 0)),
        out_shape=jax.ShapeDtypeStruct((bsz, NSA_KV_GROUPS, n_row, LANE), BF16),
        compiler_params=_cparams(("parallel", "parallel")),
        name="nsa_compress",
    )(kvc, wcat, posa, posb, b1cat, w2cat, b2cat)


def _nsa_attn_kernel(q_ref, kvc_ref, kvs_ref, kvw_ref, gate_ref, gb_ref, c2s_ref, o_ref, *, tq, tk, n_cmp):
    def tile(qi, carry):
        _nsa_attn_tile(qi, q_ref, kvc_ref, kvs_ref, kvw_ref, gate_ref, gb_ref, c2s_ref, o_ref, tq=tq, tk=tk, n_cmp=n_cmp)
        return carry
    lax.fori_loop(0, q_ref.shape[1] // tq, tile, 0)


def _nsa_attn_tile(qi, q_ref, kvc_ref, kvs_ref, kvw_ref, gate_ref, gb_ref, c2s_ref, o_ref, *, tq, tk, n_cmp):
    hpg = NSA_HPG
    t0 = pl.multiple_of(qi * tq, tq)
    q4 = jnp.concatenate([q_ref[0, pl.ds(t0, tq), h * LANE:(h + 1) * LANE] for h in range(hpg)], axis=0)
    q4 = q4 * jnp.asarray(NSA_HEAD_DIM ** -0.5, BF16)
    t_col = t0 + lax.broadcasted_iota(jnp.int32, (tq, 1), 0)
    lane = lax.broadcasted_iota(jnp.int32, (1, LANE), 1)
    ones_lane = (lane == 0).astype(BF16)

    def with_ones(kv):
        return jnp.where(lane < NSA_HEAD_DIM, ones_lane, kv)

    span = WINDOW + tq
    w0 = pl.multiple_of(jnp.maximum(t0 - WINDOW, 0), tq)
    kvw = kvw_ref[0, pl.ds(w0, span), :]
    kvw1 = with_ones(kvw)
    kvc = kvc_ref[0, 0]
    n_c = kvc.shape[0]
    q_heads = [q4[h * tq:(h + 1) * tq] for h in range(hpg)]
    s_c = [_dot_nt(q_heads[h], kvc) for h in range(hpg)]
    s_w = [_dot_nt(q_heads[h], kvw).astype(BF16) for h in range(hpg)]

    c_lane = lax.broadcasted_iota(jnp.int32, (1, n_c), 1)
    valid_c = (c_lane * CMP_STRIDE + (CMP_BLOCK - 1) <= t_col) & (c_lane < n_cmp)
    p_heads = []
    for h in range(hpg):
        sm = jnp.where(valid_c, s_c[h], NEG)
        m = jnp.max(sm, axis=-1, keepdims=True)
        p = jnp.where(valid_c, jnp.exp(sm - m), 0.0)
        l = jnp.sum(p, axis=-1, keepdims=True)
        p_heads.append(p / jnp.maximum(l, 1e-30))
    p_sum = p_heads[0]
    for h in range(1, hpg):
        p_sum = p_sum + p_heads[h]
    imp = _dot_nt(c2s_ref[...], p_sum, precision=HIGHEST)
    o_c = [_dot(p_heads[h].astype(BF16), kvc) for h in range(hpg)]

    lag = t_col - (w0 + lax.broadcasted_iota(jnp.int32, (1, span), 1))
    bias_w = jnp.where((lag >= 0) & (lag < WINDOW), 0.0, NEG).astype(BF16)
    o_w = []
    for h in range(hpg):
        sm = s_w[h] + bias_w
        o_w.append(_dot(jnp.exp(sm - jnp.max(sm, axis=-1, keepdims=True)), kvw1))

    n_sel = imp.shape[0]
    blk = lax.broadcasted_iota(jnp.int32, (n_sel, 1), 0)
    cur = (t0 + lax.broadcasted_iota(jnp.int32, (1, tq), 1)) // SEL_BLOCK
    forced = (blk == 0) | (blk == cur) | (blk == cur - 1)
    score = jnp.where(forced, -NEG, jnp.where(blk <= cur, imp, NEG))
    rank = jnp.zeros((n_sel, tq), F32)
    for i in range(n_sel):
        s_i = score[i:i + 1, :]
        ahead = (s_i > score) | ((s_i == score) & (blk > i))
        rank = rank + ahead.astype(F32)
    member = ((rank < SEL_TOPK) & (score > 0.5 * NEG)).astype(BF16)

    blk_per_tile = tk // SEL_BLOCK
    key_lane = lax.broadcasted_iota(jnp.int32, (1, tk), 1)

    def sel_body(kt, carry):
        k0 = pl.multiple_of(kt * tk, tk)
        kv = kvs_ref[0, pl.ds(k0, tk), :]
        kv1 = with_ones(kv)
        expand = (key_lane // SEL_BLOCK + kt * blk_per_tile == blk).astype(BF16)
        valid = (_dot_tn(member, expand) > 0.5) & (k0 + key_lane <= t_col)
        bias = jnp.where(valid, 0.0, NEG).astype(BF16)
        s = [_dot_nt(q_heads[h], kv).astype(BF16) for h in range(hpg)]
        new = []
        for h in range(hpg):
            m_old, acc_old = carry[h]
            sm = s[h] + bias
            m_new = jnp.maximum(m_old, jnp.max(sm, axis=-1, keepdims=True).astype(F32))
            pv = _dot(jnp.exp(sm - m_new.astype(BF16)), kv1)
            new.append((m_new, jnp.exp(m_old - m_new) * acc_old + pv))
        return tuple(new)

    init = tuple((jnp.full((tq, 1), M_FLOOR, F32), jnp.zeros((tq, LANE), F32)) for _ in range(hpg))
    n_kt = (t0 + tq - 1) // tk + 1
    sel = lax.fori_loop(0, n_kt, sel_body, init)

    gates = jax.nn.sigmoid(gate_ref[0, pl.ds(t0, tq), :].astype(F32) + gb_ref[0])
    outs = []
    for h in range(hpg):
        o_s = sel[h][1]
        outs.append(gates[:, 3 * h:3 * h + 1] * o_c[h]
                    + gates[:, 3 * h + 1:3 * h + 2] * (o_s / o_s[:, 0:1])
                    + gates[:, 3 * h + 2:3 * h + 3] * (o_w[h] / o_w[h][:, 0:1]))
    for j in range(hpg // 2):
        pair = jnp.where(lane < NSA_HEAD_DIM, pltpu.roll(outs[2 * j], NSA_HEAD_DIM, 1), outs[2 * j + 1])
        o_ref[0, pl.ds(t0, tq), j * LANE:(j + 1) * LANE] = pair.astype(o_ref.dtype)


def _nsa_attention(main, kvcmp, gate_b, c2s, bsz, t):
    tq = min(NSA_TQ, t)
    tk = min(NSA_TK, t)
    n_cmp = (t - CMP_BLOCK) // CMP_STRIDE + 1
    q_blocks = NSA_HEADS
    g = NSA_KV_GROUPS
    kernel = functools.partial(_nsa_attn_kernel, tq=tq, tk=tk, n_cmp=n_cmp)
    return pl.pallas_call(
        kernel,
        grid=(bsz, g),
        in_specs=[
            pl.BlockSpec((1, t, NSA_HPG * LANE), lambda b, gi: (b, 0, gi)),
            pl.BlockSpec((1, 1) + kvcmp.shape[2:], lambda b, gi: (b, gi, 0, 0)),
            pl.BlockSpec((1, t, LANE), lambda b, gi: (b, 0, q_blocks + gi)),
            pl.BlockSpec((1, t, LANE), lambda b, gi: (b, 0, q_blocks + g + gi)),
            pl.BlockSpec((1, t, LANE), lambda b, gi: (b, 0, q_blocks + 2 * g + gi)),
            pl.BlockSpec((1, 1, LANE), lambda b, gi: (gi, 0, 0)),
            pl.BlockSpec(c2s.shape, lambda b, gi: (0, 0)),
        ],
        out_specs=pl.BlockSpec((1, t, NSA_HPG * NSA_HEAD_DIM), lambda b, gi: (b, 0, gi)),
        out_shape=jax.ShapeDtypeStruct((bsz, t, D_MODEL), BF16),
        compiler_params=_cparams(("parallel", "parallel")),
        name="nsa_attention",
    )(main, kvcmp, main, main, main, gate_b, c2s)


def _nsa_mixer(x2, bsz, t, w_in, cmp_pos, cmp_w1, cmp_b1, cmp_w2, cmp_b2, gate_b, w_out, ln_g, ln_b):
    d, g, dh, hpg = D_MODEL, NSA_KV_GROUPS, NSA_HEAD_DIM, NSA_HPG
    kvw_ = NSA_KV_WIDTH

    def pair(k0):
        k = w_in[:, k0:k0 + kvw_].reshape(d, g, 1, dh)
        v = w_in[:, k0 + kvw_:k0 + 2 * kvw_].reshape(d, g, 1, dh)
        return jnp.concatenate([k, v], axis=2).reshape(d, g * LANE)

    wq = jnp.pad(w_in[:, :d].reshape(d, NSA_HEADS, dh), ((0, 0), (0, 0), (0, LANE - dh))).reshape(d, NSA_HEADS * LANE)
    n_gate = 3 * hpg
    wg = jnp.pad(w_in[:, d + 6 * kvw_:].reshape(d, g, n_gate), ((0, 0), (0, 0), (0, LANE - n_gate))).reshape(d, g * LANE)
    w_main = jnp.concatenate([wq, pair(d + 2 * kvw_), pair(d + 4 * kvw_), wg], axis=1).astype(BF16)
    w_kvc = pair(d).astype(BF16)
    main, kvc = _proj(x2, [w_main, w_kvc], [BF16, F32])

    half = CMP_BLOCK // 2
    w1 = cmp_w1.reshape(2, CMP_BLOCK, dh, CMP_HIDDEN)
    z = jnp.zeros((half, dh, CMP_HIDDEN), F32)
    top = jnp.concatenate([w1[0, :half], z, w1[0, half:], z], axis=2)
    bot = jnp.concatenate([z, w1[1, :half], z, w1[1, half:]], axis=2)
    wcat = jnp.concatenate([top, bot], axis=1).astype(BF16)
    pos = jnp.concatenate([cmp_pos[0], cmp_pos[1]], axis=-1)
    posa, posb = pos[:half, None, :], pos[half:, None, :]
    b1cat = cmp_b1.reshape(1, 2 * CMP_HIDDEN)
    zz = jnp.zeros((CMP_HIDDEN, dh), F32)
    w2cat = jnp.concatenate([jnp.concatenate([cmp_w2[0], zz], axis=1),
                             jnp.concatenate([zz, cmp_w2[1]], axis=1)], axis=0).astype(BF16)
    b2cat = cmp_b2.reshape(1, 2 * dh)
    kvcmp = _nsa_compress(kvc.reshape(bsz, t, g * LANE), wcat, posa, posb, b1cat, w2cat, b2cat)

    n_row = t // CMP_STRIDE
    n_sel = t // SEL_BLOCK
    cmp_start = np.arange(n_row) * CMP_STRIDE
    sel_start = np.arange(n_sel) * SEL_BLOCK
    overlap = (np.minimum(cmp_start[:, None] + CMP_BLOCK, sel_start[None, :] + SEL_BLOCK)
               - np.maximum(cmp_start[:, None], sel_start[None, :]))
    c2s = jnp.asarray((np.clip(overlap, 0, None) / CMP_STRIDE).astype(np.float32).T)
    gb = jnp.pad(gate_b.reshape(g, 1, n_gate), ((0, 0), (0, 0), (0, LANE - n_gate)))
    o = _nsa_attention(main.reshape(bsz, t, -1), kvcmp, gb, c2s, bsz, t)
    return _outproj_ln(o.reshape(bsz * t, d), w_out.astype(BF16), x2, ln_g, ln_b)


def _mlstm_kernel(q_ref, k_ref, v_ref, og_ref, gif_ref, gb_ref, cwq_ref, cwk_ref, cbq_ref, cbk_ref, ng_ref,
                  o_ref, pad_s, q_s, k_s, ct_s, *, chunk):
    t = q_ref.shape[1]
    dk, dv = MLSTM_QK_DIM, MLSTM_V_DIM
    head = pl.program_id(1)
    halo = 8

    def conv_silu(x_ref, w_ref, b_ref, dst, scale):
        pad_s[0:halo, :] = jnp.zeros((halo, dk), F32)
        pad_s[halo:halo + t, :] = x_ref[0]
        rows = min(t, 256)
        for r0 in range(0, t, rows):
            y = b_ref[...] + w_ref[0:1, :] * pad_s[pl.ds(halo + r0 - (MLSTM_CONV - 1), rows), :]
            for j in range(1, MLSTM_CONV):
                y = y + w_ref[j:j + 1, :] * pad_s[pl.ds(halo + r0 - (MLSTM_CONV - 1) + j, rows), :]
            y = y * jax.nn.sigmoid(y)
            dst[r0:r0 + rows, :] = (y * scale).astype(dst.dtype)

    conv_silu(q_ref, cwq_ref, cbq_ref, q_s, 1.0)
    conv_silu(k_ref, cwk_ref, cbk_ref, k_s, dk ** -0.5)
    ct_s[...] = jnp.zeros((dk, dv), F32)

    row = lax.broadcasted_iota(jnp.int32, (chunk, chunk), 0)
    col = lax.broadcasted_iota(jnp.int32, (chunk, chunk), 1)
    causal = col <= row
    eye = col == row
    upper = (row <= col).astype(F32)
    b_i = gb_ref[pl.ds(head, 1), :]
    b_f = gb_ref[pl.ds(MLSTM_HEADS + head, 1), :]
    norm_g = ng_ref[...]

    def body(c, carry):
        n_row, m_prev = carry
        r0 = pl.multiple_of(c * chunk, chunk)
        qc = q_s[pl.ds(r0, chunk), :]
        kc = k_s[pl.ds(r0, chunk), :]
        vc = v_ref[0, pl.ds(r0, chunk), :]
        li_row = gif_ref[0, c, pl.ds(head, 1), :] + b_i
        zf = gif_ref[0, c, pl.ds(MLSTM_HEADS + head, 1), :] + b_f
        lf_row = jnp.minimum(zf, 0.0) - jnp.log1p(jnp.exp(-jnp.abs(zf)))
        b_row = jnp.dot(lf_row, upper, precision=HIGHEST, preferred_element_type=F32)
        b_col = jnp.sum(jnp.where(causal, lf_row, 0.0), axis=-1, keepdims=True)
        li_col = jnp.sum(jnp.where(eye, li_row, 0.0), axis=-1, keepdims=True)
        d_mat = jnp.where(causal, b_col - b_row + li_row, NEG)
        m_t = jnp.maximum(b_col + m_prev, jnp.max(d_mat, axis=-1, keepdims=True))
        w_inter = jnp.exp(b_col + m_prev - m_t)
        s = _dot_nt(qc, kc) * jnp.exp(d_mat - m_t)
        num = _dot(s.astype(BF16), vc) + w_inter * _dot(qc, ct_s[...].astype(BF16))
        den = jnp.sum(s, axis=-1, keepdims=True) + w_inter * jnp.sum(qc.astype(F32) * n_row, axis=-1, keepdims=True)
        h = num / jnp.maximum(jnp.abs(den), jnp.exp(-m_t))
        b_end = b_col[chunk - 1:chunk, :]
        decay = b_end - b_col + li_col
        m_new = jnp.maximum(b_end + m_prev, jnp.max(decay, axis=0, keepdims=True))
        w_k = jnp.exp(decay - m_new)
        scale = jnp.exp(b_end + m_prev - m_new)
        ct_s[...] = scale * ct_s[...] + _dot_tn(kc, (vc.astype(F32) * w_k).astype(BF16))
        n_new = scale * n_row + jnp.sum(w_k * kc.astype(F32), axis=0, keepdims=True)
        hn = h * lax.rsqrt(jnp.mean(h * h, axis=-1, keepdims=True) + RMS_EPS) * norm_g
        og = og_ref[0, pl.ds(r0, chunk), :].astype(F32)
        o_ref[0, pl.ds(r0, chunk), :] = (hn * jax.nn.sigmoid(og)).astype(o_ref.dtype)
        return n_new, m_new

    lax.fori_loop(0, t // chunk, body, (jnp.zeros((1, dk), F32), jnp.zeros((1, 1), F32)))


def _mlstm_core(qk, v, og, gif, gate_b, conv_w, conv_b, norm_g, bsz, t, chunk):
    hh, dk, dv = MLSTM_HEADS, MLSTM_QK_DIM, MLSTM_V_DIM
    nc = t // chunk
    kernel = functools.partial(_mlstm_kernel, chunk=chunk)
    return pl.pallas_call(
        kernel,
        grid=(bsz, hh),
        in_specs=[
            pl.BlockSpec((1, t, dk), lambda b, h: (b, 0, h)),
            pl.BlockSpec((1, t, dk), lambda b, h: (b, 0, hh + h)),
            pl.BlockSpec((1, t, dv), lambda b, h: (b, 0, h)),
            pl.BlockSpec((1, t, dv), lambda b, h: (b, 0, h)),
            pl.BlockSpec((1, nc, 2 * hh, chunk), lambda b, h: (b, 0, 0, 0)),
            pl.BlockSpec((2 * hh, 1), lambda b, h: (0, 0)),
            pl.BlockSpec((MLSTM_CONV, dk), lambda b, h: (0, h)),
            pl.BlockSpec((MLSTM_CONV, dk), lambda b, h: (0, hh + h)),
            pl.BlockSpec((1, dk), lambda b, h: (0, h)),
            pl.BlockSpec((1, dk), lambda b, h: (0, hh + h)),
            pl.BlockSpec((1, dv), lambda b, h: (0, h)),
        ],
        out_specs=pl.BlockSpec((1, t, dv), lambda b, h: (b, 0, h)),
        out_shape=jax.ShapeDtypeStruct((bsz, t, hh * dv), BF16),
        scratch_shapes=[
            pltpu.VMEM((t + 8, dk), F32),
            pltpu.VMEM((t, dk), BF16),
            pltpu.VMEM((t, dk), BF16),
            pltpu.VMEM((dk, dv), F32),
        ],
        compiler_params=_cparams(("parallel", "parallel")),
        name="mlstm",
    )(qk, qk, v, og, gif, gate_b.reshape(2 * hh, 1), conv_w, conv_w, conv_b.reshape(1, -1), conv_b.reshape(1, -1),
      norm_g.reshape(1, -1))


def _mlstm_mixer(x2, bsz, t, w_in, conv_w, conv_b, gate_b, norm_g, w_out, ln_g, ln_b):
    d, hh = D_MODEL, MLSTM_HEADS
    qkw = 2 * hh * MLSTM_QK_DIM
    vw = hh * MLSTM_V_DIM
    chunk = min(MLSTM_L, t)
    w_qk = w_in[:, :qkw].astype(BF16)
    w_v = w_in[:, qkw:qkw + vw].astype(BF16)
    w_og = w_in[:, qkw + vw:qkw + vw + d].astype(BF16)
    w_gif_t = w_in[:, qkw + vw + d:].T
    qk, v, og, gif_t = _proj(x2, [w_qk, w_v, w_og], [F32, BF16, BF16], [w_gif_t])
    gif = gif_t.reshape(2 * hh, bsz, t // chunk, chunk).transpose(1, 2, 0, 3)
    h = _mlstm_core(qk.reshape(bsz, t, qkw), v.reshape(bsz, t, vw), og.reshape(bsz, t, d), gif, gate_b,
                    conv_w, conv_b, norm_g, bsz, t, chunk)
    return _outproj_ln(h.reshape(bsz * t, d), w_out.astype(BF16), x2, ln_g, ln_b)


def _hgrn_kernel(q_ref, f_ref, i_ref, g_ref, low_ref, ng_ref, o_ref, st_s, qd_s, kd_s, ke_s, dec_s, *,
                 layer_idx, heads):
    t = q_ref.shape[1]
    dh, chunk = HGRN_DIM, HGRN_CHUNK
    low = low_ref[...]
    e = jnp.exp(low - jnp.max(low, axis=0, keepdims=True))
    soft = e / jnp.sum(e, axis=0, keepdims=True)
    lb = jnp.zeros((1, heads * dh), F32)
    for r in range(1, layer_idx + 1):
        lb = lb + soft[r:r + 1, :]
    norm_g = ng_ref[...]
    st_s[...] = jnp.zeros(st_s.shape, F32)
    width = heads * dh
    sup = min(HGRN_SUPER, t)
    n_sub = sup // chunk

    in_chunk = lax.broadcasted_iota(jnp.int32, (sup, 1), 0) % chunk

    def prep(s, carry):
        r0 = pl.multiple_of(s * sup, sup)
        f = lb + (1.0 - lb) * jax.nn.sigmoid(f_ref[0, pl.ds(r0, sup), :])
        gcum = jnp.log(f)
        shift = 1
        while shift < chunk:
            gcum = gcum + jnp.where(in_chunk >= shift, pltpu.roll(gcum, shift, 0), 0.0)
            shift *= 2
        g_end = jnp.concatenate(
            [jnp.broadcast_to(gcum[(j + 1) * chunk - 1:(j + 1) * chunk, :], (chunk, width)) for j in range(n_sub)], axis=0)
        qv = q_ref[0, pl.ds(r0, sup), :].astype(F32)
        kk = 1.0 - f
        qd_s[pl.ds(r0, sup), :] = (qv * jax.nn.sigmoid(qv) * jnp.exp(gcum)).astype(BF16)
        kd_s[pl.ds(r0, sup), :] = (kk * jnp.exp(-gcum)).astype(BF16)
        ke_s[pl.ds(r0, sup), :] = (kk * jnp.exp(g_end - gcum)).astype(BF16)
        for j in range(n_sub):
            dec_s[s, j:j + 1, :] = jnp.exp(gcum[(j + 1) * chunk - 1:(j + 1) * chunk, :])
        return carry

    lax.fori_loop(0, t // sup, prep, 0)

    row = lax.broadcasted_iota(jnp.int32, (sup, sup), 0)
    col = lax.broadcasted_iota(jnp.int32, (sup, sup), 1)
    keep = (col <= row) & (col // chunk == row // chunk)

    def body(s, carry):
        r0 = pl.multiple_of(s * sup, sup)
        decay = dec_s[s]
        col_of = [slice(h * dh, (h + 1) * dh) for h in range(heads)]
        sub_of = [slice(j * chunk, (j + 1) * chunk) for j in range(n_sub)]
        q_dec = [qd_s[pl.ds(r0, sup), c] for c in col_of]
        vv = [i_ref[0, pl.ds(r0, sup), c] for c in col_of]
        score = [_dot_nt(q_dec[h], kd_s[pl.ds(r0, sup), col_of[h]]) for h in range(heads)]
        k_end = [ke_s[pl.ds(r0, sup), c] for c in col_of]
        kv = [[_dot_tn(vv[h][r], k_end[h][r]) for r in sub_of] for h in range(heads)]
        intra = [_dot(jnp.where(keep, score[h], 0.0).astype(BF16), vv[h]) for h in range(heads)]
        states = []
        for h in range(heads):
            st = st_s[h]
            per_sub = []
            for j in range(n_sub):
                per_sub.append(st.astype(BF16))
                st = st * decay[j:j + 1, col_of[h]] + kv[h][j]
            st_s[h] = st
            states.append(per_sub)
        for h in range(heads):
            inter = [_dot_nt(q_dec[h][sub_of[j]], states[h][j]) for j in range(n_sub)]
            o = intra[h] + jnp.concatenate(inter, axis=0)
            on = o * lax.rsqrt(jnp.mean(o * o, axis=-1, keepdims=True) + RMS_EPS) * norm_g[:, col_of[h]]
            gate = jax.nn.sigmoid(g_ref[0, pl.ds(r0, sup), col_of[h]].astype(F32))
            o_ref[0, pl.ds(r0, sup), col_of[h]] = (on * gate).astype(o_ref.dtype)
        return carry

    lax.fori_loop(0, t // sup, body, 0)


def _hgrn_core(q, f, i, g, lower, norm_g, bsz, t, layer_idx):
    heads = 4
    width = heads * HGRN_DIM
    n_grp = HGRN_HEADS // heads
    kernel = functools.partial(_hgrn_kernel, layer_idx=layer_idx, heads=heads)
    act = pl.BlockSpec((1, t, width), lambda b, j: (b, 0, j))
    return pl.pallas_call(
        kernel,
        grid=(bsz, n_grp),
        in_specs=[act, act, act, act,
                  pl.BlockSpec((DEPTH, width), lambda b, j: (0, j)),
                  pl.BlockSpec((1, width), lambda b, j: (0, j))],
        out_specs=act,
        out_shape=jax.ShapeDtypeStruct((bsz, t, D_MODEL), BF16),
        scratch_shapes=[pltpu.VMEM((heads, HGRN_DIM, HGRN_DIM), F32), pltpu.VMEM((t, width), BF16),
                        pltpu.VMEM((t, width), BF16), pltpu.VMEM((t, width), BF16),
                        pltpu.VMEM((max(t // HGRN_SUPER, 1), min(HGRN_SUPER, t) // HGRN_CHUNK, width), F32)],
        compiler_params=_cparams(("parallel", "parallel")),
        name="hgrn2",
    )(q, f, i, g, lower, norm_g.reshape(1, -1))


def _hgrn_mixer(x2, bsz, t, layer_idx, w_in, lower, norm_g, w_out, ln_g, ln_b):
    d = D_MODEL
    ws = [w_in[:, j * d:(j + 1) * d].astype(BF16) for j in range(4)]
    q, f, i, g = _proj(x2, ws, [BF16, F32, BF16, BF16])
    shp = (bsz, t, d)
    o = _hgrn_core(q.reshape(shp), f.reshape(shp), i.reshape(shp), g.reshape(shp), lower, norm_g, bsz, t, layer_idx)
    return _outproj_ln(o.reshape(bsz * t, d), w_out.astype(BF16), x2, ln_g, ln_b)


def _router_kernel(x_ref, rwt_ref, rb_ref, spos_ref, gate_ref, tinfo_ref, seg_ref, carry_s):
    @pl.when(pl.program_id(0) == 0)
    def _():
        carry_s[...] = jnp.zeros(carry_s.shape, F32)

    tm = x_ref.shape[0]
    ne = N_EXPERTS
    logits = _dot_nt(rwt_ref[...], x_ref[...], precision=HIGHEST) + rb_ref[...]
    e_iota = lax.broadcasted_iota(jnp.int32, (ne, tm), 0)
    work = logits
    vals, picks = [], []
    for k in range(TOP_K):
        mx = jnp.max(work, axis=0, keepdims=True)
        idx = jnp.min(jnp.where(work == mx, e_iota, ne), axis=0, keepdims=True)
        pick = e_iota == idx
        vals.append(mx)
        picks.append(pick)
        work = jnp.where(pick, -jnp.inf, work)
    exps = [jnp.exp(v - vals[0]) for v in vals]
    tot = exps[0]
    for k in range(1, TOP_K):
        tot = tot + exps[k]
    for k in range(TOP_K):
        gate_ref[k:k + 1, :] = exps[k] / tot
    hot = picks[0].astype(F32)
    for k in range(1, TOP_K):
        hot = hot + picks[k].astype(F32)
    before = (lax.broadcasted_iota(jnp.int32, (tm, tm), 0) < lax.broadcasted_iota(jnp.int32, (tm, tm), 1)).astype(BF16)
    prior = _dot(hot.astype(BF16), before)
    cnt = jnp.sum(hot, axis=1, keepdims=True)
    run = jnp.floor((cnt + (ROW_ALIGN - 1)) * (1.0 / ROW_ALIGN)) * ROW_ALIGN
    sub = lax.broadcasted_iota(jnp.int32, (ne, ne), 0)
    lan = lax.broadcasted_iota(jnp.int32, (ne, ne), 1)
    run_row = jnp.sum(jnp.where(sub == lan, run, 0.0), axis=0, keepdims=True)
    soff = jnp.sum(jnp.where(lan < sub, run_row, 0.0), axis=1, keepdims=True)
    for k in range(TOP_K):
        pos = jnp.sum(jnp.where(picks[k], prior + soff, 0.0), axis=0, keepdims=True)
        spos_ref[k:k + 1, :] = pos.astype(jnp.int32)
    sub = lax.broadcasted_iota(jnp.int32, (ne, LANE), 0)
    lan = lax.broadcasted_iota(jnp.int32, (ne, LANE), 1)
    field, le = lan // ne, lan % ne
    carry = carry_s[...]
    info = (jnp.where((field == 0) & (sub == le), carry, 0.0) + jnp.where((field == 1) & (sub == le), run, 0.0)
            + jnp.where((field == 2) & (sub < le), run, 0.0)
            + jnp.where((field == 3) & (le == 0), run * (1.0 / ROW_ALIGN), 0.0))
    tinfo_ref[0] = jnp.sum(info, axis=0, keepdims=True).astype(jnp.int32)
    total = carry + run
    carry_s[...] = total
    seg_ref[...] = total.astype(jnp.int32)


def _router(x2, router_w, router_b):
    n, d = x2.shape
    tm = min(MOE_TILE, n)
    n_tiles = n // tm
    row = pl.BlockSpec((TOP_K, tm), lambda i: (0, i))
    return pl.pallas_call(
        _router_kernel,
        grid=(n_tiles,),
        in_specs=[
            pl.BlockSpec((tm, d), lambda i: (i, 0)),
            pl.BlockSpec((N_EXPERTS, d), lambda i: (0, 0)),
            pl.BlockSpec((N_EXPERTS, 1), lambda i: (0, 0)),
        ],
        out_specs=[row, row, pl.BlockSpec((1, 1, LANE), lambda i: (i, 0, 0)),
                   pl.BlockSpec((N_EXPERTS, 1), lambda i: (0, 0))],
        out_shape=[
            jax.ShapeDtypeStruct((TOP_K, n), jnp.int32),
            jax.ShapeDtypeStruct((TOP_K, n), F32),
            jax.ShapeDtypeStruct((n_tiles, 1, LANE), jnp.int32),
            jax.ShapeDtypeStruct((N_EXPERTS, 1), jnp.int32),
        ],
        scratch_shapes=[pltpu.VMEM((N_EXPERTS, 1), F32)],
        compiler_params=_cparams(("arbitrary",)),
        name="router",
    )(x2, router_w.T, router_b.reshape(N_EXPERTS, 1))


def _group_copy(src, s, dst, d, sem):
    return pltpu.make_async_copy(src.at[pl.ds(pl.multiple_of(s, ROW_ALIGN), ROW_ALIGN), :],
                                 dst.at[pl.ds(pl.multiple_of(d, ROW_ALIGN), ROW_ALIGN), :], sem)


def _for_each_group(tinfo_ref, base_ref, fn):
    def per_expert(e, _):
        seg_row = base_ref[e] + tinfo_ref[0, 0, e]
        groups = tinfo_ref[0, 0, N_EXPERTS + e] // ROW_ALIGN
        sorted_row = tinfo_ref[0, 0, 2 * N_EXPERTS + e]

        def per_group(j, _):
            fn(sorted_row + j * ROW_ALIGN, seg_row + j * ROW_ALIGN)
            return 0
        return lax.fori_loop(0, groups, per_group, 0)
    lax.fori_loop(0, N_EXPERTS, per_expert, 0)


def _wait_groups(count, src, dst, sem):
    def one(j, _):
        _group_copy(src, 0, dst, 0, sem).wait()
        return 0
    lax.fori_loop(0, count, one, 0)


def _dispatch_kernel(base_ref, seg_ref, tinfo_ref, spos_ref, gate_ref, x_ref, xs_ref, sort2_s, zero_s, pending_s, sems):
    tb, d = x_ref.shape
    s_rows = sort2_s.shape[1]
    step = pl.program_id(0)
    sort_s, sem = sort2_s.at[step % 2], sems.at[step % 2]
    prev_sem = sems.at[(step + 1) % 2]

    @pl.when(step == 0)
    def _():
        zero_s[...] = jnp.zeros(zero_s.shape, F32)
        for fill in (True, False):
            def per_expert(e, _):
                def per_group(r, _):
                    cp = _group_copy(zero_s, 0, xs_ref, r * ROW_ALIGN, sem)
                    cp.start() if fill else cp.wait()
                    return 0
                first = jnp.where(e < N_EXPERTS, base_ref[jnp.minimum(e, N_EXPERTS - 1)] + seg_ref[jnp.minimum(e, N_EXPERTS - 1)],
                                  base_ref[N_EXPERTS])
                last = jnp.where(e < N_EXPERTS, base_ref[jnp.minimum(e + 1, N_EXPERTS)], xs_ref.shape[0])
                return lax.fori_loop(first // ROW_ALIGN, last // ROW_ALIGN, per_group, 0)
            lax.fori_loop(0, N_EXPERTS + 1, per_expert, 0)

    xb = x_ref[...].astype(BF16)
    spos = spos_ref[...]
    gate = gate_ref[...]
    for r0 in range(0, s_rows, SORT_ROWS):
        r_iota = r0 + lax.broadcasted_iota(jnp.int32, (SORT_ROWS, tb), 0)
        hits = [spos[k:k + 1, :] == r_iota for k in range(TOP_K)]
        onehot = jnp.where(hits[0] | hits[1] | hits[2] | hits[3], 1.0, 0.0).astype(BF16)
        sort_s[r0:r0 + SORT_ROWS, :d] = _dot(onehot, xb)
        g_sel = jnp.where(hits[0], gate[0:1, :], 0.0)
        for k in range(1, TOP_K):
            g_sel = g_sel + jnp.where(hits[k], gate[k:k + 1, :], 0.0)
        sort_s[r0:r0 + SORT_ROWS, d:] = jnp.broadcast_to(jnp.sum(g_sel, axis=1, keepdims=True), (SORT_ROWS, LANE))

    _for_each_group(tinfo_ref, base_ref, lambda s, r: _group_copy(sort_s, s, xs_ref, r, sem).start())
    groups = tinfo_ref[0, 0, 3 * N_EXPERTS]

    @pl.when(step > 0)
    def _():
        _wait_groups(pending_s[0], sort_s, xs_ref, prev_sem)

    pending_s[0] = groups

    @pl.when(step == pl.num_programs(0) - 1)
    def _():
        _wait_groups(groups, sort_s, xs_ref, sem)


def _tile_info_spec(n_tiles, ahead=0):
    return pl.BlockSpec((1, 1, LANE), lambda i, *_: (jnp.minimum(i + ahead, n_tiles - 1), 0, 0),
                        memory_space=pltpu.SMEM)


def _dispatch(x2, spos, gate, tinfo, base, seg, rows):
    n, d = x2.shape
    tb = min(MOE_TILE, n)
    row = pl.BlockSpec((TOP_K, tb), lambda i, *_: (0, i))
    return pl.pallas_call(
        _dispatch_kernel,
        grid_spec=pltpu.PrefetchScalarGridSpec(
            num_scalar_prefetch=2,
            grid=(n // tb,),
            in_specs=[_tile_info_spec(n // tb), row, row, pl.BlockSpec((tb, d), lambda i, *_: (i, 0))],
            out_specs=pl.BlockSpec(memory_space=pl.ANY),
            scratch_shapes=[pltpu.VMEM((2, _sorted_rows(tb), d + LANE), F32), pltpu.VMEM((ROW_ALIGN, d + LANE), F32),
                            pltpu.SMEM((1,), jnp.int32), pltpu.SemaphoreType.DMA((2,))],
        ),
        out_shape=jax.ShapeDtypeStruct((rows, d + LANE), F32),
        compiler_params=_cparams(("arbitrary",)),
        name="moe_dispatch",
    )(base, seg, tinfo, spos, gate, x2)


def _expert_kernel(be_ref, nused_ref, x_ref, wgu_ref, bgu_ref, wd_ref, bd_ref, y_ref, wgu_s, wd_s):
    i = pl.program_id(0)
    used = i < nused_ref[0]

    @pl.when(used & ((i == 0) | (be_ref[i] != be_ref[jnp.maximum(i - 1, 0)])))
    def _():
        wgu_s[...] = wgu_ref[0, 0].astype(BF16)
        wd_s[...] = wd_ref[0, 0].astype(BF16)

    @pl.when(used)
    def _():
        d = y_ref.shape[1]
        xb = x_ref[:, :d].astype(BF16)
        h = _dot(xb, wgu_s[...]) + bgu_ref[0, 0]
        h_gate = jnp.minimum(h[:, :D_FF], SWIGLU_LIMIT)
        h_up = jnp.clip(h[:, D_FF:], -SWIGLU_LIMIT, SWIGLU_LIMIT)
        act = (h_up + 1.0) * h_gate * jax.nn.sigmoid(SWIGLU_ALPHA * h_gate)
        y_ref[...] = (_dot(act.astype(BF16), wd_s[...]) + bd_ref[0, 0]) * x_ref[:, d:d + 1]

    @pl.when(jnp.logical_not(used))
    def _():
        y_ref[...] = jnp.zeros(y_ref.shape, F32)


def _experts(xs, block_expert, n_used, layer, w_gu, b_gu, w_down, b_down):
    rows = xs.shape[0]
    d = D_MODEL
    n_blocks = rows // MOE_BLOCK
    f2 = w_gu.shape[3]
    depth = w_gu.shape[0]
    xmap = lambda i, be, nu: (jnp.minimum(i, nu[0] - 1), 0)
    return pl.pallas_call(
        _expert_kernel,
        grid_spec=pltpu.PrefetchScalarGridSpec(
            num_scalar_prefetch=2,
            grid=(n_blocks,),
            in_specs=[
                pl.BlockSpec((MOE_BLOCK, d + LANE), xmap),
                pl.BlockSpec((1, 1, d, f2), lambda i, be, nu: (layer, be[i], 0, 0)),
                pl.BlockSpec((1, 1, 1, f2), lambda i, be, nu: (layer, be[i], 0, 0)),
                pl.BlockSpec((1, 1, D_FF, d), lambda i, be, nu: (layer, be[i], 0, 0)),
                pl.BlockSpec((1, 1, 1, d), lambda i, be, nu: (layer, be[i], 0, 0)),
            ],
            out_specs=pl.BlockSpec((MOE_BLOCK, d), lambda i, be, nu: (i, 0)),
            scratch_shapes=[pltpu.VMEM((d, f2), BF16), pltpu.VMEM((D_FF, d), BF16)],
        ),
        out_shape=jax.ShapeDtypeStruct((rows, d), F32),
        compiler_params=_cparams(("arbitrary",)),
        name="moe_experts",
    )(block_expert, n_used, xs, w_gu, b_gu.reshape(depth, N_EXPERTS, 1, f2), w_down,
      b_down.reshape(depth, N_EXPERTS, 1, d))


def _combine_kernel(base_ref, tinfo_ref, tnext_ref, spos_ref, x_ref, g_ref, b_ref, y_ref, o_ref, ybuf2_s, yhi_s, ylo_s,
                    p_s, sems):
    tb = x_ref.shape[0]
    s_rows = ybuf2_s.shape[1]
    step = pl.program_id(0)
    ybuf_s, sem = ybuf2_s.at[step % 2], sems.at[step % 2]
    ynext_s, next_sem = ybuf2_s.at[(step + 1) % 2], sems.at[(step + 1) % 2]

    @pl.when(step == 0)
    def _():
        ybuf2_s[...] = jnp.zeros(ybuf2_s.shape, F32)
        _for_each_group(tinfo_ref, base_ref, lambda s, r: _group_copy(y_ref, r, ybuf_s, s, sem).start())

    @pl.when(step + 1 < pl.num_programs(0))
    def _():
        _for_each_group(tnext_ref, base_ref, lambda s, r: _group_copy(y_ref, r, ynext_s, s, next_sem).start())

    spos = spos_ref[...]
    for c0 in range(0, s_rows, SORT_ROWS):
        c_iota = c0 + lax.broadcasted_iota(jnp.int32, (tb, SORT_ROWS), 1)
        hit = spos[:, 0:1] == c_iota
        for k in range(1, TOP_K):
            hit = hit | (spos[:, k:k + 1] == c_iota)
        p_s[:, c0:c0 + SORT_ROWS] = jnp.where(hit, 1.0, 0.0).astype(BF16)

    _wait_groups(tinfo_ref[0, 0, 3 * N_EXPERTS], y_ref, ybuf_s, sem)
    for c0 in range(0, s_rows, SORT_ROWS):
        yc = ybuf_s[c0:c0 + SORT_ROWS, :]
        hi = yc.astype(BF16)
        yhi_s[c0:c0 + SORT_ROWS, :] = hi
        ylo_s[c0:c0 + SORT_ROWS, :] = (yc - hi.astype(F32)).astype(BF16)
    p = p_s[...]
    moe = _dot(p, yhi_s[...]) + _dot(p, ylo_s[...])
    o_ref[...] = _layer_norm(DEEPNORM_ALPHA * x_ref[...] + moe, g_ref[...], b_ref[...])


def _combine(y, x2, spos_col, tinfo, base, ln_g, ln_b):
    n, d = x2.shape
    tb = min(MOE_TILE, n)
    s_rows = _sorted_rows(tb)
    return pl.pallas_call(
        _combine_kernel,
        grid_spec=pltpu.PrefetchScalarGridSpec(
            num_scalar_prefetch=1,
            grid=(n // tb,),
            in_specs=[
                _tile_info_spec(n // tb), _tile_info_spec(n // tb, ahead=1),
                pl.BlockSpec((tb, TOP_K), lambda i, *_: (i, 0)),
                pl.BlockSpec((tb, d), lambda i, *_: (i, 0)),
                pl.BlockSpec((1, d), lambda i, *_: (0, 0)),
                pl.BlockSpec((1, d), lambda i, *_: (0, 0)),
                pl.BlockSpec(memory_space=pl.ANY),
            ],
            out_specs=pl.BlockSpec((tb, d), lambda i, *_: (i, 0)),
            scratch_shapes=[pltpu.VMEM((2, s_rows, d), F32), pltpu.VMEM((s_rows, d), BF16), pltpu.VMEM((s_rows, d), BF16),
                            pltpu.VMEM((tb, s_rows), BF16), pltpu.SemaphoreType.DMA((2,))],
        ),
        out_shape=jax.ShapeDtypeStruct((n, d), F32),
        compiler_params=_cparams(("arbitrary",)),
        name="moe_combine",
    )(base, tinfo, tinfo, spos_col, x2, ln_g.reshape(1, d), ln_b.reshape(1, d), y)


def _sorted_rows(tb):
    return TOP_K * tb + N_EXPERTS * ROW_ALIGN


def _moe_ffn_ln(x2, layer, router_w, router_b, w_gu, b_gu, w_down, b_down, ln_g, ln_b):
    n, _ = x2.shape
    n_tiles = n // min(MOE_TILE, n)
    spos, gate, tinfo, seg = _router(x2, router_w, router_b)
    seg = seg.reshape(N_EXPERTS)
    max_rows = n * TOP_K + n_tiles * N_EXPERTS * (ROW_ALIGN - 1) + N_EXPERTS * (MOE_BLOCK - 1)
    n_blocks = (max_rows + MOE_BLOCK - 1) // MOE_BLOCK
    padded = (seg + MOE_BLOCK - 1) // MOE_BLOCK * MOE_BLOCK
    padded_end = jnp.cumsum(padded)
    base = jnp.concatenate([jnp.zeros((1,), jnp.int32), padded_end]).astype(jnp.int32)
    n_used = (padded_end[-1:] // MOE_BLOCK).astype(jnp.int32)
    block_start = jnp.arange(n_blocks, dtype=jnp.int32) * MOE_BLOCK
    block_expert = jnp.sum(block_start[:, None] >= padded_end[None, :], axis=1).astype(jnp.int32)
    last_expert = jnp.sum(padded_end[-1] - 1 >= padded_end).astype(jnp.int32)
    block_expert = jnp.minimum(block_expert, last_expert)
    xs = _dispatch(x2, spos, gate, tinfo, base, seg, n_blocks * MOE_BLOCK)
    y = _experts(xs, block_expert, n_used, layer, w_gu, b_gu, w_down, b_down)
    return _combine(y, x2, spos.T, tinfo, base, ln_g, ln_b)


def kernel(x, ln_g, ln_b, nsa_w_in, nsa_cmp_pos, nsa_cmp_w1, nsa_cmp_b1, nsa_cmp_w2, nsa_cmp_b2, nsa_gate_b, nsa_w_out, ml_w_in, ml_conv_w, ml_conv_b, ml_gate_b, ml_norm_g, ml_w_out, hg_w_in, hg_lower, hg_norm_g, hg_w_out, router_w, router_b, moe_w_gu, moe_b_gu, moe_w_down, moe_b_down):
    bsz, t, d = x.shape
    x2 = x.reshape(bsz * t, d)
    for layer in range(DEPTH):
        kind, slot = layer % N_MIXERS, layer // N_MIXERS
        if kind == 0:
            x2 = _nsa_mixer(x2, bsz, t, nsa_w_in[slot], nsa_cmp_pos[slot], nsa_cmp_w1[slot], nsa_cmp_b1[slot],
                            nsa_cmp_w2[slot], nsa_cmp_b2[slot], nsa_gate_b[slot], nsa_w_out[slot],
                            ln_g[layer, 0], ln_b[layer, 0])
        elif kind == 1:
            x2 = _mlstm_mixer(x2, bsz, t, ml_w_in[slot], ml_conv_w[slot], ml_conv_b[slot], ml_gate_b[slot],
                              ml_norm_g[slot], ml_w_out[slot], ln_g[layer, 0], ln_b[layer, 0])
        else:
            x2 = _hgrn_mixer(x2, bsz, t, layer, hg_w_in[slot], hg_lower, hg_norm_g[slot], hg_w_out[slot],
                             ln_g[layer, 0], ln_b[layer, 0])
        x2 = _moe_ffn_ln(x2, layer, router_w[layer], router_b[layer], moe_w_gu, moe_b_gu, moe_w_down, moe_b_down,
                         ln_g[layer, 1], ln_b[layer, 1])
    return x2.reshape(bsz, t, d)
```

```python
import functools

import numpy as np
import jax
import jax.numpy as jnp
from jax import lax
from jax.experimental import pallas as pl
from jax.experimental.pallas import tpu as pltpu

F32 = jnp.float32
BF16 = jnp.bfloat16
HIGHEST = lax.Precision.HIGHEST

D_MODEL = 1024
DEPTH = 4
N_MIXERS = 3

NSA_HEADS = 16
NSA_KV_GROUPS = 4
NSA_HEAD_DIM = 64
NSA_HPG = 4
NSA_KV_WIDTH = 256
CMP_BLOCK = 32
CMP_STRIDE = 16
CMP_HIDDEN = 256
SEL_BLOCK = 64
SEL_TOPK = 8
WINDOW = 512

MLSTM_HEADS = 4
MLSTM_QK_DIM = 128
MLSTM_V_DIM = 256
MLSTM_CONV = 4

HGRN_HEADS = 8
HGRN_DIM = 128
HGRN_CHUNK = 32

N_EXPERTS = 32
TOP_K = 4
D_FF = 1024
SWIGLU_LIMIT = 7.0
SWIGLU_ALPHA = 1.702

LN_EPS = 1e-5
RMS_EPS = 1e-6
DEEPNORM_ALPHA = (2 * DEPTH) ** 0.25
NEG = -1e30
M_FLOOR = -1e20

LANE = 128
VMEM_LIMIT = 56 * 1024 * 1024

PROJ_TM = 512
MOE_TILE = 512
MOE_BLOCK = 256
ROW_ALIGN = 8
SORT_ROWS = 256
COMBINE_SPLIT = 3
NSA_TQ = 128
NSA_TK = 512
MLSTM_L = 512
HGRN_SUPER = 128


def _cparams(sem):
    return pltpu.CompilerParams(dimension_semantics=sem, vmem_limit_bytes=VMEM_LIMIT)


def _dot(a, b):
    return jnp.dot(a, b, preferred_element_type=F32)


def _dot_nt(a, b, precision=None):
    return lax.dot_general(a, b, (((1,), (1,)), ((), ())), precision=precision, preferred_element_type=F32)


def _dot_tn(a, b):
    return lax.dot_general(a, b, (((0,), (0,)), ((), ())), preferred_element_type=F32)


def _layer_norm(z, g, b):
    mu = jnp.mean(z, axis=-1, keepdims=True)
    zc = z - mu
    var = jnp.mean(zc * zc, axis=-1, keepdims=True)
    return zc * lax.rsqrt(var + LN_EPS) * g + b


def _proj_kernel(x_ref, *refs, n_w, n_t):
    w_refs = refs[:n_w]
    wt_refs = refs[n_w:n_w + n_t]
    o_refs = refs[n_w + n_t:2 * n_w + n_t]
    ot_refs = refs[2 * n_w + n_t:]
    x = x_ref[...]
    xb = x.astype(BF16)
    for w_ref, o_ref in zip(w_refs, o_refs):
        ncol = w_ref.shape[1]
        for c0 in range(0, ncol, 512):
            c1 = min(c0 + 512, ncol)
            o_ref[:, c0:c1] = _dot(xb, w_ref[:, c0:c1]).astype(o_ref.dtype)
    for wt_ref, o_ref in zip(wt_refs, ot_refs):
        o_ref[...] = _dot_nt(wt_ref[...], x, precision=HIGHEST)


def _proj(x2, weights, out_dtypes, weights_t=()):
    n, d = x2.shape
    tm = min(PROJ_TM, n)
    in_specs = [pl.BlockSpec((tm, d), lambda i: (i, 0))]
    in_specs += [pl.BlockSpec(w.shape, lambda i: (0, 0)) for w in weights]
    in_specs += [pl.BlockSpec(w.shape, lambda i: (0, 0)) for w in weights_t]
    out_shape = [jax.ShapeDtypeStruct((n, w.shape[1]), dt) for w, dt in zip(weights, out_dtypes)]
    out_shape += [jax.ShapeDtypeStruct((w.shape[0], n), F32) for w in weights_t]
    out_specs = [pl.BlockSpec((tm, w.shape[1]), lambda i: (i, 0)) for w in weights]
    out_specs += [pl.BlockSpec((w.shape[0], tm), lambda i: (0, i)) for w in weights_t]
    return pl.pallas_call(
        functools.partial(_proj_kernel, n_w=len(weights), n_t=len(weights_t)),
        grid=(n // tm,),
        in_specs=in_specs,
        out_specs=out_specs,
        out_shape=out_shape,
        compiler_params=_cparams(("parallel",)),
        name="proj",
    )(x2, *weights, *weights_t)


def _outproj_ln_kernel(h_ref, w_ref, x_ref, g_ref, b_ref, o_ref):
    y = _dot(h_ref[...], w_ref[...])
    o_ref[...] = _layer_norm(DEEPNORM_ALPHA * x_ref[...] + y, g_ref[...], b_ref[...])


def _outproj_ln(h2, w, x2, g, b):
    n, d = x2.shape
    tm = min(PROJ_TM, n)
    return pl.pallas_call(
        _outproj_ln_kernel,
        grid=(n // tm,),
        in_specs=[
            pl.BlockSpec((tm, d), lambda i: (i, 0)),
            pl.BlockSpec((d, d), lambda i: (0, 0)),
            pl.BlockSpec((tm, d), lambda i: (i, 0)),
            pl.BlockSpec((1, d), lambda i: (0, 0)),
            pl.BlockSpec((1, d), lambda i: (0, 0)),
        ],
        out_specs=pl.BlockSpec((tm, d), lambda i: (i, 0)),
        out_shape=jax.ShapeDtypeStruct((n, d), F32),
        compiler_params=_cparams(("parallel",)),
        name="outproj_ln",
    )(h2, w, x2, g.reshape(1, d), b.reshape(1, d))


def _gelu_tanh(x):
    return 0.5 * x * (1.0 + jnp.tanh(np.sqrt(2.0 / np.pi).astype(np.float32) * (x + 0.044715 * (x * x * x))))


def _nsa_compress_kernel(kv_ref, wcat_ref, posa_ref, posb_ref, b1_ref, w2_ref, b2_ref, o_ref):
    n_row = kv_ref.shape[1] // CMP_STRIDE
    half = 2 * CMP_HIDDEN
    acc_a = jnp.zeros((n_row, half), F32)
    acc_b = jnp.zeros((n_row, half), F32)
    for r in range(CMP_STRIDE):
        xr = kv_ref[0, pl.ds(r, n_row, stride=CMP_STRIDE), :]
        w = wcat_ref[r]
        acc_a = acc_a + _dot((xr + posa_ref[r]).astype(BF16), w[:, :half])
        acc_b = acc_b + _dot((xr + posb_ref[r]).astype(BF16), w[:, half:])
    hidden = acc_a + pltpu.roll(acc_b, n_row - 1, 0) + b1_ref[...]
    hidden = _gelu_tanh(hidden)
    o_ref[0, 0] = (_dot(hidden.astype(BF16), w2_ref[...]) + b2_ref[...]).astype(o_ref.dtype)


def _nsa_compress(kvc, wcat, posa, posb, b1cat, w2cat, b2cat):
    bsz, t, _ = kvc.shape
    n_row = t // CMP_STRIDE
    return pl.pallas_call(
        _nsa_compress_kernel,
        grid=(bsz, NSA_KV_GROUPS),
        in_specs=[
            pl.BlockSpec((1, t, LANE), lambda b, g: (b, 0, g)),
            pl.BlockSpec(wcat.shape, lambda b, g: (0, 0, 0)),
            pl.BlockSpec(posa.shape, lambda b, g: (0, 0, 0)),
            pl.BlockSpec(posb.shape, lambda b, g: (0, 0, 0)),
            pl.BlockSpec(b1cat.shape, lambda b, g: (0, 0)),
            pl.BlockSpec(w2cat.shape, lambda b, g: (0, 0)),
            pl.BlockSpec(b2cat.shape, lambda b, g: (0, 0)),
        ],
        out_specs=pl.BlockSpec((1, 1, n_row, LANE), lambda b, g: (b, g, 0, 0)),
        out_shape=jax.ShapeDtypeStruct((bsz, NSA_KV_GROUPS, n_row, LANE), BF16),
        compiler_params=_cparams(("parallel", "parallel")),
        name="nsa_compress",
    )(kvc, wcat, posa, posb, b1cat, w2cat, b2cat)


def _nsa_attn_kernel(q_ref, kvc_ref, kvs_ref, kvw_ref, gate_ref, gb_ref, c2s_ref, o_ref, *, tq, tk, n_cmp):
    def tile(qi, carry):
        _nsa_attn_tile(qi, q_ref, kvc_ref, kvs_ref, kvw_ref, gate_ref, gb_ref, c2s_ref, o_ref, tq=tq, tk=tk, n_cmp=n_cmp)
        return carry
    lax.fori_loop(0, q_ref.shape[1] // tq, tile, 0)


def _nsa_attn_tile(qi, q_ref, kvc_ref, kvs_ref, kvw_ref, gate_ref, gb_ref, c2s_ref, o_ref, *, tq, tk, n_cmp):
    hpg = NSA_HPG
    t0 = pl.multiple_of(qi * tq, tq)
    q4 = jnp.concatenate([q_ref[0, pl.ds(t0, tq), h * LANE:(h + 1) * LANE] for h in range(hpg)], axis=0)
    q4 = q4 * jnp.asarray(NSA_HEAD_DIM ** -0.5, BF16)
    t_col = t0 + lax.broadcasted_iota(jnp.int32, (tq, 1), 0)
    lane = lax.broadcasted_iota(jnp.int32, (1, LANE), 1)
    ones_lane = (lane == 0).astype(BF16)

    def with_ones(kv):
        return jnp.where(lane < NSA_HEAD_DIM, ones_lane, kv)

    gates = jax.nn.sigmoid(gate_ref[0, pl.ds(t0, tq), :].astype(F32) + gb_ref[0])
    gate_of = [[jnp.broadcast_to(gates[:, 3 * h + j:3 * h + j + 1], (tq, LANE)) for j in range(3)] for h in range(hpg)]

    span = WINDOW + tq
    w0 = pl.multiple_of(jnp.maximum(t0 - WINDOW, 0), tq)
    kvw = kvw_ref[0, pl.ds(w0, span), :]
    kvw1 = with_ones(kvw)
    kvc = kvc_ref[0, 0]
    n_c = kvc.shape[0]
    q_heads = [q4[h * tq:(h + 1) * tq] for h in range(hpg)]
    s_c = [_dot_nt(q_heads[h], kvc) for h in range(hpg)]
    s_w = [_dot_nt(q_heads[h], kvw).astype(BF16) for h in range(hpg)]

    c_lane = lax.broadcasted_iota(jnp.int32, (1, n_c), 1)
    valid_c = (c_lane * CMP_STRIDE + (CMP_BLOCK - 1) <= t_col) & (c_lane < n_cmp)
    p_heads = []
    for h in range(hpg):
        sm = jnp.where(valid_c, s_c[h], NEG)
        m = jnp.max(sm, axis=-1, keepdims=True)
        p = jnp.where(valid_c, jnp.exp(sm - m), 0.0)
        l = jnp.sum(p, axis=-1, keepdims=True)
        p_heads.append(p / jnp.maximum(l, 1e-30))
    p_sum = p_heads[0]
    for h in range(1, hpg):
        p_sum = p_sum + p_heads[h]
    imp = _dot_nt(c2s_ref[...], p_sum, precision=HIGHEST)
    o_c = [_dot(p_heads[h].astype(BF16), kvc) for h in range(hpg)]

    lag = t_col - (w0 + lax.broadcasted_iota(jnp.int32, (1, span), 1))
    bias_w = jnp.where((lag >= 0) & (lag < WINDOW), 0.0, NEG).astype(BF16)
    o_w = []
    for h in range(hpg):
        sm = s_w[h] + bias_w
        o_w.append(_dot(jnp.exp(sm - jnp.max(sm, axis=-1, keepdims=True)), kvw1))
    partial = [gate_of[h][0] * o_c[h] + gate_of[h][2] * (o_w[h] / o_w[h][:, 0:1]) for h in range(hpg)]

    n_sel = imp.shape[0]
    blk = lax.broadcasted_iota(jnp.int32, (n_sel, 1), 0)
    cur = (t0 + lax.broadcasted_iota(jnp.int32, (1, tq), 1)) // SEL_BLOCK
    forced = (blk == 0) | (blk == cur) | (blk == cur - 1)
    score = jnp.where(forced, -NEG, jnp.where(blk <= cur, imp, NEG))
    rank = jnp.zeros((n_sel, tq), F32)
    for i in range(n_sel):
        s_i = score[i:i + 1, :]
        ahead = (s_i > score) | ((s_i == score) & (blk > i))
        rank = rank + ahead.astype(F32)
    member = ((rank < SEL_TOPK) & (score > 0.5 * NEG)).astype(BF16)

    blk_per_tile = tk // SEL_BLOCK
    key_lane = lax.broadcasted_iota(jnp.int32, (1, tk), 1)

    def sel_body(kt, carry):
        k0 = pl.multiple_of(kt * tk, tk)
        kv = kvs_ref[0, pl.ds(k0, tk), :]
        kv1 = with_ones(kv)
        expand = (key_lane // SEL_BLOCK + kt * blk_per_tile == blk).astype(BF16)
        valid = (_dot_tn(member, expand) > 0.5) & (k0 + key_lane <= t_col)
        bias = jnp.where(valid, 0.0, NEG).astype(BF16)
        s = [_dot_nt(q_heads[h], kv).astype(BF16) for h in range(hpg)]
        new = []
        for h in range(hpg):
            m_old, acc_old = carry[h]
            sm = s[h] + bias
            m_new = jnp.maximum(m_old, jnp.max(sm, axis=-1, keepdims=True).astype(F32))
            pv = _dot(jnp.exp(sm - m_new.astype(BF16)), kv1)
            new.append((m_new, jnp.exp(m_old - m_new) * acc_old + pv))
        return tuple(new)

    init = tuple((jnp.full((tq, 1), M_FLOOR, F32), jnp.zeros((tq, LANE), F32)) for _ in range(hpg))
    n_kt = (t0 + tq - 1) // tk + 1
    sel = lax.fori_loop(0, n_kt, sel_body, init)

    outs = []
    for h in range(hpg):
        o_s = sel[h][1]
        outs.append(partial[h] + gate_of[h][1] * (o_s / o_s[:, 0:1]))
    for j in range(hpg // 2):
        pair = jnp.where(lane < NSA_HEAD_DIM, pltpu.roll(outs[2 * j], NSA_HEAD_DIM, 1), outs[2 * j + 1])
        o_ref[0, pl.ds(t0, tq), j * LANE:(j + 1) * LANE] = pair.astype(o_ref.dtype)


def _nsa_attention(main, kvcmp, gate_b, c2s, bsz, t):
    tq = min(NSA_TQ, t)
    tk = min(NSA_TK, t)
    n_cmp = (t - CMP_BLOCK) // CMP_STRIDE + 1
    q_blocks = NSA_HEADS
    g = NSA_KV_GROUPS
    kernel = functools.partial(_nsa_attn_kernel, tq=tq, tk=tk, n_cmp=n_cmp)
    return pl.pallas_call(
        kernel,
        grid=(bsz, g),
        in_specs=[
            pl.BlockSpec((1, t, NSA_HPG * LANE), lambda b, gi: (b, 0, gi)),
            pl.BlockSpec((1, 1) + kvcmp.shape[2:], lambda b, gi: (b, gi, 0, 0)),
            pl.BlockSpec((1, t, LANE), lambda b, gi: (b, 0, q_blocks + gi)),
            pl.BlockSpec((1, t, LANE), lambda b, gi: (b, 0, q_blocks + g + gi)),
            pl.BlockSpec((1, t, LANE), lambda b, gi: (b, 0, q_blocks + 2 * g + gi)),
            pl.BlockSpec((1, 1, LANE), lambda b, gi: (gi, 0, 0)),
            pl.BlockSpec(c2s.shape, lambda b, gi: (0, 0)),
        ],
        out_specs=pl.BlockSpec((1, t, NSA_HPG * NSA_HEAD_DIM), lambda b, gi: (b, 0, gi)),
        out_shape=jax.ShapeDtypeStruct((bsz, t, D_MODEL), BF16),
        compiler_params=_cparams(("parallel", "parallel")),
        name="nsa_attention",
    )(main, kvcmp, main, main, main, gate_b, c2s)


def _nsa_mixer(x2, bsz, t, w_in, cmp_pos, cmp_w1, cmp_b1, cmp_w2, cmp_b2, gate_b, w_out, ln_g, ln_b):
    d, g, dh, hpg = D_MODEL, NSA_KV_GROUPS, NSA_HEAD_DIM, NSA_HPG
    kvw_ = NSA_KV_WIDTH

    def pair(k0):
        k = w_in[:, k0:k0 + kvw_].reshape(d, g, 1, dh)
        v = w_in[:, k0 + kvw_:k0 + 2 * kvw_].reshape(d, g, 1, dh)
        return jnp.concatenate([k, v], axis=2).reshape(d, g * LANE)

    wq = jnp.pad(w_in[:, :d].reshape(d, NSA_HEADS, dh), ((0, 0), (0, 0), (0, LANE - dh))).reshape(d, NSA_HEADS * LANE)
    n_gate = 3 * hpg
    wg = jnp.pad(w_in[:, d + 6 * kvw_:].reshape(d, g, n_gate), ((0, 0), (0, 0), (0, LANE - n_gate))).reshape(d, g * LANE)
    w_main = jnp.concatenate([wq, pair(d + 2 * kvw_), pair(d + 4 * kvw_), wg], axis=1).astype(BF16)
    w_kvc = pair(d).astype(BF16)
    main, kvc = _proj(x2, [w_main, w_kvc], [BF16, F32])

    half = CMP_BLOCK // 2
    w1 = cmp_w1.reshape(2, CMP_BLOCK, dh, CMP_HIDDEN)
    z = jnp.zeros((half, dh, CMP_HIDDEN), F32)
    top = jnp.concatenate([w1[0, :half], z, w1[0, half:], z], axis=2)
    bot = jnp.concatenate([z, w1[1, :half], z, w1[1, half:]], axis=2)
    wcat = jnp.concatenate([top, bot], axis=1).astype(BF16)
    pos = jnp.concatenate([cmp_pos[0], cmp_pos[1]], axis=-1)
    posa, posb = pos[:half, None, :], pos[half:, None, :]
    b1cat = cmp_b1.reshape(1, 2 * CMP_HIDDEN)
    zz = jnp.zeros((CMP_HIDDEN, dh), F32)
    w2cat = jnp.concatenate([jnp.concatenate([cmp_w2[0], zz], axis=1),
                             jnp.concatenate([zz, cmp_w2[1]], axis=1)], axis=0).astype(BF16)
    b2cat = cmp_b2.reshape(1, 2 * dh)
    kvcmp = _nsa_compress(kvc.reshape(bsz, t, g * LANE), wcat, posa, posb, b1cat, w2cat, b2cat)

    n_row = t // CMP_STRIDE
    n_sel = t // SEL_BLOCK
    cmp_start = np.arange(n_row) * CMP_STRIDE
    sel_start = np.arange(n_sel) * SEL_BLOCK
    overlap = (np.minimum(cmp_start[:, None] + CMP_BLOCK, sel_start[None, :] + SEL_BLOCK)
               - np.maximum(cmp_start[:, None], sel_start[None, :]))
    c2s = jnp.asarray((np.clip(overlap, 0, None) / CMP_STRIDE).astype(np.float32).T)
    gb = jnp.pad(gate_b.reshape(g, 1, n_gate), ((0, 0), (0, 0), (0, LANE - n_gate)))
    o = _nsa_attention(main.reshape(bsz, t, -1), kvcmp, gb, c2s, bsz, t)
    return _outproj_ln(o.reshape(bsz * t, d), w_out.astype(BF16), x2, ln_g, ln_b)


def _mlstm_kernel(q_ref, k_ref, v_ref, og_ref, gif_ref, gb_ref, cwq_ref, cwk_ref, cbq_ref, cbk_ref, ng_ref,
                  o_ref, pad_s, q_s, k_s, ct_s, *, chunk):
    t = q_ref.shape[1]
    dk, dv = MLSTM_QK_DIM, MLSTM_V_DIM
    head = pl.program_id(1)
    halo = 8

    def conv_silu(x_ref, w_ref, b_ref, dst, scale):
        pad_s[0:halo, :] = jnp.zeros((halo, dk), F32)
        pad_s[halo:halo + t, :] = x_ref[0]
        rows = min(t, 256)
        for r0 in range(0, t, rows):
            y = b_ref[...] + w_ref[0:1, :] * pad_s[pl.ds(halo + r0 - (MLSTM_CONV - 1), rows), :]
            for j in range(1, MLSTM_CONV):
                y = y + w_ref[j:j + 1, :] * pad_s[pl.ds(halo + r0 - (MLSTM_CONV - 1) + j, rows), :]
            y = y * jax.nn.sigmoid(y)
            dst[r0:r0 + rows, :] = (y * scale).astype(dst.dtype)

    conv_silu(q_ref, cwq_ref, cbq_ref, q_s, 1.0)
    conv_silu(k_ref, cwk_ref, cbk_ref, k_s, dk ** -0.5)
    ct_s[...] = jnp.zeros((dk, dv), F32)

    row = lax.broadcasted_iota(jnp.int32, (chunk, chunk), 0)
    col = lax.broadcasted_iota(jnp.int32, (chunk, chunk), 1)
    causal = col <= row
    eye = col == row
    upper = (row <= col).astype(F32)
    b_i = gb_ref[pl.ds(head, 1), :]
    b_f = gb_ref[pl.ds(MLSTM_HEADS + head, 1), :]
    norm_g = ng_ref[...]

    def body(c, carry):
        n_row, m_prev = carry
        r0 = pl.multiple_of(c * chunk, chunk)
        qc = q_s[pl.ds(r0, chunk), :]
        kc = k_s[pl.ds(r0, chunk), :]
        vc = v_ref[0, pl.ds(r0, chunk), :]
        li_row = gif_ref[0, c, pl.ds(head, 1), :] + b_i
        zf = gif_ref[0, c, pl.ds(MLSTM_HEADS + head, 1), :] + b_f
        lf_row = jnp.minimum(zf, 0.0) - jnp.log1p(jnp.exp(-jnp.abs(zf)))
        b_row = jnp.dot(lf_row, upper, precision=HIGHEST, preferred_element_type=F32)
        b_col = jnp.sum(jnp.where(causal, lf_row, 0.0), axis=-1, keepdims=True)
        li_col = jnp.sum(jnp.where(eye, li_row, 0.0), axis=-1, keepdims=True)
        d_mat = jnp.where(causal, b_col - b_row + li_row, NEG)
        m_t = jnp.maximum(b_col + m_prev, jnp.max(d_mat, axis=-1, keepdims=True))
        w_inter = jnp.exp(b_col + m_prev - m_t)
        s = _dot_nt(qc, kc) * jnp.exp(d_mat - m_t)
        num = _dot(s.astype(BF16), vc) + w_inter * _dot(qc, ct_s[...].astype(BF16))
        den = jnp.sum(s, axis=-1, keepdims=True) + w_inter * jnp.sum(qc.astype(F32) * n_row, axis=-1, keepdims=True)
        h = num / jnp.maximum(jnp.abs(den), jnp.exp(-m_t))
        b_end = b_col[chunk - 1:chunk, :]
        decay = b_end - b_col + li_col
        m_new = jnp.maximum(b_end + m_prev, jnp.max(decay, axis=0, keepdims=True))
        w_k = jnp.exp(decay - m_new)
        scale = jnp.exp(b_end + m_prev - m_new)
        ct_s[...] = scale * ct_s[...] + _dot_tn(kc, (vc.astype(F32) * w_k).astype(BF16))
        n_new = scale * n_row + jnp.sum(w_k * kc.astype(F32), axis=0, keepdims=True)
        hn = h * lax.rsqrt(jnp.mean(h * h, axis=-1, keepdims=True) + RMS_EPS) * norm_g
        og = og_ref[0, pl.ds(r0, chunk), :].astype(F32)
        o_ref[0, pl.ds(r0, chunk), :] = (hn * jax.nn.sigmoid(og)).astype(o_ref.dtype)
        return n_new, m_new

    lax.fori_loop(0, t // chunk, body, (jnp.zeros((1, dk), F32), jnp.zeros((1, 1), F32)))


def _mlstm_core(qk, v, og, gif, gate_b, conv_w, conv_b, norm_g, bsz, t, chunk):
    hh, dk, dv = MLSTM_HEADS, MLSTM_QK_DIM, MLSTM_V_DIM
    nc = t // chunk
    kernel = functools.partial(_mlstm_kernel, chunk=chunk)
    return pl.pallas_call(
        kernel,
        grid=(bsz, hh),
        in_specs=[
            pl.BlockSpec((1, t, dk), lambda b, h: (b, 0, h)),
            pl.BlockSpec((1, t, dk), lambda b, h: (b, 0, hh + h)),
            pl.BlockSpec((1, t, dv), lambda b, h: (b, 0, h)),
            pl.BlockSpec((1, t, dv), lambda b, h: (b, 0, h)),
            pl.BlockSpec((1, nc, 2 * hh, chunk), lambda b, h: (b, 0, 0, 0)),
            pl.BlockSpec((2 * hh, 1), lambda b, h: (0, 0)),
            pl.BlockSpec((MLSTM_CONV, dk), lambda b, h: (0, h)),
            pl.BlockSpec((MLSTM_CONV, dk), lambda b, h: (0, hh + h)),
            pl.BlockSpec((1, dk), lambda b, h: (0, h)),
            pl.BlockSpec((1, dk), lambda b, h: (0, hh + h)),
            pl.BlockSpec((1, dv), lambda b, h: (0, h)),
        ],
        out_specs=pl.BlockSpec((1, t, dv), lambda b, h: (b, 0, h)),
        out_shape=jax.ShapeDtypeStruct((bsz, t, hh * dv), BF16),
        scratch_shapes=[
            pltpu.VMEM((t + 8, dk), F32),
            pltpu.VMEM((t, dk), BF16),
            pltpu.VMEM((t, dk), BF16),
            pltpu.VMEM((dk, dv), F32),
        ],
        compiler_params=_cparams(("parallel", "parallel")),
        name="mlstm",
    )(qk, qk, v, og, gif, gate_b.reshape(2 * hh, 1), conv_w, conv_w, conv_b.reshape(1, -1), conv_b.reshape(1, -1),
      norm_g.reshape(1, -1))


def _mlstm_mixer(x2, bsz, t, w_in, conv_w, conv_b, gate_b, norm_g, w_out, ln_g, ln_b):
    d, hh = D_MODEL, MLSTM_HEADS
    qkw = 2 * hh * MLSTM_QK_DIM
    vw = hh * MLSTM_V_DIM
    chunk = min(MLSTM_L, t)
    w_qk = w_in[:, :qkw].astype(BF16)
    w_v = w_in[:, qkw:qkw + vw].astype(BF16)
    w_og = w_in[:, qkw + vw:qkw + vw + d].astype(BF16)
    w_gif_t = w_in[:, qkw + vw + d:].T
    qk, v, og, gif_t = _proj(x2, [w_qk, w_v, w_og], [F32, BF16, BF16], [w_gif_t])
    gif = gif_t.reshape(2 * hh, bsz, t // chunk, chunk).transpose(1, 2, 0, 3)
    h = _mlstm_core(qk.reshape(bsz, t, qkw), v.reshape(bsz, t, vw), og.reshape(bsz, t, d), gif, gate_b,
                    conv_w, conv_b, norm_g, bsz, t, chunk)
    return _outproj_ln(h.reshape(bsz * t, d), w_out.astype(BF16), x2, ln_g, ln_b)


def _hgrn_kernel(q_ref, f_ref, i_ref, g_ref, low_ref, ng_ref, o_ref, st_s, qd_s, kd_s, ke_s, dec_s, *,
                 layer_idx, heads):
    t = q_ref.shape[1]
    dh, chunk = HGRN_DIM, HGRN_CHUNK
    low = low_ref[...]
    e = jnp.exp(low - jnp.max(low, axis=0, keepdims=True))
    soft = e / jnp.sum(e, axis=0, keepdims=True)
    lb = jnp.zeros((1, heads * dh), F32)
    for r in range(1, layer_idx + 1):
        lb = lb + soft[r:r + 1, :]
    norm_g = ng_ref[...]
    st_s[...] = jnp.zeros(st_s.shape, F32)
    width = heads * dh
    sup = min(HGRN_SUPER, t)
    n_sub = sup // chunk

    in_chunk = lax.broadcasted_iota(jnp.int32, (sup, 1), 0) % chunk

    def prep(s, carry):
        r0 = pl.multiple_of(s * sup, sup)
        f = lb + (1.0 - lb) * jax.nn.sigmoid(f_ref[0, pl.ds(r0, sup), :])
        gcum = jnp.log(f)
        shift = 1
        while shift < chunk:
            gcum = gcum + jnp.where(in_chunk >= shift, pltpu.roll(gcum, shift, 0), 0.0)
            shift *= 2
        g_end = jnp.concatenate(
            [jnp.broadcast_to(gcum[(j + 1) * chunk - 1:(j + 1) * chunk, :], (chunk, width)) for j in range(n_sub)], axis=0)
        qv = q_ref[0, pl.ds(r0, sup), :].astype(F32)
        kk = 1.0 - f
        qd_s[pl.ds(r0, sup), :] = (qv * jax.nn.sigmoid(qv) * jnp.exp(gcum)).astype(BF16)
        kd_s[pl.ds(r0, sup), :] = (kk * jnp.exp(-gcum)).astype(BF16)
        ke_s[pl.ds(r0, sup), :] = (kk * jnp.exp(g_end - gcum)).astype(BF16)
        for j in range(n_sub):
            dec_s[s, j:j + 1, :] = jnp.exp(gcum[(j + 1) * chunk - 1:(j + 1) * chunk, :])
        return carry

    lax.fori_loop(0, t // sup, prep, 0)

    row = lax.broadcasted_iota(jnp.int32, (sup, sup), 0)
    col = lax.broadcasted_iota(jnp.int32, (sup, sup), 1)
    keep = (col <= row) & (col // chunk == row // chunk)

    def body(s, carry):
        r0 = pl.multiple_of(s * sup, sup)
        decay = dec_s[s]
        col_of = [slice(h * dh, (h + 1) * dh) for h in range(heads)]
        sub_of = [slice(j * chunk, (j + 1) * chunk) for j in range(n_sub)]
        q_dec = [qd_s[pl.ds(r0, sup), c] for c in col_of]
        vv = [i_ref[0, pl.ds(r0, sup), c] for c in col_of]
        score = [_dot_nt(q_dec[h], kd_s[pl.ds(r0, sup), col_of[h]]) for h in range(heads)]
        k_end = [ke_s[pl.ds(r0, sup), c] for c in col_of]
        kv = [[_dot_tn(vv[h][r], k_end[h][r]) for r in sub_of] for h in range(heads)]
        intra = [_dot(jnp.where(keep, score[h], 0.0).astype(BF16), vv[h]) for h in range(heads)]
        states = []
        for h in range(heads):
            st = st_s[h]
            per_sub = []
            for j in range(n_sub):
                per_sub.append(st.astype(BF16))
                st = st * decay[j:j + 1, col_of[h]] + kv[h][j]
            st_s[h] = st
            states.append(per_sub)
        for h in range(heads):
            inter = [_dot_nt(q_dec[h][sub_of[j]], states[h][j]) for j in range(n_sub)]
            o = intra[h] + jnp.concatenate(inter, axis=0)
            on = o * lax.rsqrt(jnp.mean(o * o, axis=-1, keepdims=True) + RMS_EPS) * norm_g[:, col_of[h]]
            gate = jax.nn.sigmoid(g_ref[0, pl.ds(r0, sup), col_of[h]].astype(F32))
            o_ref[0, pl.ds(r0, sup), col_of[h]] = (on * gate).astype(o_ref.dtype)
        return carry

    lax.fori_loop(0, t // sup, body, 0)


def _hgrn_core(q, f, i, g, lower, norm_g, bsz, t, layer_idx):
    heads = 4
    width = heads * HGRN_DIM
    n_grp = HGRN_HEADS // heads
    kernel = functools.partial(_hgrn_kernel, layer_idx=layer_idx, heads=heads)
    act = pl.BlockSpec((1, t, width), lambda b, j: (b, 0, j))
    return pl.pallas_call(
        kernel,
        grid=(bsz, n_grp),
        in_specs=[act, act, act, act,
                  pl.BlockSpec((DEPTH, width), lambda b, j: (0, j)),
                  pl.BlockSpec((1, width), lambda b, j: (0, j))],
        out_specs=act,
        out_shape=jax.ShapeDtypeStruct((bsz, t, D_MODEL), BF16),
        scratch_shapes=[pltpu.VMEM((heads, HGRN_DIM, HGRN_DIM), F32), pltpu.VMEM((t, width), BF16),
                        pltpu.VMEM((t, width), BF16), pltpu.VMEM((t, width), BF16),
                        pltpu.VMEM((max(t // HGRN_SUPER, 1), min(HGRN_SUPER, t) // HGRN_CHUNK, width), F32)],
        compiler_params=_cparams(("parallel", "parallel")),
        name="hgrn2",
    )(q, f, i, g, lower, norm_g.reshape(1, -1))


def _hgrn_mixer(x2, bsz, t, layer_idx, w_in, lower, norm_g, w_out, ln_g, ln_b):
    d = D_MODEL
    ws = [w_in[:, j * d:(j + 1) * d].astype(BF16) for j in range(4)]
    q, f, i, g = _proj(x2, ws, [BF16, F32, BF16, BF16])
    shp = (bsz, t, d)
    o = _hgrn_core(q.reshape(shp), f.reshape(shp), i.reshape(shp), g.reshape(shp), lower, norm_g, bsz, t, layer_idx)
    return _outproj_ln(o.reshape(bsz * t, d), w_out.astype(BF16), x2, ln_g, ln_b)


def _router_kernel(x_ref, rwt_ref, rb_ref, spos_ref, gate_ref, tinfo_ref, seg_ref, carry_s):
    @pl.when(pl.program_id(0) == 0)
    def _():
        carry_s[...] = jnp.zeros(carry_s.shape, F32)

    tm = x_ref.shape[0]
    ne = N_EXPERTS
    logits = _dot_nt(rwt_ref[...], x_ref[...], precision=HIGHEST) + rb_ref[...]
    e_iota = lax.broadcasted_iota(jnp.int32, (ne, tm), 0)
    work = logits
    vals, picks = [], []
    for k in range(TOP_K):
        mx = jnp.max(work, axis=0, keepdims=True)
        idx = jnp.min(jnp.where(work == mx, e_iota, ne), axis=0, keepdims=True)
        pick = e_iota == idx
        vals.append(mx)
        picks.append(pick)
        work = jnp.where(pick, -jnp.inf, work)
    exps = [jnp.exp(v - vals[0]) for v in vals]
    tot = exps[0]
    for k in range(1, TOP_K):
        tot = tot + exps[k]
    for k in range(TOP_K):
        gate_ref[k:k + 1, :] = exps[k] / tot
    hot = picks[0].astype(F32)
    for k in range(1, TOP_K):
        hot = hot + picks[k].astype(F32)
    before = (lax.broadcasted_iota(jnp.int32, (tm, tm), 0) < lax.broadcasted_iota(jnp.int32, (tm, tm), 1)).astype(BF16)
    prior = _dot(hot.astype(BF16), before)
    cnt = jnp.sum(hot, axis=1, keepdims=True)
    run = jnp.floor((cnt + (ROW_ALIGN - 1)) * (1.0 / ROW_ALIGN)) * ROW_ALIGN
    sub = lax.broadcasted_iota(jnp.int32, (ne, ne), 0)
    lan = lax.broadcasted_iota(jnp.int32, (ne, ne), 1)
    run_row = jnp.sum(jnp.where(sub == lan, run, 0.0), axis=0, keepdims=True)
    soff = jnp.sum(jnp.where(lan < sub, run_row, 0.0), axis=1, keepdims=True)
    for k in range(TOP_K):
        pos = jnp.sum(jnp.where(picks[k], prior + soff, 0.0), axis=0, keepdims=True)
        spos_ref[k:k + 1, :] = pos.astype(jnp.int32)
    sub = lax.broadcasted_iota(jnp.int32, (ne, LANE), 0)
    lan = lax.broadcasted_iota(jnp.int32, (ne, LANE), 1)
    field, le = lan // ne, lan % ne
    carry = carry_s[...]
    info = (jnp.where((field == 0) & (sub == le), carry, 0.0) + jnp.where((field == 1) & (sub == le), run, 0.0)
            + jnp.where((field == 2) & (sub < le), run, 0.0)
            + jnp.where((field == 3) & (le == 0), run * (1.0 / ROW_ALIGN), 0.0))
    tinfo_ref[0] = jnp.sum(info, axis=0, keepdims=True).astype(jnp.int32)
    total = carry + run
    carry_s[...] = total
    seg_ref[...] = total.astype(jnp.int32)


def _router(x2, router_w, router_b):
    n, d = x2.shape
    tm = min(MOE_TILE, n)
    n_tiles = n // tm
    row = pl.BlockSpec((TOP_K, tm), lambda i: (0, i))
    return pl.pallas_call(
        _router_kernel,
        grid=(n_tiles,),
        in_specs=[
            pl.BlockSpec((tm, d), lambda i: (i, 0)),
            pl.BlockSpec((N_EXPERTS, d), lambda i: (0, 0)),
            pl.BlockSpec((N_EXPERTS, 1), lambda i: (0, 0)),
        ],
        out_specs=[row, row, pl.BlockSpec((1, 1, LANE), lambda i: (i, 0, 0)),
                   pl.BlockSpec((N_EXPERTS, 1), lambda i: (0, 0))],
        out_shape=[
            jax.ShapeDtypeStruct((TOP_K, n), jnp.int32),
            jax.ShapeDtypeStruct((TOP_K, n), F32),
            jax.ShapeDtypeStruct((n_tiles, 1, LANE), jnp.int32),
            jax.ShapeDtypeStruct((N_EXPERTS, 1), jnp.int32),
        ],
        scratch_shapes=[pltpu.VMEM((N_EXPERTS, 1), F32)],
        compiler_params=_cparams(("arbitrary",)),
        name="router",
    )(x2, router_w.T, router_b.reshape(N_EXPERTS, 1))


def _group_copy(src, s, dst, d, sem):
    return pltpu.make_async_copy(src.at[pl.ds(pl.multiple_of(s, ROW_ALIGN), ROW_ALIGN), :],
                                 dst.at[pl.ds(pl.multiple_of(d, ROW_ALIGN), ROW_ALIGN), :], sem)


def _for_each_group(tinfo_ref, base_ref, fn):
    def per_expert(e, _):
        seg_row = base_ref[e] + tinfo_ref[0, 0, e]
        groups = tinfo_ref[0, 0, N_EXPERTS + e] // ROW_ALIGN
        sorted_row = tinfo_ref[0, 0, 2 * N_EXPERTS + e]

        def per_group(j, _):
            fn(sorted_row + j * ROW_ALIGN, seg_row + j * ROW_ALIGN)
            return 0
        return lax.fori_loop(0, groups, per_group, 0)
    lax.fori_loop(0, N_EXPERTS, per_expert, 0)


def _wait_groups(count, src, dst, sem):
    def one(j, _):
        _group_copy(src, 0, dst, 0, sem).wait()
        return 0
    lax.fori_loop(0, count, one, 0)


def _dispatch_kernel(base_ref, seg_ref, tinfo_ref, spos_ref, gate_ref, x_ref, xs_ref, sort2_s, zero_s, pending_s, sems):
    tb, d = x_ref.shape
    s_rows = sort2_s.shape[1]
    step = pl.program_id(0)
    sort_s, sem = sort2_s.at[step % 2], sems.at[step % 2]
    prev_sem = sems.at[(step + 1) % 2]

    @pl.when(step == 0)
    def _():
        zero_s[...] = jnp.zeros(zero_s.shape, F32)
        for fill in (True, False):
            def per_expert(e, _):
                def per_group(r, _):
                    cp = _group_copy(zero_s, 0, xs_ref, r * ROW_ALIGN, sem)
                    cp.start() if fill else cp.wait()
                    return 0
                first = jnp.where(e < N_EXPERTS, base_ref[jnp.minimum(e, N_EXPERTS - 1)] + seg_ref[jnp.minimum(e, N_EXPERTS - 1)],
                                  base_ref[N_EXPERTS])
                last = jnp.where(e < N_EXPERTS, base_ref[jnp.minimum(e + 1, N_EXPERTS)], xs_ref.shape[0])
                return lax.fori_loop(first // ROW_ALIGN, last // ROW_ALIGN, per_group, 0)
            lax.fori_loop(0, N_EXPERTS + 1, per_expert, 0)

    xb = x_ref[...].astype(BF16)
    spos = spos_ref[...]
    gate = gate_ref[...]
    for r0 in range(0, s_rows, SORT_ROWS):
        r_iota = r0 + lax.broadcasted_iota(jnp.int32, (SORT_ROWS, tb), 0)
        hits = [spos[k:k + 1, :] == r_iota for k in range(TOP_K)]
        onehot = jnp.where(hits[0] | hits[1] | hits[2] | hits[3], 1.0, 0.0).astype(BF16)
        sort_s[r0:r0 + SORT_ROWS, :d] = _dot(onehot, xb)
        g_sel = jnp.where(hits[0], gate[0:1, :], 0.0)
        for k in range(1, TOP_K):
            g_sel = g_sel + jnp.where(hits[k], gate[k:k + 1, :], 0.0)
        sort_s[r0:r0 + SORT_ROWS, d:] = jnp.broadcast_to(jnp.sum(g_sel, axis=1, keepdims=True), (SORT_ROWS, LANE))

    _for_each_group(tinfo_ref, base_ref, lambda s, r: _group_copy(sort_s, s, xs_ref, r, sem).start())
    groups = tinfo_ref[0, 0, 3 * N_EXPERTS]

    @pl.when(step > 0)
    def _():
        _wait_groups(pending_s[0], sort_s, xs_ref, prev_sem)

    pending_s[0] = groups

    @pl.when(step == pl.num_programs(0) - 1)
    def _():
        _wait_groups(groups, sort_s, xs_ref, sem)


def _tile_info_spec(n_tiles, ahead=0):
    return pl.BlockSpec((1, 1, LANE), lambda i, *_: (jnp.minimum(i + ahead, n_tiles - 1), 0, 0),
                        memory_space=pltpu.SMEM)


def _dispatch(x2, spos, gate, tinfo, base, seg, rows):
    n, d = x2.shape
    tb = min(MOE_TILE, n)
    row = pl.BlockSpec((TOP_K, tb), lambda i, *_: (0, i))
    return pl.pallas_call(
        _dispatch_kernel,
        grid_spec=pltpu.PrefetchScalarGridSpec(
            num_scalar_prefetch=2,
            grid=(n // tb,),
            in_specs=[_tile_info_spec(n // tb), row, row, pl.BlockSpec((tb, d), lambda i, *_: (i, 0))],
            out_specs=pl.BlockSpec(memory_space=pl.ANY),
            scratch_shapes=[pltpu.VMEM((2, _sorted_rows(tb), d + LANE), F32), pltpu.VMEM((ROW_ALIGN, d + LANE), F32),
                            pltpu.SMEM((1,), jnp.int32), pltpu.SemaphoreType.DMA((2,))],
        ),
        out_shape=jax.ShapeDtypeStruct((rows, d + LANE), F32),
        compiler_params=_cparams(("arbitrary",)),
        name="moe_dispatch",
    )(base, seg, tinfo, spos, gate, x2)


def _expert_kernel(be_ref, nused_ref, x_ref, wgu_ref, bgu_ref, wd_ref, bd_ref, y_ref, wgu_s, wd_s):
    i = pl.program_id(0)
    used = i < nused_ref[0]

    @pl.when(used & ((i == 0) | (be_ref[i] != be_ref[jnp.maximum(i - 1, 0)])))
    def _():
        wgu_s[...] = wgu_ref[0, 0].astype(BF16)
        wd_s[...] = wd_ref[0, 0].astype(BF16)

    @pl.when(used)
    def _():
        d = y_ref.shape[1]
        xb = x_ref[:, :d].astype(BF16)
        h = _dot(xb, wgu_s[...]) + bgu_ref[0, 0]
        h_gate = jnp.minimum(h[:, :D_FF], SWIGLU_LIMIT)
        h_up = jnp.clip(h[:, D_FF:], -SWIGLU_LIMIT, SWIGLU_LIMIT)
        act = (h_up + 1.0) * h_gate * jax.nn.sigmoid(SWIGLU_ALPHA * h_gate)
        y_ref[...] = (_dot(act.astype(BF16), wd_s[...]) + bd_ref[0, 0]) * x_ref[:, d:d + 1]

    @pl.when(jnp.logical_not(used))
    def _():
        y_ref[...] = jnp.zeros(y_ref.shape, F32)


def _experts(xs, block_expert, n_used, layer, w_gu, b_gu, w_down, b_down):
    rows = xs.shape[0]
    d = D_MODEL
    n_blocks = rows // MOE_BLOCK
    f2 = w_gu.shape[3]
    depth = w_gu.shape[0]
    xmap = lambda i, be, nu: (jnp.minimum(i, nu[0] - 1), 0)
    return pl.pallas_call(
        _expert_kernel,
        grid_spec=pltpu.PrefetchScalarGridSpec(
            num_scalar_prefetch=2,
            grid=(n_blocks,),
            in_specs=[
                pl.BlockSpec((MOE_BLOCK, d + LANE), xmap),
                pl.BlockSpec((1, 1, d, f2), lambda i, be, nu: (layer, be[i], 0, 0)),
                pl.BlockSpec((1, 1, 1, f2), lambda i, be, nu: (layer, be[i], 0, 0)),
                pl.BlockSpec((1, 1, D_FF, d), lambda i, be, nu: (layer, be[i], 0, 0)),
                pl.BlockSpec((1, 1, 1, d), lambda i, be, nu: (layer, be[i], 0, 0)),
            ],
            out_specs=pl.BlockSpec((MOE_BLOCK, d), lambda i, be, nu: (i, 0)),
            scratch_shapes=[pltpu.VMEM((d, f2), BF16), pltpu.VMEM((D_FF, d), BF16)],
        ),
        out_shape=jax.ShapeDtypeStruct((rows, d), F32),
        compiler_params=_cparams(("arbitrary",)),
        name="moe_experts",
    )(block_expert, n_used, xs, w_gu, b_gu.reshape(depth, N_EXPERTS, 1, f2), w_down,
      b_down.reshape(depth, N_EXPERTS, 1, d))


def _combine_kernel(base_ref, tinfo_ref, tnext_ref, spos_ref, x_ref, g_ref, b_ref, y_ref, o_ref, ybuf2_s, sems):
    tb = x_ref.shape[0]
    s_rows = ybuf2_s.shape[1]
    step = pl.program_id(0)
    ybuf_s, sem = ybuf2_s.at[step % 2], sems.at[step % 2]
    ynext_s, next_sem = ybuf2_s.at[(step + 1) % 2], sems.at[(step + 1) % 2]

    @pl.when(step == 0)
    def _():
        ybuf2_s[...] = jnp.zeros(ybuf2_s.shape, F32)
        _for_each_group(tinfo_ref, base_ref, lambda s, r: _group_copy(y_ref, r, ybuf_s, s, sem).start())

    @pl.when(step + 1 < pl.num_programs(0))
    def _():
        _for_each_group(tnext_ref, base_ref, lambda s, r: _group_copy(y_ref, r, ynext_s, s, next_sem).start())

    spos = spos_ref[...]
    n_split = COMBINE_SPLIT if s_rows % (COMBINE_SPLIT * LANE) == 0 else 1
    width = s_rows // n_split

    def onehot(c0):
        c_iota = c0 + lax.broadcasted_iota(jnp.int32, (tb, width), 1)
        hit = spos[:, 0:1] == c_iota
        for k in range(1, TOP_K):
            hit = hit | (spos[:, k:k + 1] == c_iota)
        return jnp.where(hit, 1.0, 0.0).astype(BF16)

    p_next = onehot(0)
    _wait_groups(tinfo_ref[0, 0, 3 * N_EXPERTS], y_ref, ybuf_s, sem)
    moe = None
    for c in range(n_split):
        part = _dot(p_next, ybuf_s[c * width:(c + 1) * width, :].astype(BF16))
        if c + 1 < n_split:
            p_next = onehot((c + 1) * width)
        moe = part if moe is None else moe + part
    o_ref[...] = _layer_norm(DEEPNORM_ALPHA * x_ref[...] + moe, g_ref[...], b_ref[...])


def _combine(y, x2, spos_col, tinfo, base, ln_g, ln_b):
    n, d = x2.shape
    tb = min(MOE_TILE, n)
    s_rows = _sorted_rows(tb)
    return pl.pallas_call(
        _combine_kernel,
        grid_spec=pltpu.PrefetchScalarGridSpec(
            num_scalar_prefetch=1,
            grid=(n // tb,),
            in_specs=[
                _tile_info_spec(n // tb), _tile_info_spec(n // tb, ahead=1),
                pl.BlockSpec((tb, TOP_K), lambda i, *_: (i, 0)),
                pl.BlockSpec((tb, d), lambda i, *_: (i, 0)),
                pl.BlockSpec((1, d), lambda i, *_: (0, 0)),
                pl.BlockSpec((1, d), lambda i, *_: (0, 0)),
                pl.BlockSpec(memory_space=pl.ANY),
            ],
            out_specs=pl.BlockSpec((tb, d), lambda i, *_: (i, 0)),
            scratch_shapes=[pltpu.VMEM((2, s_rows, d), F32), pltpu.SemaphoreType.DMA((2,))],
        ),
        out_shape=jax.ShapeDtypeStruct((n, d), F32),
        compiler_params=_cparams(("arbitrary",)),
        name="moe_combine",
    )(base, tinfo, tinfo, spos_col, x2, ln_g.reshape(1, d), ln_b.reshape(1, d), y)


def _sorted_rows(tb):
    return TOP_K * tb + N_EXPERTS * ROW_ALIGN


def _moe_ffn_ln(x2, layer, router_w, router_b, w_gu, b_gu, w_down, b_down, ln_g, ln_b):
    n, _ = x2.shape
    n_tiles = n // min(MOE_TILE, n)
    spos, gate, tinfo, seg = _router(x2, router_w, router_b)
    seg = seg.reshape(N_EXPERTS)
    max_rows = n * TOP_K + n_tiles * N_EXPERTS * (ROW_ALIGN - 1) + N_EXPERTS * (MOE_BLOCK - 1)
    n_blocks = (max_rows + MOE_BLOCK - 1) // MOE_BLOCK
    padded = (seg + MOE_BLOCK - 1) // MOE_BLOCK * MOE_BLOCK
    padded_end = jnp.cumsum(padded)
    base = jnp.concatenate([jnp.zeros((1,), jnp.int32), padded_end]).astype(jnp.int32)
    n_used = (padded_end[-1:] // MOE_BLOCK).astype(jnp.int32)
    block_start = jnp.arange(n_blocks, dtype=jnp.int32) * MOE_BLOCK
    block_expert = jnp.sum(block_start[:, None] >= padded_end[None, :], axis=1).astype(jnp.int32)
    last_expert = jnp.sum(padded_end[-1] - 1 >= padded_end).astype(jnp.int32)
    block_expert = jnp.minimum(block_expert, last_expert)
    xs = _dispatch(x2, spos, gate, tinfo, base, seg, n_blocks * MOE_BLOCK)
    y = _experts(xs, block_expert, n_used, layer, w_gu, b_gu, w_down, b_down)
    return _combine(y, x2, spos.T, tinfo, base, ln_g, ln_b)


def kernel(x, ln_g, ln_b, nsa_w_in, nsa_cmp_pos, nsa_cmp_w1, nsa_cmp_b1, nsa_cmp_w2, nsa_cmp_b2, nsa_gate_b, nsa_w_out, ml_w_in, ml_conv_w, ml_conv_b, ml_gate_b, ml_norm_g, ml_w_out, hg_w_in, hg_lower, hg_norm_g, hg_w_out, router_w, router_b, moe_w_gu, moe_b_gu, moe_w_down, moe_b_down):
    bsz, t, d = x.shape
    x2 = x.reshape(bsz * t, d)
    for layer in range(DEPTH):
        kind, slot = layer % N_MIXERS, layer // N_MIXERS
        if kind == 0:
            x2 = _nsa_mixer(x2, bsz, t, nsa_w_in[slot], nsa_cmp_pos[slot], nsa_cmp_w1[slot], nsa_cmp_b1[slot],
                            nsa_cmp_w2[slot], nsa_cmp_b2[slot], nsa_gate_b[slot], nsa_w_out[slot],
                            ln_g[layer, 0], ln_b[layer, 0])
        elif kind == 1:
            x2 = _mlstm_mixer(x2, bsz, t, ml_w_in[slot], ml_conv_w[slot], ml_conv_b[slot], ml_gate_b[slot],
                              ml_norm_g[slot], ml_w_out[slot], ln_g[layer, 0], ln_b[layer, 0])
        else:
            x2 = _hgrn_mixer(x2, bsz, t, layer, hg_w_in[slot], hg_lower, hg_norm_g[slot], hg_w_out[slot],
                             ln_g[layer, 0], ln_b[layer, 0])
        x2 = _moe_ffn_ln(x2, layer, router_w[layer], router_b[layer], moe_w_gu, moe_b_gu, moe_w_down, moe_b_down,
                         ln_g[layer, 1], ln_b[layer, 1])
    return x2.reshape(bsz, t, d)
```

```python
import functools

import numpy as np
import jax
import jax.numpy as jnp
from jax import lax
from jax.experimental import pallas as pl
from jax.experimental.pallas import tpu as pltpu

F32 = jnp.float32
BF16 = jnp.bfloat16
HIGHEST = lax.Precision.HIGHEST

D_MODEL = 1024
DEPTH = 4
N_MIXERS = 3

NSA_HEADS = 16
NSA_KV_GROUPS = 4
NSA_HEAD_DIM = 64
NSA_HPG = 4
NSA_KV_WIDTH = 256
CMP_BLOCK = 32
CMP_STRIDE = 16
CMP_HIDDEN = 256
SEL_BLOCK = 64
SEL_TOPK = 8
WINDOW = 512

MLSTM_HEADS = 4
MLSTM_QK_DIM = 128
MLSTM_V_DIM = 256
MLSTM_CONV = 4

HGRN_HEADS = 8
HGRN_DIM = 128
HGRN_CHUNK = 32

N_EXPERTS = 32
TOP_K = 4
D_FF = 1024
SWIGLU_LIMIT = 7.0
SWIGLU_ALPHA = 1.702

LN_EPS = 1e-5
RMS_EPS = 1e-6
DEEPNORM_ALPHA = (2 * DEPTH) ** 0.25
NEG = -1e30
M_FLOOR = -1e20

LANE = 128
VMEM_LIMIT = 56 * 1024 * 1024

PROJ_TM = 512
MOE_TILE = 512
MOE_BLOCK = 512
ROW_ALIGN = 8
BIG_GROUP = 32
SORT_ROWS = 256
COMBINE_SPLIT = 3
NSA_TQ = 256
NSA_TK = 512
MLSTM_L = 512
HGRN_SUPER = 128


def _cparams(sem):
    return pltpu.CompilerParams(dimension_semantics=sem, vmem_limit_bytes=VMEM_LIMIT)


def _dot(a, b):
    return jnp.dot(a, b, preferred_element_type=F32)


def _dot_nt(a, b, precision=None):
    return lax.dot_general(a, b, (((1,), (1,)), ((), ())), precision=precision, preferred_element_type=F32)


def _dot_tn(a, b):
    return lax.dot_general(a, b, (((0,), (0,)), ((), ())), preferred_element_type=F32)


def _layer_norm(z, g, b):
    mu = jnp.mean(z, axis=-1, keepdims=True)
    zc = z - mu
    var = jnp.mean(zc * zc, axis=-1, keepdims=True)
    return zc * lax.rsqrt(var + LN_EPS) * g + b


def _proj_kernel(x_ref, *refs, n_w, n_t):
    w_refs = refs[:n_w]
    wt_refs = refs[n_w:n_w + n_t]
    o_refs = refs[n_w + n_t:2 * n_w + n_t]
    ot_refs = refs[2 * n_w + n_t:]
    x = x_ref[...]
    xb = x.astype(BF16)
    for w_ref, o_ref in zip(w_refs, o_refs):
        ncol = w_ref.shape[1]
        for c0 in range(0, ncol, 512):
            c1 = min(c0 + 512, ncol)
            o_ref[:, c0:c1] = _dot(xb, w_ref[:, c0:c1]).astype(o_ref.dtype)
    for wt_ref, o_ref in zip(wt_refs, ot_refs):
        o_ref[...] = _dot_nt(wt_ref[...], x, precision=HIGHEST)


def _proj(x2, weights, out_dtypes, weights_t=()):
    n, d = x2.shape
    tm = min(PROJ_TM, n)
    in_specs = [pl.BlockSpec((tm, d), lambda i: (i, 0))]
    in_specs += [pl.BlockSpec(w.shape, lambda i: (0, 0)) for w in weights]
    in_specs += [pl.BlockSpec(w.shape, lambda i: (0, 0)) for w in weights_t]
    out_shape = [jax.ShapeDtypeStruct((n, w.shape[1]), dt) for w, dt in zip(weights, out_dtypes)]
    out_shape += [jax.ShapeDtypeStruct((w.shape[0], n), F32) for w in weights_t]
    out_specs = [pl.BlockSpec((tm, w.shape[1]), lambda i: (i, 0)) for w in weights]
    out_specs += [pl.BlockSpec((w.shape[0], tm), lambda i: (0, i)) for w in weights_t]
    return pl.pallas_call(
        functools.partial(_proj_kernel, n_w=len(weights), n_t=len(weights_t)),
        grid=(n // tm,),
        in_specs=in_specs,
        out_specs=out_specs,
        out_shape=out_shape,
        compiler_params=_cparams(("parallel",)),
        name="proj",
    )(x2, *weights, *weights_t)


def _outproj_ln_kernel(h_ref, w_ref, x_ref, g_ref, b_ref, o_ref):
    y = _dot(h_ref[...], w_ref[...])
    o_ref[...] = _layer_norm(DEEPNORM_ALPHA * x_ref[...] + y, g_ref[...], b_ref[...])


def _outproj_ln(h2, w, x2, g, b):
    n, d = x2.shape
    tm = min(PROJ_TM, n)
    return pl.pallas_call(
        _outproj_ln_kernel,
        grid=(n // tm,),
        in_specs=[
            pl.BlockSpec((tm, d), lambda i: (i, 0)),
            pl.BlockSpec((d, d), lambda i: (0, 0)),
            pl.BlockSpec((tm, d), lambda i: (i, 0)),
            pl.BlockSpec((1, d), lambda i: (0, 0)),
            pl.BlockSpec((1, d), lambda i: (0, 0)),
        ],
        out_specs=pl.BlockSpec((tm, d), lambda i: (i, 0)),
        out_shape=jax.ShapeDtypeStruct((n, d), F32),
        compiler_params=_cparams(("parallel",)),
        name="outproj_ln",
    )(h2, w, x2, g.reshape(1, d), b.reshape(1, d))


def _gelu_tanh(x):
    return 0.5 * x * (1.0 + jnp.tanh(np.sqrt(2.0 / np.pi).astype(np.float32) * (x + 0.044715 * (x * x * x))))


def _nsa_compress_kernel(kv_ref, wcat_ref, posa_ref, posb_ref, b1_ref, w2_ref, b2_ref, o_ref):
    n_row = kv_ref.shape[1] // CMP_STRIDE
    half = 2 * CMP_HIDDEN
    acc_a = jnp.zeros((n_row, half), F32)
    acc_b = jnp.zeros((n_row, half), F32)
    for r in range(CMP_STRIDE):
        xr = kv_ref[0, pl.ds(r, n_row, stride=CMP_STRIDE), :]
        w = wcat_ref[r]
        acc_a = acc_a + _dot((xr + posa_ref[r]).astype(BF16), w[:, :half])
        acc_b = acc_b + _dot((xr + posb_ref[r]).astype(BF16), w[:, half:])
    hidden = acc_a + pltpu.roll(acc_b, n_row - 1, 0) + b1_ref[...]
    hidden = _gelu_tanh(hidden)
    o_ref[0, 0] = (_dot(hidden.astype(BF16), w2_ref[...]) + b2_ref[...]).astype(o_ref.dtype)


def _nsa_compress(kvc, wcat, posa, posb, b1cat, w2cat, b2cat):
    bsz, t, _ = kvc.shape
    n_row = t // CMP_STRIDE
    return pl.pallas_call(
        _nsa_compress_kernel,
        grid=(bsz, NSA_KV_GROUPS),
        in_specs=[
            pl.BlockSpec((1, t, LANE), lambda b, g: (b, 0, g)),
            pl.BlockSpec(wcat.shape, lambda b, g: (0, 0, 0)),
            pl.BlockSpec(posa.shape, lambda b, g: (0, 0, 0)),
            pl.BlockSpec(posb.shape, lambda b, g: (0, 0, 0)),
            pl.BlockSpec(b1cat.shape, lambda b, g: (0, 0)),
            pl.BlockSpec(w2cat.shape, lambda b, g: (0, 0)),
            pl.BlockSpec(b2cat.shape, lambda b, g: (0, 0)),
        ],
        out_specs=pl.BlockSpec((1, 1, n_row, LANE), lambda b, g: (b, g, 0, 0)),
        out_shape=jax.ShapeDtypeStruct((bsz, NSA_KV_GROUPS, n_row, LANE), BF16),
        compiler_params=_cparams(("parallel", "parallel")),
        name="nsa_compress",
    )(kvc, wcat, posa, posb, b1cat, w2cat, b2cat)


def _nsa_attn_kernel(q_ref, kvc_ref, kvs_ref, kvw_ref, gate_ref, gb_ref, c2s_ref, o_ref, *, tq, tk, n_cmp):
    def tile(qi, carry):
        _nsa_attn_tile(qi, q_ref, kvc_ref, kvs_ref, kvw_ref, gate_ref, gb_ref, c2s_ref, o_ref, tq=tq, tk=tk, n_cmp=n_cmp)
        return carry
    lax.fori_loop(0, q_ref.shape[1] // tq, tile, 0)


def _nsa_attn_tile(qi, q_ref, kvc_ref, kvs_ref, kvw_ref, gate_ref, gb_ref, c2s_ref, o_ref, *, tq, tk, n_cmp):
    hpg = NSA_HPG
    t0 = pl.multiple_of(qi * tq, tq)
    q4 = jnp.concatenate([q_ref[0, pl.ds(t0, tq), h * LANE:(h + 1) * LANE] for h in range(hpg)], axis=0)
    q4 = q4 * jnp.asarray(NSA_HEAD_DIM ** -0.5, BF16)
    t_col = t0 + lax.broadcasted_iota(jnp.int32, (tq, 1), 0)
    lane = lax.broadcasted_iota(jnp.int32, (1, LANE), 1)
    ones_lane = (lane == 0).astype(BF16)

    def with_ones(kv):
        return jnp.where(lane < NSA_HEAD_DIM, ones_lane, kv)

    gates = jax.nn.sigmoid(gate_ref[0, pl.ds(t0, tq), :].astype(F32) + gb_ref[0])
    gate_of = [[jnp.broadcast_to(gates[:, 3 * h + j:3 * h + j + 1], (tq, LANE)) for j in range(3)] for h in range(hpg)]

    span = WINDOW + tq
    w0 = pl.multiple_of(jnp.maximum(t0 - WINDOW, 0), tq)
    kvw = kvw_ref[0, pl.ds(w0, span), :]
    kvw1 = with_ones(kvw)
    kvc = kvc_ref[0, 0]
    n_c = kvc.shape[0]
    q_heads = [q4[h * tq:(h + 1) * tq] for h in range(hpg)]
    s_c = [_dot_nt(q_heads[h], kvc) for h in range(hpg)]
    s_w = [_dot_nt(q_heads[h], kvw).astype(BF16) for h in range(hpg)]

    c_lane = lax.broadcasted_iota(jnp.int32, (1, n_c), 1)
    valid_c = (c_lane * CMP_STRIDE + (CMP_BLOCK - 1) <= t_col) & (c_lane < n_cmp)
    p_heads = []
    for h in range(hpg):
        sm = jnp.where(valid_c, s_c[h], NEG)
        m = jnp.max(sm, axis=-1, keepdims=True)
        p = jnp.where(valid_c, jnp.exp(sm - m), 0.0)
        l = jnp.sum(p, axis=-1, keepdims=True)
        p_heads.append(p / jnp.maximum(l, 1e-30))
    p_sum = p_heads[0]
    for h in range(1, hpg):
        p_sum = p_sum + p_heads[h]
    imp = _dot_nt(c2s_ref[...], p_sum, precision=HIGHEST)
    o_c = [_dot(p_heads[h].astype(BF16), kvc) for h in range(hpg)]

    lag = t_col - (w0 + lax.broadcasted_iota(jnp.int32, (1, span), 1))
    bias_w = jnp.where((lag >= 0) & (lag < WINDOW), 0.0, NEG).astype(BF16)
    o_w = []
    for h in range(hpg):
        sm = s_w[h] + bias_w
        o_w.append(_dot(jnp.exp(sm - jnp.max(sm, axis=-1, keepdims=True)), kvw1))
    partial = [gate_of[h][0] * o_c[h] + gate_of[h][2] * (o_w[h] / o_w[h][:, 0:1]) for h in range(hpg)]

    n_sel = imp.shape[0]
    blk = lax.broadcasted_iota(jnp.int32, (n_sel, 1), 0)
    cur = (t0 + lax.broadcasted_iota(jnp.int32, (1, tq), 1)) // SEL_BLOCK
    forced = (blk == 0) | (blk == cur) | (blk == cur - 1)
    score = jnp.where(forced, -NEG, jnp.where(blk <= cur, imp, NEG))
    rank = jnp.zeros((n_sel, tq), F32)
    for i in range(n_sel):
        s_i = score[i:i + 1, :]
        ahead = (s_i > score) | ((s_i == score) & (blk > i))
        rank = rank + ahead.astype(F32)
    member = ((rank < SEL_TOPK) & (score > 0.5 * NEG)).astype(BF16)

    blk_per_tile = tk // SEL_BLOCK
    key_lane = lax.broadcasted_iota(jnp.int32, (1, tk), 1)

    def sel_body(kt, carry):
        k0 = pl.multiple_of(kt * tk, tk)
        kv = kvs_ref[0, pl.ds(k0, tk), :]
        kv1 = with_ones(kv)
        expand = (key_lane // SEL_BLOCK + kt * blk_per_tile == blk).astype(BF16)
        valid = (_dot_tn(member, expand) > 0.5) & (k0 + key_lane <= t_col)
        bias = jnp.where(valid, 0.0, NEG).astype(BF16)
        s = [_dot_nt(q_heads[h], kv).astype(BF16) for h in range(hpg)]
        new = []
        for h in range(hpg):
            m_old, acc_old = carry[h]
            sm = s[h] + bias
            m_new = jnp.maximum(m_old, jnp.max(sm, axis=-1, keepdims=True).astype(F32))
            pv = _dot(jnp.exp(sm - m_new.astype(BF16)), kv1)
            new.append((m_new, jnp.exp(m_old - m_new) * acc_old + pv))
        return tuple(new)

    init = tuple((jnp.full((tq, 1), M_FLOOR, F32), jnp.zeros((tq, LANE), F32)) for _ in range(hpg))
    n_kt = (t0 + tq - 1) // tk + 1
    sel = lax.fori_loop(0, n_kt, sel_body, init)

    outs = []
    for h in range(hpg):
        o_s = sel[h][1]
        outs.append(partial[h] + gate_of[h][1] * (o_s / o_s[:, 0:1]))
    for j in range(hpg // 2):
        pair = jnp.where(lane < NSA_HEAD_DIM, pltpu.roll(outs[2 * j], NSA_HEAD_DIM, 1), outs[2 * j + 1])
        o_ref[0, pl.ds(t0, tq), j * LANE:(j + 1) * LANE] = pair.astype(o_ref.dtype)


def _nsa_attention(main, kvcmp, gate_b, c2s, bsz, t):
    tq = min(NSA_TQ, t)
    tk = min(NSA_TK, t)
    n_cmp = (t - CMP_BLOCK) // CMP_STRIDE + 1
    q_blocks = NSA_HEADS
    g = NSA_KV_GROUPS
    kernel = functools.partial(_nsa_attn_kernel, tq=tq, tk=tk, n_cmp=n_cmp)
    return pl.pallas_call(
        kernel,
        grid=(bsz, g),
        in_specs=[
            pl.BlockSpec((1, t, NSA_HPG * LANE), lambda b, gi: (b, 0, gi)),
            pl.BlockSpec((1, 1) + kvcmp.shape[2:], lambda b, gi: (b, gi, 0, 0)),
            pl.BlockSpec((1, t, LANE), lambda b, gi: (b, 0, q_blocks + gi)),
            pl.BlockSpec((1, t, LANE), lambda b, gi: (b, 0, q_blocks + g + gi)),
            pl.BlockSpec((1, t, LANE), lambda b, gi: (b, 0, q_blocks + 2 * g + gi)),
            pl.BlockSpec((1, 1, LANE), lambda b, gi: (gi, 0, 0)),
            pl.BlockSpec(c2s.shape, lambda b, gi: (0, 0)),
        ],
        out_specs=pl.BlockSpec((1, t, NSA_HPG * NSA_HEAD_DIM), lambda b, gi: (b, 0, gi)),
        out_shape=jax.ShapeDtypeStruct((bsz, t, D_MODEL), BF16),
        compiler_params=_cparams(("parallel", "parallel")),
        name="nsa_attention",
    )(main, kvcmp, main, main, main, gate_b, c2s)


def _nsa_mixer(x2, bsz, t, w_in, cmp_pos, cmp_w1, cmp_b1, cmp_w2, cmp_b2, gate_b, w_out, ln_g, ln_b):
    d, g, dh, hpg = D_MODEL, NSA_KV_GROUPS, NSA_HEAD_DIM, NSA_HPG
    kvw_ = NSA_KV_WIDTH

    def pair(k0):
        k = w_in[:, k0:k0 + kvw_].reshape(d, g, 1, dh)
        v = w_in[:, k0 + kvw_:k0 + 2 * kvw_].reshape(d, g, 1, dh)
        return jnp.concatenate([k, v], axis=2).reshape(d, g * LANE)

    wq = jnp.pad(w_in[:, :d].reshape(d, NSA_HEADS, dh), ((0, 0), (0, 0), (0, LANE - dh))).reshape(d, NSA_HEADS * LANE)
    n_gate = 3 * hpg
    wg = jnp.pad(w_in[:, d + 6 * kvw_:].reshape(d, g, n_gate), ((0, 0), (0, 0), (0, LANE - n_gate))).reshape(d, g * LANE)
    w_main = jnp.concatenate([wq, pair(d + 2 * kvw_), pair(d + 4 * kvw_), wg], axis=1).astype(BF16)
    w_kvc = pair(d).astype(BF16)
    main, kvc = _proj(x2, [w_main, w_kvc], [BF16, F32])

    half = CMP_BLOCK // 2
    w1 = cmp_w1.reshape(2, CMP_BLOCK, dh, CMP_HIDDEN)
    z = jnp.zeros((half, dh, CMP_HIDDEN), F32)
    top = jnp.concatenate([w1[0, :half], z, w1[0, half:], z], axis=2)
    bot = jnp.concatenate([z, w1[1, :half], z, w1[1, half:]], axis=2)
    wcat = jnp.concatenate([top, bot], axis=1).astype(BF16)
    pos = jnp.concatenate([cmp_pos[0], cmp_pos[1]], axis=-1)
    posa, posb = pos[:half, None, :], pos[half:, None, :]
    b1cat = cmp_b1.reshape(1, 2 * CMP_HIDDEN)
    zz = jnp.zeros((CMP_HIDDEN, dh), F32)
    w2cat = jnp.concatenate([jnp.concatenate([cmp_w2[0], zz], axis=1),
                             jnp.concatenate([zz, cmp_w2[1]], axis=1)], axis=0).astype(BF16)
    b2cat = cmp_b2.reshape(1, 2 * dh)
    kvcmp = _nsa_compress(kvc.reshape(bsz, t, g * LANE), wcat, posa, posb, b1cat, w2cat, b2cat)

    n_row = t // CMP_STRIDE
    n_sel = t // SEL_BLOCK
    cmp_start = np.arange(n_row) * CMP_STRIDE
    sel_start = np.arange(n_sel) * SEL_BLOCK
    overlap = (np.minimum(cmp_start[:, None] + CMP_BLOCK, sel_start[None, :] + SEL_BLOCK)
               - np.maximum(cmp_start[:, None], sel_start[None, :]))
    c2s = jnp.asarray((np.clip(overlap, 0, None) / CMP_STRIDE).astype(np.float32).T)
    gb = jnp.pad(gate_b.reshape(g, 1, n_gate), ((0, 0), (0, 0), (0, LANE - n_gate)))
    o = _nsa_attention(main.reshape(bsz, t, -1), kvcmp, gb, c2s, bsz, t)
    return _outproj_ln(o.reshape(bsz * t, d), w_out.astype(BF16), x2, ln_g, ln_b)


def _mlstm_kernel(q_ref, k_ref, v_ref, og_ref, gif_ref, gb_ref, cwq_ref, cwk_ref, cbq_ref, cbk_ref, ng_ref,
                  o_ref, pad_s, q_s, k_s, ct_s, *, chunk):
    t = q_ref.shape[1]
    dk, dv = MLSTM_QK_DIM, MLSTM_V_DIM
    head = pl.program_id(1)
    halo = 8

    def conv_silu(x_ref, w_ref, b_ref, dst, scale):
        pad_s[0:halo, :] = jnp.zeros((halo, dk), F32)
        pad_s[halo:halo + t, :] = x_ref[0]
        rows = min(t, 256)
        for r0 in range(0, t, rows):
            y = b_ref[...] + w_ref[0:1, :] * pad_s[pl.ds(halo + r0 - (MLSTM_CONV - 1), rows), :]
            for j in range(1, MLSTM_CONV):
                y = y + w_ref[j:j + 1, :] * pad_s[pl.ds(halo + r0 - (MLSTM_CONV - 1) + j, rows), :]
            y = y * jax.nn.sigmoid(y)
            dst[r0:r0 + rows, :] = (y * scale).astype(dst.dtype)

    conv_silu(q_ref, cwq_ref, cbq_ref, q_s, 1.0)
    conv_silu(k_ref, cwk_ref, cbk_ref, k_s, dk ** -0.5)
    ct_s[...] = jnp.zeros((dk, dv), F32)

    row = lax.broadcasted_iota(jnp.int32, (chunk, chunk), 0)
    col = lax.broadcasted_iota(jnp.int32, (chunk, chunk), 1)
    causal = col <= row
    eye = col == row
    upper = (row <= col).astype(F32)
    b_i = gb_ref[pl.ds(head, 1), :]
    b_f = gb_ref[pl.ds(MLSTM_HEADS + head, 1), :]
    norm_g = ng_ref[...]

    def body(c, carry):
        n_row, m_prev = carry
        r0 = pl.multiple_of(c * chunk, chunk)
        qc = q_s[pl.ds(r0, chunk), :]
        kc = k_s[pl.ds(r0, chunk), :]
        vc = v_ref[0, pl.ds(r0, chunk), :]
        li_row = gif_ref[0, c, pl.ds(head, 1), :] + b_i
        zf = gif_ref[0, c, pl.ds(MLSTM_HEADS + head, 1), :] + b_f
        lf_row = jnp.minimum(zf, 0.0) - jnp.log1p(jnp.exp(-jnp.abs(zf)))
        b_row = jnp.dot(lf_row, upper, precision=HIGHEST, preferred_element_type=F32)
        b_col = jnp.sum(jnp.where(causal, lf_row, 0.0), axis=-1, keepdims=True)
        li_col = jnp.sum(jnp.where(eye, li_row, 0.0), axis=-1, keepdims=True)
        d_mat = jnp.where(causal, b_col - b_row + li_row, NEG)
        m_t = jnp.maximum(b_col + m_prev, jnp.max(d_mat, axis=-1, keepdims=True))
        w_inter = jnp.exp(b_col + m_prev - m_t)
        s = _dot_nt(qc, kc) * jnp.exp(d_mat - m_t)
        num = _dot(s.astype(BF16), vc) + w_inter * _dot(qc, ct_s[...].astype(BF16))
        den = jnp.sum(s, axis=-1, keepdims=True) + w_inter * jnp.sum(qc.astype(F32) * n_row, axis=-1, keepdims=True)
        h = num / jnp.maximum(jnp.abs(den), jnp.exp(-m_t))
        b_end = b_col[chunk - 1:chunk, :]
        decay = b_end - b_col + li_col
        m_new = jnp.maximum(b_end + m_prev, jnp.max(decay, axis=0, keepdims=True))
        w_k = jnp.exp(decay - m_new)
        scale = jnp.exp(b_end + m_prev - m_new)
        ct_s[...] = scale * ct_s[...] + _dot_tn(kc, (vc.astype(F32) * w_k).astype(BF16))
        n_new = scale * n_row + jnp.sum(w_k * kc.astype(F32), axis=0, keepdims=True)
        hn = h * lax.rsqrt(jnp.mean(h * h, axis=-1, keepdims=True) + RMS_EPS) * norm_g
        og = og_ref[0, pl.ds(r0, chunk), :].astype(F32)
        o_ref[0, pl.ds(r0, chunk), :] = (hn * jax.nn.sigmoid(og)).astype(o_ref.dtype)
        return n_new, m_new

    lax.fori_loop(0, t // chunk, body, (jnp.zeros((1, dk), F32), jnp.zeros((1, 1), F32)))


def _mlstm_core(qk, v, og, gif, gate_b, conv_w, conv_b, norm_g, bsz, t, chunk):
    hh, dk, dv = MLSTM_HEADS, MLSTM_QK_DIM, MLSTM_V_DIM
    nc = t // chunk
    kernel = functools.partial(_mlstm_kernel, chunk=chunk)
    return pl.pallas_call(
        kernel,
        grid=(bsz, hh),
        in_specs=[
            pl.BlockSpec((1, t, dk), lambda b, h: (b, 0, h)),
            pl.BlockSpec((1, t, dk), lambda b, h: (b, 0, hh + h)),
            pl.BlockSpec((1, t, dv), lambda b, h: (b, 0, h)),
            pl.BlockSpec((1, t, dv), lambda b, h: (b, 0, h)),
            pl.BlockSpec((1, nc, 2 * hh, chunk), lambda b, h: (b, 0, 0, 0)),
            pl.BlockSpec((2 * hh, 1), lambda b, h: (0, 0)),
            pl.BlockSpec((MLSTM_CONV, dk), lambda b, h: (0, h)),
            pl.BlockSpec((MLSTM_CONV, dk), lambda b, h: (0, hh + h)),
            pl.BlockSpec((1, dk), lambda b, h: (0, h)),
            pl.BlockSpec((1, dk), lambda b, h: (0, hh + h)),
            pl.BlockSpec((1, dv), lambda b, h: (0, h)),
        ],
        out_specs=pl.BlockSpec((1, t, dv), lambda b, h: (b, 0, h)),
        out_shape=jax.ShapeDtypeStruct((bsz, t, hh * dv), BF16),
        scratch_shapes=[
            pltpu.VMEM((t + 8, dk), F32),
            pltpu.VMEM((t, dk), BF16),
            pltpu.VMEM((t, dk), BF16),
            pltpu.VMEM((dk, dv), F32),
        ],
        compiler_params=_cparams(("parallel", "parallel")),
        name="mlstm",
    )(qk, qk, v, og, gif, gate_b.reshape(2 * hh, 1), conv_w, conv_w, conv_b.reshape(1, -1), conv_b.reshape(1, -1),
      norm_g.reshape(1, -1))


def _mlstm_mixer(x2, bsz, t, w_in, conv_w, conv_b, gate_b, norm_g, w_out, ln_g, ln_b):
    d, hh = D_MODEL, MLSTM_HEADS
    qkw = 2 * hh * MLSTM_QK_DIM
    vw = hh * MLSTM_V_DIM
    chunk = min(MLSTM_L, t)
    w_qk = w_in[:, :qkw].astype(BF16)
    w_v = w_in[:, qkw:qkw + vw].astype(BF16)
    w_og = w_in[:, qkw + vw:qkw + vw + d].astype(BF16)
    w_gif_t = w_in[:, qkw + vw + d:].T
    qk, v, og, gif_t = _proj(x2, [w_qk, w_v, w_og], [F32, BF16, BF16], [w_gif_t])
    gif = gif_t.reshape(2 * hh, bsz, t // chunk, chunk).transpose(1, 2, 0, 3)
    h = _mlstm_core(qk.reshape(bsz, t, qkw), v.reshape(bsz, t, vw), og.reshape(bsz, t, d), gif, gate_b,
                    conv_w, conv_b, norm_g, bsz, t, chunk)
    return _outproj_ln(h.reshape(bsz * t, d), w_out.astype(BF16), x2, ln_g, ln_b)


def _hgrn_kernel(q_ref, f_ref, i_ref, g_ref, low_ref, ng_ref, o_ref, st_s, qd_s, kd_s, ke_s, dec_s, *,
                 layer_idx, heads):
    t = q_ref.shape[1]
    dh, chunk = HGRN_DIM, HGRN_CHUNK
    low = low_ref[...]
    e = jnp.exp(low - jnp.max(low, axis=0, keepdims=True))
    soft = e / jnp.sum(e, axis=0, keepdims=True)
    lb = jnp.zeros((1, heads * dh), F32)
    for r in range(1, layer_idx + 1):
        lb = lb + soft[r:r + 1, :]
    norm_g = ng_ref[...]
    st_s[...] = jnp.zeros(st_s.shape, F32)
    width = heads * dh
    sup = min(HGRN_SUPER, t)
    n_sub = sup // chunk

    in_chunk = lax.broadcasted_iota(jnp.int32, (sup, 1), 0) % chunk

    def prep(s, carry):
        r0 = pl.multiple_of(s * sup, sup)
        f = lb + (1.0 - lb) * jax.nn.sigmoid(f_ref[0, pl.ds(r0, sup), :])
        gcum = jnp.log(f)
        shift = 1
        while shift < chunk:
            gcum = gcum + jnp.where(in_chunk >= shift, pltpu.roll(gcum, shift, 0), 0.0)
            shift *= 2
        g_end = jnp.concatenate(
            [jnp.broadcast_to(gcum[(j + 1) * chunk - 1:(j + 1) * chunk, :], (chunk, width)) for j in range(n_sub)], axis=0)
        qv = q_ref[0, pl.ds(r0, sup), :].astype(F32)
        kk = 1.0 - f
        qd_s[pl.ds(r0, sup), :] = (qv * jax.nn.sigmoid(qv) * jnp.exp(gcum)).astype(BF16)
        kd_s[pl.ds(r0, sup), :] = (kk * jnp.exp(-gcum)).astype(BF16)
        ke_s[pl.ds(r0, sup), :] = (kk * jnp.exp(g_end - gcum)).astype(BF16)
        for j in range(n_sub):
            dec_s[s, j:j + 1, :] = jnp.exp(gcum[(j + 1) * chunk - 1:(j + 1) * chunk, :])
        return carry

    lax.fori_loop(0, t // sup, prep, 0)

    row = lax.broadcasted_iota(jnp.int32, (sup, sup), 0)
    col = lax.broadcasted_iota(jnp.int32, (sup, sup), 1)
    keep = (col <= row) & (col // chunk == row // chunk)

    def body(s, carry):
        r0 = pl.multiple_of(s * sup, sup)
        decay = dec_s[s]
        col_of = [slice(h * dh, (h + 1) * dh) for h in range(heads)]
        sub_of = [slice(j * chunk, (j + 1) * chunk) for j in range(n_sub)]
        q_dec = [qd_s[pl.ds(r0, sup), c] for c in col_of]
        vv = [i_ref[0, pl.ds(r0, sup), c] for c in col_of]
        score = [_dot_nt(q_dec[h], kd_s[pl.ds(r0, sup), col_of[h]]) for h in range(heads)]
        k_end = [ke_s[pl.ds(r0, sup), c] for c in col_of]
        kv = [[_dot_tn(vv[h][r], k_end[h][r]) for r in sub_of] for h in range(heads)]
        intra = [_dot(jnp.where(keep, score[h], 0.0).astype(BF16), vv[h]) for h in range(heads)]
        states = []
        for h in range(heads):
            st = st_s[h]
            per_sub = []
            for j in range(n_sub):
                per_sub.append(st.astype(BF16))
                st = st * decay[j:j + 1, col_of[h]] + kv[h][j]
            st_s[h] = st
            states.append(per_sub)
        for h in range(heads):
            inter = [_dot_nt(q_dec[h][sub_of[j]], states[h][j]) for j in range(n_sub)]
            o = intra[h] + jnp.concatenate(inter, axis=0)
            on = o * lax.rsqrt(jnp.mean(o * o, axis=-1, keepdims=True) + RMS_EPS) * norm_g[:, col_of[h]]
            gate = jax.nn.sigmoid(g_ref[0, pl.ds(r0, sup), col_of[h]].astype(F32))
            o_ref[0, pl.ds(r0, sup), col_of[h]] = (on * gate).astype(o_ref.dtype)
        return carry

    lax.fori_loop(0, t // sup, body, 0)


def _hgrn_core(q, f, i, g, lower, norm_g, bsz, t, layer_idx):
    heads = 4
    width = heads * HGRN_DIM
    n_grp = HGRN_HEADS // heads
    kernel = functools.partial(_hgrn_kernel, layer_idx=layer_idx, heads=heads)
    act = pl.BlockSpec((1, t, width), lambda b, j: (b, 0, j))
    return pl.pallas_call(
        kernel,
        grid=(bsz, n_grp),
        in_specs=[act, act, act, act,
                  pl.BlockSpec((DEPTH, width), lambda b, j: (0, j)),
                  pl.BlockSpec((1, width), lambda b, j: (0, j))],
        out_specs=act,
        out_shape=jax.ShapeDtypeStruct((bsz, t, D_MODEL), BF16),
        scratch_shapes=[pltpu.VMEM((heads, HGRN_DIM, HGRN_DIM), F32), pltpu.VMEM((t, width), BF16),
                        pltpu.VMEM((t, width), BF16), pltpu.VMEM((t, width), BF16),
                        pltpu.VMEM((max(t // HGRN_SUPER, 1), min(HGRN_SUPER, t) // HGRN_CHUNK, width), F32)],
        compiler_params=_cparams(("parallel", "parallel")),
        name="hgrn2",
    )(q, f, i, g, lower, norm_g.reshape(1, -1))


def _hgrn_mixer(x2, bsz, t, layer_idx, w_in, lower, norm_g, w_out, ln_g, ln_b):
    d = D_MODEL
    ws = [w_in[:, j * d:(j + 1) * d].astype(BF16) for j in range(4)]
    q, f, i, g = _proj(x2, ws, [BF16, F32, BF16, BF16])
    shp = (bsz, t, d)
    o = _hgrn_core(q.reshape(shp), f.reshape(shp), i.reshape(shp), g.reshape(shp), lower, norm_g, bsz, t, layer_idx)
    return _outproj_ln(o.reshape(bsz * t, d), w_out.astype(BF16), x2, ln_g, ln_b)


def _router_kernel(x_ref, rwt_ref, rb_ref, spos_ref, gate_ref, tinfo_ref, seg_ref, carry_s):
    @pl.when(pl.program_id(0) == 0)
    def _():
        carry_s[...] = jnp.zeros(carry_s.shape, F32)

    tm = x_ref.shape[0]
    ne = N_EXPERTS
    logits = _dot_nt(rwt_ref[...], x_ref[...], precision=HIGHEST) + rb_ref[...]
    e_iota = lax.broadcasted_iota(jnp.int32, (ne, tm), 0)
    work = logits
    vals, picks = [], []
    for k in range(TOP_K):
        mx = jnp.max(work, axis=0, keepdims=True)
        idx = jnp.min(jnp.where(work == mx, e_iota, ne), axis=0, keepdims=True)
        pick = e_iota == idx
        vals.append(mx)
        picks.append(pick)
        work = jnp.where(pick, -jnp.inf, work)
    exps = [jnp.exp(v - vals[0]) for v in vals]
    tot = exps[0]
    for k in range(1, TOP_K):
        tot = tot + exps[k]
    for k in range(TOP_K):
        gate_ref[k:k + 1, :] = exps[k] / tot
    hot = picks[0].astype(F32)
    for k in range(1, TOP_K):
        hot = hot + picks[k].astype(F32)
    before = (lax.broadcasted_iota(jnp.int32, (tm, tm), 0) < lax.broadcasted_iota(jnp.int32, (tm, tm), 1)).astype(BF16)
    prior = _dot(hot.astype(BF16), before)
    cnt = jnp.sum(hot, axis=1, keepdims=True)
    run = jnp.floor((cnt + (ROW_ALIGN - 1)) * (1.0 / ROW_ALIGN)) * ROW_ALIGN
    sub = lax.broadcasted_iota(jnp.int32, (ne, ne), 0)
    lan = lax.broadcasted_iota(jnp.int32, (ne, ne), 1)
    run_row = jnp.sum(jnp.where(sub == lan, run, 0.0), axis=0, keepdims=True)
    soff = jnp.sum(jnp.where(lan < sub, run_row, 0.0), axis=1, keepdims=True)
    for k in range(TOP_K):
        pos = jnp.sum(jnp.where(picks[k], prior + soff, 0.0), axis=0, keepdims=True)
        spos_ref[k:k + 1, :] = pos.astype(jnp.int32)
    n_big = jnp.floor(run * (1.0 / BIG_GROUP))
    sub = lax.broadcasted_iota(jnp.int32, (ne, LANE), 0)
    lan = lax.broadcasted_iota(jnp.int32, (ne, LANE), 1)
    field, le = lan // ne, lan % ne
    carry = carry_s[...]
    info = (jnp.where((field == 0) & (sub == le), carry, 0.0) + jnp.where((field == 1) & (sub == le), run, 0.0)
            + jnp.where((field == 2) & (sub < le), run, 0.0)
            + jnp.where((field == 3) & (le == 0), (run - n_big * BIG_GROUP) * (1.0 / ROW_ALIGN), 0.0)
            + jnp.where((field == 3) & (le == 1), n_big, 0.0))
    tinfo_ref[0] = jnp.sum(info, axis=0, keepdims=True).astype(jnp.int32)
    total = carry + run
    carry_s[...] = total
    seg_ref[...] = total.astype(jnp.int32)


def _router(x2, router_w, router_b):
    n, d = x2.shape
    tm = min(MOE_TILE, n)
    n_tiles = n // tm
    row = pl.BlockSpec((TOP_K, tm), lambda i: (0, i))
    return pl.pallas_call(
        _router_kernel,
        grid=(n_tiles,),
        in_specs=[
            pl.BlockSpec((tm, d), lambda i: (i, 0)),
            pl.BlockSpec((N_EXPERTS, d), lambda i: (0, 0)),
            pl.BlockSpec((N_EXPERTS, 1), lambda i: (0, 0)),
        ],
        out_specs=[row, row, pl.BlockSpec((1, 1, LANE), lambda i: (i, 0, 0)),
                   pl.BlockSpec((N_EXPERTS, 1), lambda i: (0, 0))],
        out_shape=[
            jax.ShapeDtypeStruct((TOP_K, n), jnp.int32),
            jax.ShapeDtypeStruct((TOP_K, n), F32),
            jax.ShapeDtypeStruct((n_tiles, 1, LANE), jnp.int32),
            jax.ShapeDtypeStruct((N_EXPERTS, 1), jnp.int32),
        ],
        scratch_shapes=[pltpu.VMEM((N_EXPERTS, 1), F32)],
        compiler_params=_cparams(("arbitrary",)),
        name="router",
    )(x2, router_w.T, router_b.reshape(N_EXPERTS, 1))


def _group_copy(src, s, dst, d, sem, rows=ROW_ALIGN):
    return pltpu.make_async_copy(src.at[pl.ds(pl.multiple_of(s, ROW_ALIGN), rows), :],
                                 dst.at[pl.ds(pl.multiple_of(d, ROW_ALIGN), rows), :], sem)


def _for_each_group(tinfo_ref, base_ref, fn):
    def per_expert(e, _):
        seg_row = base_ref[e] + tinfo_ref[0, 0, e]
        run = tinfo_ref[0, 0, N_EXPERTS + e]
        sorted_row = tinfo_ref[0, 0, 2 * N_EXPERTS + e]
        n_big = run // BIG_GROUP

        def big(j, _):
            fn(sorted_row + j * BIG_GROUP, seg_row + j * BIG_GROUP, BIG_GROUP)
            return 0

        def small(j, _):
            fn(sorted_row + n_big * BIG_GROUP + j * ROW_ALIGN, seg_row + n_big * BIG_GROUP + j * ROW_ALIGN, ROW_ALIGN)
            return 0

        lax.fori_loop(0, n_big, big, 0)
        return lax.fori_loop(0, (run - n_big * BIG_GROUP) // ROW_ALIGN, small, 0)
    lax.fori_loop(0, N_EXPERTS, per_expert, 0)


def _wait_groups(n_small, n_big, src, dst, sem):
    for count, rows in ((n_big, BIG_GROUP), (n_small, ROW_ALIGN)):
        def one(j, _, rows=rows):
            _group_copy(src, 0, dst, 0, sem, rows).wait()
            return 0
        lax.fori_loop(0, count, one, 0)


def _dispatch_kernel(base_ref, seg_ref, tinfo_ref, spos_ref, gate_ref, x_ref, xs_ref, sort2_s, zero_s, pending_s, sems):
    tb, d = x_ref.shape
    s_rows = sort2_s.shape[1]
    step = pl.program_id(0)
    sort_s, sem = sort2_s.at[step % 2], sems.at[step % 2]
    prev_sem = sems.at[(step + 1) % 2]

    @pl.when(step == 0)
    def _():
        zero_s[...] = jnp.zeros(zero_s.shape, F32)
        for fill in (True, False):
            def per_expert(e, _):
                def per_group(r, _):
                    cp = _group_copy(zero_s, 0, xs_ref, r * ROW_ALIGN, sem)
                    cp.start() if fill else cp.wait()
                    return 0
                first = jnp.where(e < N_EXPERTS, base_ref[jnp.minimum(e, N_EXPERTS - 1)] + seg_ref[jnp.minimum(e, N_EXPERTS - 1)],
                                  base_ref[N_EXPERTS])
                last = jnp.where(e < N_EXPERTS, base_ref[jnp.minimum(e + 1, N_EXPERTS)], xs_ref.shape[0])
                return lax.fori_loop(first // ROW_ALIGN, last // ROW_ALIGN, per_group, 0)
            lax.fori_loop(0, N_EXPERTS + 1, per_expert, 0)

    xb = x_ref[...].astype(BF16)
    spos = spos_ref[...]
    gate = gate_ref[...]
    for r0 in range(0, s_rows, SORT_ROWS):
        r_iota = r0 + lax.broadcasted_iota(jnp.int32, (SORT_ROWS, tb), 0)
        hits = [spos[k:k + 1, :] == r_iota for k in range(TOP_K)]
        onehot = jnp.where(hits[0] | hits[1] | hits[2] | hits[3], 1.0, 0.0).astype(BF16)
        sort_s[r0:r0 + SORT_ROWS, :d] = _dot(onehot, xb)
        g_sel = jnp.where(hits[0], gate[0:1, :], 0.0)
        for k in range(1, TOP_K):
            g_sel = g_sel + jnp.where(hits[k], gate[k:k + 1, :], 0.0)
        sort_s[r0:r0 + SORT_ROWS, d:] = jnp.broadcast_to(jnp.sum(g_sel, axis=1, keepdims=True), (SORT_ROWS, LANE))

    _for_each_group(tinfo_ref, base_ref, lambda s, r, rows: _group_copy(sort_s, s, xs_ref, r, sem, rows).start())
    n_small, n_big = tinfo_ref[0, 0, 3 * N_EXPERTS], tinfo_ref[0, 0, 3 * N_EXPERTS + 1]

    @pl.when(step > 0)
    def _():
        _wait_groups(pending_s[0], pending_s[1], sort_s, xs_ref, prev_sem)

    pending_s[0] = n_small
    pending_s[1] = n_big

    @pl.when(step == pl.num_programs(0) - 1)
    def _():
        _wait_groups(n_small, n_big, sort_s, xs_ref, sem)


def _tile_info_spec(n_tiles, ahead=0):
    return pl.BlockSpec((1, 1, LANE), lambda i, *_: (jnp.minimum(i + ahead, n_tiles - 1), 0, 0),
                        memory_space=pltpu.SMEM)


def _dispatch(x2, spos, gate, tinfo, base, seg, rows):
    n, d = x2.shape
    tb = min(MOE_TILE, n)
    row = pl.BlockSpec((TOP_K, tb), lambda i, *_: (0, i))
    return pl.pallas_call(
        _dispatch_kernel,
        grid_spec=pltpu.PrefetchScalarGridSpec(
            num_scalar_prefetch=2,
            grid=(n // tb,),
            in_specs=[_tile_info_spec(n // tb), row, row, pl.BlockSpec((tb, d), lambda i, *_: (i, 0))],
            out_specs=pl.BlockSpec(memory_space=pl.ANY),
            scratch_shapes=[pltpu.VMEM((2, _sorted_rows(tb), d + LANE), F32), pltpu.VMEM((ROW_ALIGN, d + LANE), F32),
                            pltpu.SMEM((2,), jnp.int32), pltpu.SemaphoreType.DMA((2,))],
        ),
        out_shape=jax.ShapeDtypeStruct((rows, d + LANE), F32),
        compiler_params=_cparams(("arbitrary",)),
        name="moe_dispatch",
    )(base, seg, tinfo, spos, gate, x2)


def _expert_kernel(be_ref, nused_ref, x_ref, wgu_ref, bgu_ref, wd_ref, bd_ref, y_ref, wgu_s, wd_s):
    i = pl.program_id(0)
    used = i < nused_ref[0]

    @pl.when(used & ((i == 0) | (be_ref[i] != be_ref[jnp.maximum(i - 1, 0)])))
    def _():
        wgu_s[...] = wgu_ref[0, 0].astype(BF16)
        wd_s[...] = wd_ref[0, 0].astype(BF16)

    @pl.when(used)
    def _():
        d = y_ref.shape[1]
        xb = x_ref[:, :d].astype(BF16)
        h = _dot(xb, wgu_s[...]) + bgu_ref[0, 0]
        h_gate = jnp.minimum(h[:, :D_FF], SWIGLU_LIMIT)
        h_up = jnp.clip(h[:, D_FF:], -SWIGLU_LIMIT, SWIGLU_LIMIT)
        act = (h_up + 1.0) * h_gate * jax.nn.sigmoid(SWIGLU_ALPHA * h_gate)
        y_ref[...] = (_dot(act.astype(BF16), wd_s[...]) + bd_ref[0, 0]) * x_ref[:, d:d + 1]

    @pl.when(jnp.logical_not(used))
    def _():
        y_ref[...] = jnp.zeros(y_ref.shape, F32)


def _experts(xs, block_expert, n_used, layer, w_gu, b_gu, w_down, b_down):
    rows = xs.shape[0]
    d = D_MODEL
    n_blocks = rows // MOE_BLOCK
    f2 = w_gu.shape[3]
    depth = w_gu.shape[0]
    xmap = lambda i, be, nu: (jnp.minimum(i, nu[0] - 1), 0)
    return pl.pallas_call(
        _expert_kernel,
        grid_spec=pltpu.PrefetchScalarGridSpec(
            num_scalar_prefetch=2,
            grid=(n_blocks,),
            in_specs=[
                pl.BlockSpec((MOE_BLOCK, d + LANE), xmap),
                pl.BlockSpec((1, 1, d, f2), lambda i, be, nu: (layer, be[i], 0, 0)),
                pl.BlockSpec((1, 1, 1, f2), lambda i, be, nu: (layer, be[i], 0, 0)),
                pl.BlockSpec((1, 1, D_FF, d), lambda i, be, nu: (layer, be[i], 0, 0)),
                pl.BlockSpec((1, 1, 1, d), lambda i, be, nu: (layer, be[i], 0, 0)),
            ],
            out_specs=pl.BlockSpec((MOE_BLOCK, d), lambda i, be, nu: (i, 0)),
            scratch_shapes=[pltpu.VMEM((d, f2), BF16), pltpu.VMEM((D_FF, d), BF16)],
        ),
        out_shape=jax.ShapeDtypeStruct((rows, d), F32),
        compiler_params=_cparams(("arbitrary",)),
        name="moe_experts",
    )(block_expert, n_used, xs, w_gu, b_gu.reshape(depth, N_EXPERTS, 1, f2), w_down,
      b_down.reshape(depth, N_EXPERTS, 1, d))


def _combine_kernel(base_ref, tinfo_ref, tnext_ref, spos_ref, x_ref, g_ref, b_ref, y_ref, o_ref, ybuf2_s, sems):
    tb = x_ref.shape[0]
    s_rows = ybuf2_s.shape[1]
    step = pl.program_id(0)
    ybuf_s, sem = ybuf2_s.at[step % 2], sems.at[step % 2]
    ynext_s, next_sem = ybuf2_s.at[(step + 1) % 2], sems.at[(step + 1) % 2]

    @pl.when(step == 0)
    def _():
        ybuf2_s[...] = jnp.zeros(ybuf2_s.shape, F32)
        _for_each_group(tinfo_ref, base_ref, lambda s, r, rows: _group_copy(y_ref, r, ybuf_s, s, sem, rows).start())

    @pl.when(step + 1 < pl.num_programs(0))
    def _():
        _for_each_group(tnext_ref, base_ref,
                        lambda s, r, rows: _group_copy(y_ref, r, ynext_s, s, next_sem, rows).start())

    spos = spos_ref[...]
    n_split = COMBINE_SPLIT if s_rows % (COMBINE_SPLIT * LANE) == 0 else 1
    width = s_rows // n_split

    def onehot(c0):
        c_iota = c0 + lax.broadcasted_iota(jnp.int32, (tb, width), 1)
        hit = spos[:, 0:1] == c_iota
        for k in range(1, TOP_K):
            hit = hit | (spos[:, k:k + 1] == c_iota)
        return jnp.where(hit, 1.0, 0.0).astype(BF16)

    p_next = onehot(0)
    _wait_groups(tinfo_ref[0, 0, 3 * N_EXPERTS], tinfo_ref[0, 0, 3 * N_EXPERTS + 1], y_ref, ybuf_s, sem)
    moe = None
    for c in range(n_split):
        part = _dot(p_next, ybuf_s[c * width:(c + 1) * width, :].astype(BF16))
        if c + 1 < n_split:
            p_next = onehot((c + 1) * width)
        moe = part if moe is None else moe + part
    o_ref[...] = _layer_norm(DEEPNORM_ALPHA * x_ref[...] + moe, g_ref[...], b_ref[...])


def _combine(y, x2, spos_col, tinfo, base, ln_g, ln_b):
    n, d = x2.shape
    tb = min(MOE_TILE, n)
    s_rows = _sorted_rows(tb)
    return pl.pallas_call(
        _combine_kernel,
        grid_spec=pltpu.PrefetchScalarGridSpec(
            num_scalar_prefetch=1,
            grid=(n // tb,),
            in_specs=[
                _tile_info_spec(n // tb), _tile_info_spec(n // tb, ahead=1),
                pl.BlockSpec((tb, TOP_K), lambda i, *_: (i, 0)),
                pl.BlockSpec((tb, d), lambda i, *_: (i, 0)),
                pl.BlockSpec((1, d), lambda i, *_: (0, 0)),
                pl.BlockSpec((1, d), lambda i, *_: (0, 0)),
                pl.BlockSpec(memory_space=pl.ANY),
            ],
            out_specs=pl.BlockSpec((tb, d), lambda i, *_: (i, 0)),
            scratch_shapes=[pltpu.VMEM((2, s_rows, d), F32), pltpu.SemaphoreType.DMA((2,))],
        ),
        out_shape=jax.ShapeDtypeStruct((n, d), F32),
        compiler_params=_cparams(("arbitrary",)),
        name="moe_combine",
    )(base, tinfo, tinfo, spos_col, x2, ln_g.reshape(1, d), ln_b.reshape(1, d), y)


def _sorted_rows(tb):
    return TOP_K * tb + N_EXPERTS * ROW_ALIGN


def _moe_ffn_ln(x2, layer, router_w, router_b, w_gu, b_gu, w_down, b_down, ln_g, ln_b):
    n, _ = x2.shape
    n_tiles = n // min(MOE_TILE, n)
    spos, gate, tinfo, seg = _router(x2, router_w, router_b)
    seg = seg.reshape(N_EXPERTS)
    max_rows = n * TOP_K + n_tiles * N_EXPERTS * (ROW_ALIGN - 1) + N_EXPERTS * (MOE_BLOCK - 1)
    n_blocks = (max_rows + MOE_BLOCK - 1) // MOE_BLOCK
    padded = (seg + MOE_BLOCK - 1) // MOE_BLOCK * MOE_BLOCK
    padded_end = jnp.cumsum(padded)
    base = jnp.concatenate([jnp.zeros((1,), jnp.int32), padded_end]).astype(jnp.int32)
    n_used = (padded_end[-1:] // MOE_BLOCK).astype(jnp.int32)
    block_start = jnp.arange(n_blocks, dtype=jnp.int32) * MOE_BLOCK
    block_expert = jnp.sum(block_start[:, None] >= padded_end[None, :], axis=1).astype(jnp.int32)
    last_expert = jnp.sum(padded_end[-1] - 1 >= padded_end).astype(jnp.int32)
    block_expert = jnp.minimum(block_expert, last_expert)
    xs = _dispatch(x2, spos, gate, tinfo, base, seg, n_blocks * MOE_BLOCK)
    y = _experts(xs, block_expert, n_used, layer, w_gu, b_gu, w_down, b_down)
    return _combine(y, x2, spos.T, tinfo, base, ln_g, ln_b)


def kernel(x, ln_g, ln_b, nsa_w_in, nsa_cmp_pos, nsa_cmp_w1, nsa_cmp_b1, nsa_cmp_w2, nsa_cmp_b2, nsa_gate_b, nsa_w_out, ml_w_in, ml_conv_w, ml_conv_b, ml_gate_b, ml_norm_g, ml_w_out, hg_w_in, hg_lower, hg_norm_g, hg_w_out, router_w, router_b, moe_w_gu, moe_b_gu, moe_w_down, moe_b_down):
    bsz, t, d = x.shape
    x2 = x.reshape(bsz * t, d)
    for layer in range(DEPTH):
        kind, slot = layer % N_MIXERS, layer // N_MIXERS
        if kind == 0:
            x2 = _nsa_mixer(x2, bsz, t, nsa_w_in[slot], nsa_cmp_pos[slot], nsa_cmp_w1[slot], nsa_cmp_b1[slot],
                            nsa_cmp_w2[slot], nsa_cmp_b2[slot], nsa_gate_b[slot], nsa_w_out[slot],
                            ln_g[layer, 0], ln_b[layer, 0])
        elif kind == 1:
            x2 = _mlstm_mixer(x2, bsz, t, ml_w_in[slot], ml_conv_w[slot], ml_conv_b[slot], ml_gate_b[slot],
                              ml_norm_g[slot], ml_w_out[slot], ln_g[layer, 0], ln_b[layer, 0])
        else:
            x2 = _hgrn_mixer(x2, bsz, t, layer, hg_w_in[slot], hg_lower, hg_norm_g[slot], hg_w_out[slot],
                             ln_g[layer, 0], ln_b[layer, 0])
        x2 = _moe_ffn_ln(x2, layer, router_w[layer], router_b[layer], moe_w_gu, moe_b_gu, moe_w_down, moe_b_down,
                         ln_g[layer, 1], ln_b[layer, 1])
    return x2.reshape(bsz, t, d)
```

```python
import functools

import numpy as np
import jax
import jax.numpy as jnp
from jax import lax
from jax.experimental import pallas as pl
from jax.experimental.pallas import tpu as pltpu

F32 = jnp.float32
BF16 = jnp.bfloat16
HIGHEST = lax.Precision.HIGHEST

D_MODEL = 1024
DEPTH = 4
N_MIXERS = 3

NSA_HEADS = 16
NSA_KV_GROUPS = 4
NSA_HEAD_DIM = 64
NSA_HPG = 4
NSA_KV_WIDTH = 256
CMP_BLOCK = 32
CMP_STRIDE = 16
CMP_HIDDEN = 256
SEL_BLOCK = 64
SEL_TOPK = 8
WINDOW = 512

MLSTM_HEADS = 4
MLSTM_QK_DIM = 128
MLSTM_V_DIM = 256
MLSTM_CONV = 4

HGRN_HEADS = 8
HGRN_DIM = 128
HGRN_CHUNK = 32

N_EXPERTS = 32
TOP_K = 4
D_FF = 1024
SWIGLU_LIMIT = 7.0
SWIGLU_ALPHA = 1.702

LN_EPS = 1e-5
RMS_EPS = 1e-6
DEEPNORM_ALPHA = (2 * DEPTH) ** 0.25
NEG = -1e30
M_FLOOR = -1e20

LANE = 128
VMEM_LIMIT = 56 * 1024 * 1024

PROJ_TM = 512
MOE_TILE = 512
MOE_BLOCK = 512
ROW_ALIGN = 8
BIG_GROUP = 32
SORT_ROWS = 256
COMBINE_SPLIT = 3
NSA_TQ = 256
NSA_TK = 512
MLSTM_L = 512
HGRN_SUPER = 128


def _cparams(sem):
    return pltpu.CompilerParams(dimension_semantics=sem, vmem_limit_bytes=VMEM_LIMIT)


def _dot(a, b):
    return jnp.dot(a, b, preferred_element_type=F32)


def _dot_nt(a, b, precision=None):
    return lax.dot_general(a, b, (((1,), (1,)), ((), ())), precision=precision, preferred_element_type=F32)


def _dot_tn(a, b):
    return lax.dot_general(a, b, (((0,), (0,)), ((), ())), preferred_element_type=F32)


def _layer_norm(z, g, b):
    mu = jnp.mean(z, axis=-1, keepdims=True)
    zc = z - mu
    var = jnp.mean(zc * zc, axis=-1, keepdims=True)
    return zc * lax.rsqrt(var + LN_EPS) * g + b


def _proj_kernel(x_ref, *refs, n_w, n_t):
    w_refs = refs[:n_w]
    wt_refs = refs[n_w:n_w + n_t]
    o_refs = refs[n_w + n_t:2 * n_w + n_t]
    ot_refs = refs[2 * n_w + n_t:]
    x = x_ref[...]
    xb = x.astype(BF16)
    for w_ref, o_ref in zip(w_refs, o_refs):
        ncol = w_ref.shape[1]
        for c0 in range(0, ncol, 512):
            c1 = min(c0 + 512, ncol)
            o_ref[:, c0:c1] = _dot(xb, w_ref[:, c0:c1]).astype(o_ref.dtype)
    for wt_ref, o_ref in zip(wt_refs, ot_refs):
        o_ref[...] = _dot_nt(wt_ref[...], x, precision=HIGHEST)


def _proj(x2, weights, out_dtypes, weights_t=()):
    n, d = x2.shape
    tm = min(PROJ_TM, n)
    in_specs = [pl.BlockSpec((tm, d), lambda i: (i, 0))]
    in_specs += [pl.BlockSpec(w.shape, lambda i: (0, 0)) for w in weights]
    in_specs += [pl.BlockSpec(w.shape, lambda i: (0, 0)) for w in weights_t]
    out_shape = [jax.ShapeDtypeStruct((n, w.shape[1]), dt) for w, dt in zip(weights, out_dtypes)]
    out_shape += [jax.ShapeDtypeStruct((w.shape[0], n), F32) for w in weights_t]
    out_specs = [pl.BlockSpec((tm, w.shape[1]), lambda i: (i, 0)) for w in weights]
    out_specs += [pl.BlockSpec((w.shape[0], tm), lambda i: (0, i)) for w in weights_t]
    return pl.pallas_call(
        functools.partial(_proj_kernel, n_w=len(weights), n_t=len(weights_t)),
        grid=(n // tm,),
        in_specs=in_specs,
        out_specs=out_specs,
        out_shape=out_shape,
        compiler_params=_cparams(("parallel",)),
        name="proj",
    )(x2, *weights, *weights_t)


def _outproj_ln_kernel(h_ref, w_ref, x_ref, g_ref, b_ref, o_ref):
    y = _dot(h_ref[...], w_ref[...])
    o_ref[...] = _layer_norm(DEEPNORM_ALPHA * x_ref[...] + y, g_ref[...], b_ref[...])


def _outproj_ln(h2, w, x2, g, b):
    n, d = x2.shape
    tm = min(PROJ_TM, n)
    return pl.pallas_call(
        _outproj_ln_kernel,
        grid=(n // tm,),
        in_specs=[
            pl.BlockSpec((tm, d), lambda i: (i, 0)),
            pl.BlockSpec((d, d), lambda i: (0, 0)),
            pl.BlockSpec((tm, d), lambda i: (i, 0)),
            pl.BlockSpec((1, d), lambda i: (0, 0)),
            pl.BlockSpec((1, d), lambda i: (0, 0)),
        ],
        out_specs=pl.BlockSpec((tm, d), lambda i: (i, 0)),
        out_shape=jax.ShapeDtypeStruct((n, d), F32),
        compiler_params=_cparams(("parallel",)),
        name="outproj_ln",
    )(h2, w, x2, g.reshape(1, d), b.reshape(1, d))


def _gelu_tanh(x):
    return 0.5 * x * (1.0 + jnp.tanh(np.sqrt(2.0 / np.pi).astype(np.float32) * (x + 0.044715 * (x * x * x))))


def _nsa_compress_kernel(kv_ref, wcat_ref, posa_ref, posb_ref, b1_ref, w2_ref, b2_ref, o_ref):
    n_row = kv_ref.shape[1] // CMP_STRIDE
    half = 2 * CMP_HIDDEN
    acc_a = jnp.zeros((n_row, half), F32)
    acc_b = jnp.zeros((n_row, half), F32)
    for r in range(CMP_STRIDE):
        xr = kv_ref[0, pl.ds(r, n_row, stride=CMP_STRIDE), :]
        w = wcat_ref[r]
        acc_a = acc_a + _dot((xr + posa_ref[r]).astype(BF16), w[:, :half])
        acc_b = acc_b + _dot((xr + posb_ref[r]).astype(BF16), w[:, half:])
    hidden = acc_a + pltpu.roll(acc_b, n_row - 1, 0) + b1_ref[...]
    hidden = _gelu_tanh(hidden)
    o_ref[0, 0] = (_dot(hidden.astype(BF16), w2_ref[...]) + b2_ref[...]).astype(o_ref.dtype)


def _nsa_compress(kvc, wcat, posa, posb, b1cat, w2cat, b2cat):
    bsz, t, _ = kvc.shape
    n_row = t // CMP_STRIDE
    return pl.pallas_call(
        _nsa_compress_kernel,
        grid=(bsz, NSA_KV_GROUPS),
        in_specs=[
            pl.BlockSpec((1, t, LANE), lambda b, g: (b, 0, g)),
            pl.BlockSpec(wcat.shape, lambda b, g: (0, 0, 0)),
            pl.BlockSpec(posa.shape, lambda b, g: (0, 0, 0)),
            pl.BlockSpec(posb.shape, lambda b, g: (0, 0, 0)),
            pl.BlockSpec(b1cat.shape, lambda b, g: (0, 0)),
            pl.BlockSpec(w2cat.shape, lambda b, g: (0, 0)),
            pl.BlockSpec(b2cat.shape, lambda b, g: (0, 0)),
        ],
        out_specs=pl.BlockSpec((1, 1, n_row, LANE), lambda b, g: (b, g, 0, 0)),
        out_shape=jax.ShapeDtypeStruct((bsz, NSA_KV_GROUPS, n_row, LANE), BF16),
        compiler_params=_cparams(("parallel", "parallel")),
        name="nsa_compress",
    )(kvc, wcat, posa, posb, b1cat, w2cat, b2cat)


def _nsa_attn_kernel(q_ref, kvc_ref, kvs_ref, kvw_ref, gate_ref, gb_ref, c2s_ref, o_ref, *, tq, tk, n_cmp):
    def tile(qi, carry):
        _nsa_attn_tile(qi, q_ref, kvc_ref, kvs_ref, kvw_ref, gate_ref, gb_ref, c2s_ref, o_ref, tq=tq, tk=tk, n_cmp=n_cmp)
        return carry
    lax.fori_loop(0, q_ref.shape[1] // tq, tile, 0)


def _nsa_attn_tile(qi, q_ref, kvc_ref, kvs_ref, kvw_ref, gate_ref, gb_ref, c2s_ref, o_ref, *, tq, tk, n_cmp):
    hpg = NSA_HPG
    t0 = pl.multiple_of(qi * tq, tq)
    t_col = t0 + lax.broadcasted_iota(jnp.int32, (tq, 1), 0)
    lane = lax.broadcasted_iota(jnp.int32, (1, LANE), 1)
    ones_lane = (lane == 0).astype(BF16)

    def with_ones(kv):
        return jnp.where(lane < NSA_HEAD_DIM, ones_lane, kv)

    gates = jax.nn.sigmoid(gate_ref[0, pl.ds(t0, tq), :].astype(F32) + gb_ref[0])
    gate_of = [[jnp.broadcast_to(gates[:, 3 * h + j:3 * h + j + 1], (tq, LANE)) for j in range(3)] for h in range(hpg)]

    span = WINDOW + tq
    w0 = pl.multiple_of(jnp.maximum(t0 - WINDOW, 0), tq)
    kvw = kvw_ref[0, pl.ds(w0, span), :]
    kvw1 = with_ones(kvw)
    kvc = kvc_ref[0, 0]
    n_c = kvc.shape[0]
    q_heads = []
    for h in range(hpg):
        pair_blk = q_ref[0, pl.ds(t0, tq), (h // 2) * LANE:(h // 2 + 1) * LANE].astype(F32)
        if h % 2:
            pair_blk = pltpu.roll(pair_blk, NSA_HEAD_DIM, 1)
        q_heads.append((jnp.where(lane < NSA_HEAD_DIM, pair_blk, 0.0) * NSA_HEAD_DIM ** -0.5).astype(BF16))
    s_c = [_dot_nt(q_heads[h], kvc) for h in range(hpg)]
    s_w = [_dot_nt(q_heads[h], kvw).astype(BF16) for h in range(hpg)]

    c_lane = lax.broadcasted_iota(jnp.int32, (1, n_c), 1)
    valid_c = (c_lane * CMP_STRIDE + (CMP_BLOCK - 1) <= t_col) & (c_lane < n_cmp)
    p_heads = []
    for h in range(hpg):
        sm = jnp.where(valid_c, s_c[h], NEG)
        m = jnp.max(sm, axis=-1, keepdims=True)
        p = jnp.where(valid_c, jnp.exp(sm - m), 0.0)
        l = jnp.sum(p, axis=-1, keepdims=True)
        p_heads.append(p / jnp.maximum(l, 1e-30))
    p_sum = p_heads[0]
    for h in range(1, hpg):
        p_sum = p_sum + p_heads[h]
    imp = _dot_nt(c2s_ref[...], p_sum, precision=HIGHEST)
    o_c = [_dot(p_heads[h].astype(BF16), kvc) for h in range(hpg)]

    lag = t_col - (w0 + lax.broadcasted_iota(jnp.int32, (1, span), 1))
    bias_w = jnp.where((lag >= 0) & (lag < WINDOW), 0.0, NEG).astype(BF16)
    o_w = []
    for h in range(hpg):
        sm = s_w[h] + bias_w
        o_w.append(_dot(jnp.exp(sm - jnp.max(sm, axis=-1, keepdims=True)), kvw1))
    partial = [gate_of[h][0] * o_c[h] + gate_of[h][2] * (o_w[h] / o_w[h][:, 0:1]) for h in range(hpg)]

    n_sel = imp.shape[0]
    blk = lax.broadcasted_iota(jnp.int32, (n_sel, 1), 0)
    cur = (t0 + lax.broadcasted_iota(jnp.int32, (1, tq), 1)) // SEL_BLOCK
    forced = (blk == 0) | (blk == cur) | (blk == cur - 1)
    score = jnp.where(forced, -NEG, jnp.where(blk <= cur, imp, NEG))
    rank = jnp.zeros((n_sel, tq), F32)
    for i in range(n_sel):
        s_i = score[i:i + 1, :]
        ahead = (s_i > score) | ((s_i == score) & (blk > i))
        rank = rank + ahead.astype(F32)
    member = ((rank < SEL_TOPK) & (score > 0.5 * NEG)).astype(BF16)

    blk_per_tile = tk // SEL_BLOCK
    key_lane = lax.broadcasted_iota(jnp.int32, (1, tk), 1)

    def sel_body(kt, carry):
        k0 = pl.multiple_of(kt * tk, tk)
        kv = kvs_ref[0, pl.ds(k0, tk), :]
        kv1 = with_ones(kv)
        expand = (key_lane // SEL_BLOCK + kt * blk_per_tile == blk).astype(BF16)
        valid = (_dot_tn(member, expand) > 0.5) & (k0 + key_lane <= t_col)
        bias = jnp.where(valid, 0.0, NEG).astype(BF16)
        s = [_dot_nt(q_heads[h], kv).astype(BF16) for h in range(hpg)]
        new = []
        for h in range(hpg):
            m_old, acc_old = carry[h]
            sm = s[h] + bias
            m_new = jnp.maximum(m_old, jnp.max(sm, axis=-1, keepdims=True).astype(F32))
            pv = _dot(jnp.exp(sm - m_new.astype(BF16)), kv1)
            new.append((m_new, jnp.exp(m_old - m_new) * acc_old + pv))
        return tuple(new)

    init = tuple((jnp.full((tq, 1), M_FLOOR, F32), jnp.zeros((tq, LANE), F32)) for _ in range(hpg))
    n_kt = (t0 + tq - 1) // tk + 1
    sel = lax.fori_loop(0, n_kt, sel_body, init)

    outs = []
    for h in range(hpg):
        o_s = sel[h][1]
        outs.append(partial[h] + gate_of[h][1] * (o_s / o_s[:, 0:1]))
    for j in range(hpg // 2):
        pair = jnp.where(lane < NSA_HEAD_DIM, pltpu.roll(outs[2 * j], NSA_HEAD_DIM, 1), outs[2 * j + 1])
        o_ref[0, pl.ds(t0, tq), j * LANE:(j + 1) * LANE] = pair.astype(o_ref.dtype)


def _nsa_attention(main, kvcmp, gate_b, c2s, bsz, t):
    tq = min(NSA_TQ, t)
    tk = min(NSA_TK, t)
    n_cmp = (t - CMP_BLOCK) // CMP_STRIDE + 1
    q_blocks = NSA_HEADS * NSA_HEAD_DIM // LANE
    g = NSA_KV_GROUPS
    kernel = functools.partial(_nsa_attn_kernel, tq=tq, tk=tk, n_cmp=n_cmp)
    return pl.pallas_call(
        kernel,
        grid=(bsz, g),
        in_specs=[
            pl.BlockSpec((1, t, NSA_HPG * NSA_HEAD_DIM), lambda b, gi: (b, 0, gi)),
            pl.BlockSpec((1, 1) + kvcmp.shape[2:], lambda b, gi: (b, gi, 0, 0)),
            pl.BlockSpec((1, t, LANE), lambda b, gi: (b, 0, q_blocks + gi)),
            pl.BlockSpec((1, t, LANE), lambda b, gi: (b, 0, q_blocks + g + gi)),
            pl.BlockSpec((1, t, LANE), lambda b, gi: (b, 0, q_blocks + 2 * g + gi)),
            pl.BlockSpec((1, 1, LANE), lambda b, gi: (gi, 0, 0)),
            pl.BlockSpec(c2s.shape, lambda b, gi: (0, 0)),
        ],
        out_specs=pl.BlockSpec((1, t, NSA_HPG * NSA_HEAD_DIM), lambda b, gi: (b, 0, gi)),
        out_shape=jax.ShapeDtypeStruct((bsz, t, D_MODEL), BF16),
        compiler_params=_cparams(("parallel", "parallel")),
        name="nsa_attention",
    )(main, kvcmp, main, main, main, gate_b, c2s)


def _nsa_mixer(x2, bsz, t, w_in, cmp_pos, cmp_w1, cmp_b1, cmp_w2, cmp_b2, gate_b, w_out, ln_g, ln_b):
    d, g, dh, hpg = D_MODEL, NSA_KV_GROUPS, NSA_HEAD_DIM, NSA_HPG
    kvw_ = NSA_KV_WIDTH

    def pair(k0):
        k = w_in[:, k0:k0 + kvw_].reshape(d, g, 1, dh)
        v = w_in[:, k0 + kvw_:k0 + 2 * kvw_].reshape(d, g, 1, dh)
        return jnp.concatenate([k, v], axis=2).reshape(d, g * LANE)

    wq = w_in[:, :d]
    n_gate = 3 * hpg
    wg = jnp.pad(w_in[:, d + 6 * kvw_:].reshape(d, g, n_gate), ((0, 0), (0, 0), (0, LANE - n_gate))).reshape(d, g * LANE)
    w_main = jnp.concatenate([wq, pair(d + 2 * kvw_), pair(d + 4 * kvw_), wg], axis=1).astype(BF16)
    w_kvc = pair(d).astype(BF16)
    main, kvc = _proj(x2, [w_main, w_kvc], [BF16, F32])

    half = CMP_BLOCK // 2
    w1 = cmp_w1.reshape(2, CMP_BLOCK, dh, CMP_HIDDEN)
    z = jnp.zeros((half, dh, CMP_HIDDEN), F32)
    top = jnp.concatenate([w1[0, :half], z, w1[0, half:], z], axis=2)
    bot = jnp.concatenate([z, w1[1, :half], z, w1[1, half:]], axis=2)
    wcat = jnp.concatenate([top, bot], axis=1).astype(BF16)
    pos = jnp.concatenate([cmp_pos[0], cmp_pos[1]], axis=-1)
    posa, posb = pos[:half, None, :], pos[half:, None, :]
    b1cat = cmp_b1.reshape(1, 2 * CMP_HIDDEN)
    zz = jnp.zeros((CMP_HIDDEN, dh), F32)
    w2cat = jnp.concatenate([jnp.concatenate([cmp_w2[0], zz], axis=1),
                             jnp.concatenate([zz, cmp_w2[1]], axis=1)], axis=0).astype(BF16)
    b2cat = cmp_b2.reshape(1, 2 * dh)
    kvcmp = _nsa_compress(kvc.reshape(bsz, t, g * LANE), wcat, posa, posb, b1cat, w2cat, b2cat)

    n_row = t // CMP_STRIDE
    n_sel = t // SEL_BLOCK
    cmp_start = np.arange(n_row) * CMP_STRIDE
    sel_start = np.arange(n_sel) * SEL_BLOCK
    overlap = (np.minimum(cmp_start[:, None] + CMP_BLOCK, sel_start[None, :] + SEL_BLOCK)
               - np.maximum(cmp_start[:, None], sel_start[None, :]))
    c2s = jnp.asarray((np.clip(overlap, 0, None) / CMP_STRIDE).astype(np.float32).T)
    gb = jnp.pad(gate_b.reshape(g, 1, n_gate), ((0, 0), (0, 0), (0, LANE - n_gate)))
    o = _nsa_attention(main.reshape(bsz, t, -1), kvcmp, gb, c2s, bsz, t)
    return _outproj_ln(o.reshape(bsz * t, d), w_out.astype(BF16), x2, ln_g, ln_b)


def _mlstm_kernel(q_ref, k_ref, v_ref, og_ref, gif_ref, gb_ref, cwq_ref, cwk_ref, cbq_ref, cbk_ref, ng_ref,
                  o_ref, pad_s, q_s, k_s, ct_s, *, chunk):
    t = q_ref.shape[1]
    dk, dv = MLSTM_QK_DIM, MLSTM_V_DIM
    head = pl.program_id(1)
    halo = 8

    def conv_silu(x_ref, w_ref, b_ref, dst, scale):
        pad_s[0:halo, :] = jnp.zeros((halo, dk), F32)
        pad_s[halo:halo + t, :] = x_ref[0]
        rows = min(t, 256)
        for r0 in range(0, t, rows):
            y = b_ref[...] + w_ref[0:1, :] * pad_s[pl.ds(halo + r0 - (MLSTM_CONV - 1), rows), :]
            for j in range(1, MLSTM_CONV):
                y = y + w_ref[j:j + 1, :] * pad_s[pl.ds(halo + r0 - (MLSTM_CONV - 1) + j, rows), :]
            y = y * jax.nn.sigmoid(y)
            dst[r0:r0 + rows, :] = (y * scale).astype(dst.dtype)

    conv_silu(q_ref, cwq_ref, cbq_ref, q_s, 1.0)
    conv_silu(k_ref, cwk_ref, cbk_ref, k_s, dk ** -0.5)
    ct_s[...] = jnp.zeros((dk, dv), F32)

    row = lax.broadcasted_iota(jnp.int32, (chunk, chunk), 0)
    col = lax.broadcasted_iota(jnp.int32, (chunk, chunk), 1)
    causal = col <= row
    eye = col == row
    upper = (row <= col).astype(F32)
    b_i = gb_ref[pl.ds(head, 1), :]
    b_f = gb_ref[pl.ds(MLSTM_HEADS + head, 1), :]
    norm_g = ng_ref[...]

    def body(c, carry):
        n_row, m_prev = carry
        r0 = pl.multiple_of(c * chunk, chunk)
        qc = q_s[pl.ds(r0, chunk), :]
        kc = k_s[pl.ds(r0, chunk), :]
        vc = v_ref[0, pl.ds(r0, chunk), :]
        li_row = gif_ref[0, c, pl.ds(head, 1), :] + b_i
        zf = gif_ref[0, c, pl.ds(MLSTM_HEADS + head, 1), :] + b_f
        lf_row = jnp.minimum(zf, 0.0) - jnp.log1p(jnp.exp(-jnp.abs(zf)))
        b_row = jnp.dot(lf_row, upper, precision=HIGHEST, preferred_element_type=F32)
        b_col = jnp.sum(jnp.where(causal, lf_row, 0.0), axis=-1, keepdims=True)
        li_col = jnp.sum(jnp.where(eye, li_row, 0.0), axis=-1, keepdims=True)
        d_mat = jnp.where(causal, b_col - b_row + li_row, NEG)
        m_t = jnp.maximum(b_col + m_prev, jnp.max(d_mat, axis=-1, keepdims=True))
        w_inter = jnp.exp(b_col + m_prev - m_t)
        s = _dot_nt(qc, kc) * jnp.exp(d_mat - m_t)
        num = _dot(s.astype(BF16), vc) + w_inter * _dot(qc, ct_s[...].astype(BF16))
        den = jnp.sum(s, axis=-1, keepdims=True) + w_inter * jnp.sum(qc.astype(F32) * n_row, axis=-1, keepdims=True)
        h = num / jnp.maximum(jnp.abs(den), jnp.exp(-m_t))
        b_end = b_col[chunk - 1:chunk, :]
        decay = b_end - b_col + li_col
        m_new = jnp.maximum(b_end + m_prev, jnp.max(decay, axis=0, keepdims=True))
        w_k = jnp.exp(decay - m_new)
        scale = jnp.exp(b_end + m_prev - m_new)
        ct_s[...] = scale * ct_s[...] + _dot_tn(kc, (vc.astype(F32) * w_k).astype(BF16))
        n_new = scale * n_row + jnp.sum(w_k * kc.astype(F32), axis=0, keepdims=True)
        hn = h * lax.rsqrt(jnp.mean(h * h, axis=-1, keepdims=True) + RMS_EPS) * norm_g
        og = og_ref[0, pl.ds(r0, chunk), :].astype(F32)
        o_ref[0, pl.ds(r0, chunk), :] = (hn * jax.nn.sigmoid(og)).astype(o_ref.dtype)
        return n_new, m_new

    lax.fori_loop(0, t // chunk, body, (jnp.zeros((1, dk), F32), jnp.zeros((1, 1), F32)))


def _mlstm_core(qk, v, og, gif, gate_b, conv_w, conv_b, norm_g, bsz, t, chunk):
    hh, dk, dv = MLSTM_HEADS, MLSTM_QK_DIM, MLSTM_V_DIM
    nc = t // chunk
    kernel = functools.partial(_mlstm_kernel, chunk=chunk)
    return pl.pallas_call(
        kernel,
        grid=(bsz, hh),
        in_specs=[
            pl.BlockSpec((1, t, dk), lambda b, h: (b, 0, h)),
            pl.BlockSpec((1, t, dk), lambda b, h: (b, 0, hh + h)),
            pl.BlockSpec((1, t, dv), lambda b, h: (b, 0, h)),
            pl.BlockSpec((1, t, dv), lambda b, h: (b, 0, h)),
            pl.BlockSpec((1, nc, 2 * hh, chunk), lambda b, h: (b, 0, 0, 0)),
            pl.BlockSpec((2 * hh, 1), lambda b, h: (0, 0)),
            pl.BlockSpec((MLSTM_CONV, dk), lambda b, h: (0, h)),
            pl.BlockSpec((MLSTM_CONV, dk), lambda b, h: (0, hh + h)),
            pl.BlockSpec((1, dk), lambda b, h: (0, h)),
            pl.BlockSpec((1, dk), lambda b, h: (0, hh + h)),
            pl.BlockSpec((1, dv), lambda b, h: (0, h)),
        ],
        out_specs=pl.BlockSpec((1, t, dv), lambda b, h: (b, 0, h)),
        out_shape=jax.ShapeDtypeStruct((bsz, t, hh * dv), BF16),
        scratch_shapes=[
            pltpu.VMEM((t + 8, dk), F32),
            pltpu.VMEM((t, dk), BF16),
            pltpu.VMEM((t, dk), BF16),
            pltpu.VMEM((dk, dv), F32),
        ],
        compiler_params=_cparams(("parallel", "parallel")),
        name="mlstm",
    )(qk, qk, v, og, gif, gate_b.reshape(2 * hh, 1), conv_w, conv_w, conv_b.reshape(1, -1), conv_b.reshape(1, -1),
      norm_g.reshape(1, -1))


def _mlstm_mixer(x2, bsz, t, w_in, conv_w, conv_b, gate_b, norm_g, w_out, ln_g, ln_b):
    d, hh = D_MODEL, MLSTM_HEADS
    qkw = 2 * hh * MLSTM_QK_DIM
    vw = hh * MLSTM_V_DIM
    chunk = min(MLSTM_L, t)
    w_qk = w_in[:, :qkw].astype(BF16)
    w_v = w_in[:, qkw:qkw + vw].astype(BF16)
    w_og = w_in[:, qkw + vw:qkw + vw + d].astype(BF16)
    w_gif_t = w_in[:, qkw + vw + d:].T
    qk, v, og, gif_t = _proj(x2, [w_qk, w_v, w_og], [F32, BF16, BF16], [w_gif_t])
    gif = gif_t.reshape(2 * hh, bsz, t // chunk, chunk).transpose(1, 2, 0, 3)
    h = _mlstm_core(qk.reshape(bsz, t, qkw), v.reshape(bsz, t, vw), og.reshape(bsz, t, d), gif, gate_b,
                    conv_w, conv_b, norm_g, bsz, t, chunk)
    return _outproj_ln(h.reshape(bsz * t, d), w_out.astype(BF16), x2, ln_g, ln_b)


def _hgrn_kernel(q_ref, f_ref, i_ref, g_ref, low_ref, ng_ref, o_ref, st_s, qd_s, kd_s, ke_s, dec_s, *,
                 layer_idx, heads):
    t = q_ref.shape[1]
    dh, chunk = HGRN_DIM, HGRN_CHUNK
    low = low_ref[...]
    e = jnp.exp(low - jnp.max(low, axis=0, keepdims=True))
    soft = e / jnp.sum(e, axis=0, keepdims=True)
    lb = jnp.zeros((1, heads * dh), F32)
    for r in range(1, layer_idx + 1):
        lb = lb + soft[r:r + 1, :]
    norm_g = ng_ref[...]
    st_s[...] = jnp.zeros(st_s.shape, F32)
    width = heads * dh
    sup = min(HGRN_SUPER, t)
    n_sub = sup // chunk

    in_chunk = lax.broadcasted_iota(jnp.int32, (sup, 1), 0) % chunk

    def prep(s, carry):
        r0 = pl.multiple_of(s * sup, sup)
        f = lb + (1.0 - lb) * jax.nn.sigmoid(f_ref[0, pl.ds(r0, sup), :])
        gcum = jnp.log(f)
        shift = 1
        while shift < chunk:
            gcum = gcum + jnp.where(in_chunk >= shift, pltpu.roll(gcum, shift, 0), 0.0)
            shift *= 2
        g_end = jnp.concatenate(
            [jnp.broadcast_to(gcum[(j + 1) * chunk - 1:(j + 1) * chunk, :], (chunk, width)) for j in range(n_sub)], axis=0)
        qv = q_ref[0, pl.ds(r0, sup), :].astype(F32)
        kk = 1.0 - f
        qd_s[pl.ds(r0, sup), :] = (qv * jax.nn.sigmoid(qv) * jnp.exp(gcum)).astype(BF16)
        kd_s[pl.ds(r0, sup), :] = (kk * jnp.exp(-gcum)).astype(BF16)
        ke_s[pl.ds(r0, sup), :] = (kk * jnp.exp(g_end - gcum)).astype(BF16)
        for j in range(n_sub):
            dec_s[s, j:j + 1, :] = jnp.exp(gcum[(j + 1) * chunk - 1:(j + 1) * chunk, :])
        return carry

    lax.fori_loop(0, t // sup, prep, 0)

    row = lax.broadcasted_iota(jnp.int32, (sup, sup), 0)
    col = lax.broadcasted_iota(jnp.int32, (sup, sup), 1)
    keep = (col <= row) & (col // chunk == row // chunk)

    def body(s, carry):
        r0 = pl.multiple_of(s * sup, sup)
        decay = dec_s[s]
        col_of = [slice(h * dh, (h + 1) * dh) for h in range(heads)]
        sub_of = [slice(j * chunk, (j + 1) * chunk) for j in range(n_sub)]
        q_dec = [qd_s[pl.ds(r0, sup), c] for c in col_of]
        vv = [i_ref[0, pl.ds(r0, sup), c] for c in col_of]
        score = [_dot_nt(q_dec[h], kd_s[pl.ds(r0, sup), col_of[h]]) for h in range(heads)]
        k_end = [ke_s[pl.ds(r0, sup), c] for c in col_of]
        kv = [[_dot_tn(vv[h][r], k_end[h][r]) for r in sub_of] for h in range(heads)]
        intra = [_dot(jnp.where(keep, score[h], 0.0).astype(BF16), vv[h]) for h in range(heads)]
        states = []
        for h in range(heads):
            st = st_s[h]
            per_sub = []
            for j in range(n_sub):
                per_sub.append(st.astype(BF16))
                st = st * decay[j:j + 1, col_of[h]] + kv[h][j]
            st_s[h] = st
            states.append(per_sub)
        for h in range(heads):
            inter = [_dot_nt(q_dec[h][sub_of[j]], states[h][j]) for j in range(n_sub)]
            o = intra[h] + jnp.concatenate(inter, axis=0)
            on = o * lax.rsqrt(jnp.mean(o * o, axis=-1, keepdims=True) + RMS_EPS) * norm_g[:, col_of[h]]
            gate = jax.nn.sigmoid(g_ref[0, pl.ds(r0, sup), col_of[h]].astype(F32))
            o_ref[0, pl.ds(r0, sup), col_of[h]] = (on * gate).astype(o_ref.dtype)
        return carry

    lax.fori_loop(0, t // sup, body, 0)


def _hgrn_core(q, f, i, g, lower, norm_g, bsz, t, layer_idx):
    heads = 4
    width = heads * HGRN_DIM
    n_grp = HGRN_HEADS // heads
    kernel = functools.partial(_hgrn_kernel, layer_idx=layer_idx, heads=heads)
    act = pl.BlockSpec((1, t, width), lambda b, j: (b, 0, j))
    return pl.pallas_call(
        kernel,
        grid=(bsz, n_grp),
        in_specs=[act, act, act, act,
                  pl.BlockSpec((DEPTH, width), lambda b, j: (0, j)),
                  pl.BlockSpec((1, width), lambda b, j: (0, j))],
        out_specs=act,
        out_shape=jax.ShapeDtypeStruct((bsz, t, D_MODEL), BF16),
        scratch_shapes=[pltpu.VMEM((heads, HGRN_DIM, HGRN_DIM), F32), pltpu.VMEM((t, width), BF16),
                        pltpu.VMEM((t, width), BF16), pltpu.VMEM((t, width), BF16),
                        pltpu.VMEM((max(t // HGRN_SUPER, 1), min(HGRN_SUPER, t) // HGRN_CHUNK, width), F32)],
        compiler_params=_cparams(("parallel", "parallel")),
        name="hgrn2",
    )(q, f, i, g, lower, norm_g.reshape(1, -1))


def _hgrn_mixer(x2, bsz, t, layer_idx, w_in, lower, norm_g, w_out, ln_g, ln_b):
    d = D_MODEL
    ws = [w_in[:, j * d:(j + 1) * d].astype(BF16) for j in range(4)]
    q, f, i, g = _proj(x2, ws, [BF16, F32, BF16, BF16])
    shp = (bsz, t, d)
    o = _hgrn_core(q.reshape(shp), f.reshape(shp), i.reshape(shp), g.reshape(shp), lower, norm_g, bsz, t, layer_idx)
    return _outproj_ln(o.reshape(bsz * t, d), w_out.astype(BF16), x2, ln_g, ln_b)


def _router_kernel(x_ref, rwt_ref, rb_ref, spos_ref, gate_ref, tinfo_ref, seg_ref, carry_s):
    @pl.when(pl.program_id(0) == 0)
    def _():
        carry_s[...] = jnp.zeros(carry_s.shape, F32)

    tm = x_ref.shape[0]
    ne = N_EXPERTS
    logits = _dot_nt(rwt_ref[...], x_ref[...], precision=HIGHEST) + rb_ref[...]
    e_iota = lax.broadcasted_iota(jnp.int32, (ne, tm), 0)
    work = logits
    vals, picks = [], []
    for k in range(TOP_K):
        mx = jnp.max(work, axis=0, keepdims=True)
        idx = jnp.min(jnp.where(work == mx, e_iota, ne), axis=0, keepdims=True)
        pick = e_iota == idx
        vals.append(mx)
        picks.append(pick)
        work = jnp.where(pick, -jnp.inf, work)
    exps = [jnp.exp(v - vals[0]) for v in vals]
    tot = exps[0]
    for k in range(1, TOP_K):
        tot = tot + exps[k]
    for k in range(TOP_K):
        gate_ref[k:k + 1, :] = exps[k] / tot
    hot = picks[0].astype(F32)
    for k in range(1, TOP_K):
        hot = hot + picks[k].astype(F32)
    before = (lax.broadcasted_iota(jnp.int32, (tm, tm), 0) < lax.broadcasted_iota(jnp.int32, (tm, tm), 1)).astype(BF16)
    prior = _dot(hot.astype(BF16), before)
    cnt = jnp.sum(hot, axis=1, keepdims=True)
    run = jnp.floor((cnt + (ROW_ALIGN - 1)) * (1.0 / ROW_ALIGN)) * ROW_ALIGN
    sub = lax.broadcasted_iota(jnp.int32, (ne, ne), 0)
    lan = lax.broadcasted_iota(jnp.int32, (ne, ne), 1)
    run_row = jnp.sum(jnp.where(sub == lan, run, 0.0), axis=0, keepdims=True)
    soff = jnp.sum(jnp.where(lan < sub, run_row, 0.0), axis=1, keepdims=True)
    for k in range(TOP_K):
        pos = jnp.sum(jnp.where(picks[k], prior + soff, 0.0), axis=0, keepdims=True)
        spos_ref[k:k + 1, :] = pos.astype(jnp.int32)
    n_big = jnp.floor(run * (1.0 / BIG_GROUP))
    sub = lax.broadcasted_iota(jnp.int32, (ne, LANE), 0)
    lan = lax.broadcasted_iota(jnp.int32, (ne, LANE), 1)
    field, le = lan // ne, lan % ne
    carry = carry_s[...]
    info = (jnp.where((field == 0) & (sub == le), carry, 0.0) + jnp.where((field == 1) & (sub == le), run, 0.0)
            + jnp.where((field == 2) & (sub < le), run, 0.0)
            + jnp.where((field == 3) & (le == 0), (run - n_big * BIG_GROUP) * (1.0 / ROW_ALIGN), 0.0)
            + jnp.where((field == 3) & (le == 1), n_big, 0.0))
    tinfo_ref[0] = jnp.sum(info, axis=0, keepdims=True).astype(jnp.int32)
    total = carry + run
    carry_s[...] = total
    seg_ref[...] = total.astype(jnp.int32)


def _router(x2, router_w, router_b):
    n, d = x2.shape
    tm = min(MOE_TILE, n)
    n_tiles = n // tm
    row = pl.BlockSpec((TOP_K, tm), lambda i: (0, i))
    return pl.pallas_call(
        _router_kernel,
        grid=(n_tiles,),
        in_specs=[
            pl.BlockSpec((tm, d), lambda i: (i, 0)),
            pl.BlockSpec((N_EXPERTS, d), lambda i: (0, 0)),
            pl.BlockSpec((N_EXPERTS, 1), lambda i: (0, 0)),
        ],
        out_specs=[row, row, pl.BlockSpec((1, 1, LANE), lambda i: (i, 0, 0)),
                   pl.BlockSpec((N_EXPERTS, 1), lambda i: (0, 0))],
        out_shape=[
            jax.ShapeDtypeStruct((TOP_K, n), jnp.int32),
            jax.ShapeDtypeStruct((TOP_K, n), F32),
            jax.ShapeDtypeStruct((n_tiles, 1, LANE), jnp.int32),
            jax.ShapeDtypeStruct((N_EXPERTS, 1), jnp.int32),
        ],
        scratch_shapes=[pltpu.VMEM((N_EXPERTS, 1), F32)],
        compiler_params=_cparams(("arbitrary",)),
        name="router",
    )(x2, router_w.T, router_b.reshape(N_EXPERTS, 1))


def _group_copy(src, s, dst, d, sem, rows=ROW_ALIGN):
    return pltpu.make_async_copy(src.at[pl.ds(pl.multiple_of(s, ROW_ALIGN), rows), :],
                                 dst.at[pl.ds(pl.multiple_of(d, ROW_ALIGN), rows), :], sem)


def _for_each_group(tinfo_ref, base_ref, fn):
    def per_expert(e, _):
        seg_row = base_ref[e] + tinfo_ref[0, 0, e]
        run = tinfo_ref[0, 0, N_EXPERTS + e]
        sorted_row = tinfo_ref[0, 0, 2 * N_EXPERTS + e]
        n_big = run // BIG_GROUP

        def big(j, _):
            fn(sorted_row + j * BIG_GROUP, seg_row + j * BIG_GROUP, BIG_GROUP)
            return 0

        def small(j, _):
            fn(sorted_row + n_big * BIG_GROUP + j * ROW_ALIGN, seg_row + n_big * BIG_GROUP + j * ROW_ALIGN, ROW_ALIGN)
            return 0

        lax.fori_loop(0, n_big, big, 0)
        return lax.fori_loop(0, (run - n_big * BIG_GROUP) // ROW_ALIGN, small, 0)
    lax.fori_loop(0, N_EXPERTS, per_expert, 0)


def _wait_groups(n_small, n_big, src, dst, sem):
    for count, rows in ((n_big, BIG_GROUP), (n_small, ROW_ALIGN)):
        def one(j, _, rows=rows):
            _group_copy(src, 0, dst, 0, sem, rows).wait()
            return 0
        lax.fori_loop(0, count, one, 0)


def _dispatch_kernel(base_ref, seg_ref, tinfo_ref, spos_ref, gate_ref, x_ref, xs_ref, hot_ref, sort2_s, zero_s, pending_s,
                     sems):
    tb, d = x_ref.shape
    s_rows = sort2_s.shape[1]
    step = pl.program_id(0)
    sort_s, sem = sort2_s.at[step % 2], sems.at[step % 2]
    prev_sem = sems.at[(step + 1) % 2]

    @pl.when(step == 0)
    def _():
        zero_s[...] = jnp.zeros(zero_s.shape, F32)
        for fill in (True, False):
            def per_expert(e, _):
                def per_group(r, _):
                    cp = _group_copy(zero_s, 0, xs_ref, r * ROW_ALIGN, sem)
                    cp.start() if fill else cp.wait()
                    return 0
                first = jnp.where(e < N_EXPERTS, base_ref[jnp.minimum(e, N_EXPERTS - 1)] + seg_ref[jnp.minimum(e, N_EXPERTS - 1)],
                                  base_ref[N_EXPERTS])
                last = jnp.where(e < N_EXPERTS, base_ref[jnp.minimum(e + 1, N_EXPERTS)], xs_ref.shape[0])
                return lax.fori_loop(first // ROW_ALIGN, last // ROW_ALIGN, per_group, 0)
            lax.fori_loop(0, N_EXPERTS + 1, per_expert, 0)

    xb = x_ref[...].astype(BF16)
    spos = spos_ref[...]
    gate = gate_ref[...]
    for r0 in range(0, s_rows, SORT_ROWS):
        r_iota = r0 + lax.broadcasted_iota(jnp.int32, (SORT_ROWS, tb), 0)
        hits = [spos[k:k + 1, :] == r_iota for k in range(TOP_K)]
        onehot = jnp.where(hits[0] | hits[1] | hits[2] | hits[3], 1.0, 0.0).astype(BF16)
        hot_ref[0, r0:r0 + SORT_ROWS, :] = onehot
        sort_s[r0:r0 + SORT_ROWS, :d] = _dot(onehot, xb)
        g_sel = jnp.where(hits[0], gate[0:1, :], 0.0)
        for k in range(1, TOP_K):
            g_sel = g_sel + jnp.where(hits[k], gate[k:k + 1, :], 0.0)
        sort_s[r0:r0 + SORT_ROWS, d:] = jnp.broadcast_to(jnp.sum(g_sel, axis=1, keepdims=True), (SORT_ROWS, LANE))

    _for_each_group(tinfo_ref, base_ref, lambda s, r, rows: _group_copy(sort_s, s, xs_ref, r, sem, rows).start())
    n_small, n_big = tinfo_ref[0, 0, 3 * N_EXPERTS], tinfo_ref[0, 0, 3 * N_EXPERTS + 1]

    @pl.when(step > 0)
    def _():
        _wait_groups(pending_s[0], pending_s[1], sort_s, xs_ref, prev_sem)

    pending_s[0] = n_small
    pending_s[1] = n_big

    @pl.when(step == pl.num_programs(0) - 1)
    def _():
        _wait_groups(n_small, n_big, sort_s, xs_ref, sem)


def _tile_info_spec(n_tiles, ahead=0):
    return pl.BlockSpec((1, 1, LANE), lambda i, *_: (jnp.minimum(i + ahead, n_tiles - 1), 0, 0),
                        memory_space=pltpu.SMEM)


def _dispatch(x2, spos, gate, tinfo, base, seg, rows):
    n, d = x2.shape
    tb = min(MOE_TILE, n)
    row = pl.BlockSpec((TOP_K, tb), lambda i, *_: (0, i))
    return pl.pallas_call(
        _dispatch_kernel,
        grid_spec=pltpu.PrefetchScalarGridSpec(
            num_scalar_prefetch=2,
            grid=(n // tb,),
            in_specs=[_tile_info_spec(n // tb), row, row, pl.BlockSpec((tb, d), lambda i, *_: (i, 0))],
            out_specs=[pl.BlockSpec(memory_space=pl.ANY),
                       pl.BlockSpec((1, _sorted_rows(tb), tb), lambda i, *_: (i, 0, 0))],
            scratch_shapes=[pltpu.VMEM((2, _sorted_rows(tb), d + LANE), F32), pltpu.VMEM((ROW_ALIGN, d + LANE), F32),
                            pltpu.SMEM((2,), jnp.int32), pltpu.SemaphoreType.DMA((2,))],
        ),
        out_shape=[jax.ShapeDtypeStruct((rows, d + LANE), F32),
                   jax.ShapeDtypeStruct((n // tb, _sorted_rows(tb), tb), BF16)],
        compiler_params=_cparams(("arbitrary",)),
        name="moe_dispatch",
    )(base, seg, tinfo, spos, gate, x2)


def _expert_kernel(be_ref, nused_ref, x_ref, wgu_ref, bgu_ref, wd_ref, bd_ref, y_ref, wgu_s, wd_s):
    i = pl.program_id(0)
    used = i < nused_ref[0]

    @pl.when(used & ((i == 0) | (be_ref[i] != be_ref[jnp.maximum(i - 1, 0)])))
    def _():
        wgu_s[...] = wgu_ref[0, 0].astype(BF16)
        wd_s[...] = wd_ref[0, 0].astype(BF16)

    @pl.when(used)
    def _():
        d = y_ref.shape[1]
        xb = x_ref[:, :d].astype(BF16)
        h = _dot(xb, wgu_s[...]) + bgu_ref[0, 0]
        h_gate = jnp.minimum(h[:, :D_FF], SWIGLU_LIMIT)
        h_up = jnp.clip(h[:, D_FF:], -SWIGLU_LIMIT, SWIGLU_LIMIT)
        act = (h_up + 1.0) * h_gate * jax.nn.sigmoid(SWIGLU_ALPHA * h_gate)
        y_ref[...] = (_dot(act.astype(BF16), wd_s[...]) + bd_ref[0, 0]) * x_ref[:, d:d + 1]

    @pl.when(jnp.logical_not(used))
    def _():
        y_ref[...] = jnp.zeros(y_ref.shape, F32)


def _experts(xs, block_expert, n_used, layer, w_gu, b_gu, w_down, b_down):
    rows = xs.shape[0]
    d = D_MODEL
    n_blocks = rows // MOE_BLOCK
    f2 = w_gu.shape[3]
    depth = w_gu.shape[0]
    xmap = lambda i, be, nu: (jnp.minimum(i, nu[0] - 1), 0)
    return pl.pallas_call(
        _expert_kernel,
        grid_spec=pltpu.PrefetchScalarGridSpec(
            num_scalar_prefetch=2,
            grid=(n_blocks,),
            in_specs=[
                pl.BlockSpec((MOE_BLOCK, d + LANE), xmap),
                pl.BlockSpec((1, 1, d, f2), lambda i, be, nu: (layer, be[i], 0, 0)),
                pl.BlockSpec((1, 1, 1, f2), lambda i, be, nu: (layer, be[i], 0, 0)),
                pl.BlockSpec((1, 1, D_FF, d), lambda i, be, nu: (layer, be[i], 0, 0)),
                pl.BlockSpec((1, 1, 1, d), lambda i, be, nu: (layer, be[i], 0, 0)),
            ],
            out_specs=pl.BlockSpec((MOE_BLOCK, d), lambda i, be, nu: (i, 0)),
            scratch_shapes=[pltpu.VMEM((d, f2), BF16), pltpu.VMEM((D_FF, d), BF16)],
        ),
        out_shape=jax.ShapeDtypeStruct((rows, d), F32),
        compiler_params=_cparams(("arbitrary",)),
        name="moe_experts",
    )(block_expert, n_used, xs, w_gu, b_gu.reshape(depth, N_EXPERTS, 1, f2), w_down,
      b_down.reshape(depth, N_EXPERTS, 1, d))


def _combine_kernel(base_ref, tinfo_ref, tnext_ref, hot_ref, x_ref, g_ref, b_ref, y_ref, o_ref, ybuf2_s, sems):
    s_rows = ybuf2_s.shape[1]
    step = pl.program_id(0)
    ybuf_s, sem = ybuf2_s.at[step % 2], sems.at[step % 2]
    ynext_s, next_sem = ybuf2_s.at[(step + 1) % 2], sems.at[(step + 1) % 2]

    @pl.when(step == 0)
    def _():
        ybuf2_s[...] = jnp.zeros(ybuf2_s.shape, F32)
        _for_each_group(tinfo_ref, base_ref, lambda s, r, rows: _group_copy(y_ref, r, ybuf_s, s, sem, rows).start())

    @pl.when(step + 1 < pl.num_programs(0))
    def _():
        _for_each_group(tnext_ref, base_ref,
                        lambda s, r, rows: _group_copy(y_ref, r, ynext_s, s, next_sem, rows).start())

    n_split = COMBINE_SPLIT if s_rows % (COMBINE_SPLIT * LANE) == 0 else 1
    width = s_rows // n_split
    _wait_groups(tinfo_ref[0, 0, 3 * N_EXPERTS], tinfo_ref[0, 0, 3 * N_EXPERTS + 1], y_ref, ybuf_s, sem)
    moe = None
    for c in range(n_split):
        rows = slice(c * width, (c + 1) * width)
        part = _dot_tn(hot_ref[0, rows, :], ybuf_s[rows, :].astype(BF16))
        moe = part if moe is None else moe + part
    o_ref[...] = _layer_norm(DEEPNORM_ALPHA * x_ref[...] + moe, g_ref[...], b_ref[...])


def _combine(y, x2, hot, tinfo, base, ln_g, ln_b):
    n, d = x2.shape
    tb = min(MOE_TILE, n)
    s_rows = _sorted_rows(tb)
    return pl.pallas_call(
        _combine_kernel,
        grid_spec=pltpu.PrefetchScalarGridSpec(
            num_scalar_prefetch=1,
            grid=(n // tb,),
            in_specs=[
                _tile_info_spec(n // tb), _tile_info_spec(n // tb, ahead=1),
                pl.BlockSpec((1, s_rows, tb), lambda i, *_: (i, 0, 0)),
                pl.BlockSpec((tb, d), lambda i, *_: (i, 0)),
                pl.BlockSpec((1, d), lambda i, *_: (0, 0)),
                pl.BlockSpec((1, d), lambda i, *_: (0, 0)),
                pl.BlockSpec(memory_space=pl.ANY),
            ],
            out_specs=pl.BlockSpec((tb, d), lambda i, *_: (i, 0)),
            scratch_shapes=[pltpu.VMEM((2, s_rows, d), F32), pltpu.SemaphoreType.DMA((2,))],
        ),
        out_shape=jax.ShapeDtypeStruct((n, d), F32),
        compiler_params=_cparams(("arbitrary",)),
        name="moe_combine",
    )(base, tinfo, tinfo, hot, x2, ln_g.reshape(1, d), ln_b.reshape(1, d), y)


def _sorted_rows(tb):
    return TOP_K * tb + N_EXPERTS * ROW_ALIGN


def _moe_ffn_ln(x2, layer, router_w, router_b, w_gu, b_gu, w_down, b_down, ln_g, ln_b):
    n, _ = x2.shape
    n_tiles = n // min(MOE_TILE, n)
    spos, gate, tinfo, seg = _router(x2, router_w, router_b)
    seg = seg.reshape(N_EXPERTS)
    max_rows = n * TOP_K + n_tiles * N_EXPERTS * (ROW_ALIGN - 1) + N_EXPERTS * (MOE_BLOCK - 1)
    n_blocks = (max_rows + MOE_BLOCK - 1) // MOE_BLOCK
    padded = (seg + MOE_BLOCK - 1) // MOE_BLOCK * MOE_BLOCK
    padded_end = jnp.cumsum(padded)
    base = jnp.concatenate([jnp.zeros((1,), jnp.int32), padded_end]).astype(jnp.int32)
    n_used = (padded_end[-1:] // MOE_BLOCK).astype(jnp.int32)
    block_start = jnp.arange(n_blocks, dtype=jnp.int32) * MOE_BLOCK
    block_expert = jnp.sum(block_start[:, None] >= padded_end[None, :], axis=1).astype(jnp.int32)
    last_expert = jnp.sum(padded_end[-1] - 1 >= padded_end).astype(jnp.int32)
    block_expert = jnp.minimum(block_expert, last_expert)
    xs, hot = _dispatch(x2, spos, gate, tinfo, base, seg, n_blocks * MOE_BLOCK)
    y = _experts(xs, block_expert, n_used, layer, w_gu, b_gu, w_down, b_down)
    return _combine(y, x2, hot, tinfo, base, ln_g, ln_b)


def kernel(x, ln_g, ln_b, nsa_w_in, nsa_cmp_pos, nsa_cmp_w1, nsa_cmp_b1, nsa_cmp_w2, nsa_cmp_b2, nsa_gate_b, nsa_w_out, ml_w_in, ml_conv_w, ml_conv_b, ml_gate_b, ml_norm_g, ml_w_out, hg_w_in, hg_lower, hg_norm_g, hg_w_out, router_w, router_b, moe_w_gu, moe_b_gu, moe_w_down, moe_b_down):
    bsz, t, d = x.shape
    x2 = x.reshape(bsz * t, d)
    for layer in range(DEPTH):
        kind, slot = layer % N_MIXERS, layer // N_MIXERS
        if kind == 0:
            x2 = _nsa_mixer(x2, bsz, t, nsa_w_in[slot], nsa_cmp_pos[slot], nsa_cmp_w1[slot], nsa_cmp_b1[slot],
                            nsa_cmp_w2[slot], nsa_cmp_b2[slot], nsa_gate_b[slot], nsa_w_out[slot],
                            ln_g[layer, 0], ln_b[layer, 0])
        elif kind == 1:
            x2 = _mlstm_mixer(x2, bsz, t, ml_w_in[slot], ml_conv_w[slot], ml_conv_b[slot], ml_gate_b[slot],
                              ml_norm_g[slot], ml_w_out[slot], ln_g[layer, 0], ln_b[layer, 0])
        else:
            x2 = _hgrn_mixer(x2, bsz, t, layer, hg_w_in[slot], hg_lower, hg_norm_g[slot], hg_w_out[slot],
                             ln_g[layer, 0], ln_b[layer, 0])
        x2 = _moe_ffn_ln(x2, layer, router_w[layer], router_b[layer], moe_w_gu, moe_b_gu, moe_w_down, moe_b_down,
                         ln_g[layer, 1], ln_b[layer, 1])
    return x2.reshape(bsz, t, d)
```

```python
import functools

import numpy as np
import jax
import jax.numpy as jnp
from jax import lax
from jax.experimental import pallas as pl
from jax.experimental.pallas import tpu as pltpu

F32 = jnp.float32
BF16 = jnp.bfloat16
HIGHEST = lax.Precision.HIGHEST

D_MODEL = 1024
DEPTH = 4
N_MIXERS = 3

NSA_HEADS = 16
NSA_KV_GROUPS = 4
NSA_HEAD_DIM = 64
NSA_HPG = 4
NSA_KV_WIDTH = 256
CMP_BLOCK = 32
CMP_STRIDE = 16
CMP_HIDDEN = 256
SEL_BLOCK = 64
SEL_TOPK = 8
WINDOW = 512

MLSTM_HEADS = 4
MLSTM_QK_DIM = 128
MLSTM_V_DIM = 256
MLSTM_CONV = 4

HGRN_HEADS = 8
HGRN_DIM = 128
HGRN_CHUNK = 32

N_EXPERTS = 32
TOP_K = 4
D_FF = 1024
SWIGLU_LIMIT = 7.0
SWIGLU_ALPHA = 1.702

LN_EPS = 1e-5
RMS_EPS = 1e-6
DEEPNORM_ALPHA = (2 * DEPTH) ** 0.25
NEG = -1e30
M_FLOOR = -1e20

LANE = 128
VMEM_LIMIT = 56 * 1024 * 1024

PROJ_TM = 512
MOE_TILE = 512
MOE_BLOCK = 512
ROW_ALIGN = 8
BIG_GROUP = 32
SORT_ROWS = 128
COMBINE_SPLIT = 3
NSA_TQ = 256
NSA_TK = 512
MLSTM_L = 512
HGRN_SUPER = 128


def _cparams(sem):
    return pltpu.CompilerParams(dimension_semantics=sem, vmem_limit_bytes=VMEM_LIMIT)


def _dot(a, b):
    return jnp.dot(a, b, preferred_element_type=F32)


def _dot_nt(a, b, precision=None):
    return lax.dot_general(a, b, (((1,), (1,)), ((), ())), precision=precision, preferred_element_type=F32)


def _dot_tn(a, b):
    return lax.dot_general(a, b, (((0,), (0,)), ((), ())), preferred_element_type=F32)


def _layer_norm(z, g, b):
    mu = jnp.mean(z, axis=-1, keepdims=True)
    zc = z - mu
    var = jnp.mean(zc * zc, axis=-1, keepdims=True)
    return zc * lax.rsqrt(var + LN_EPS) * g + b


def _proj_kernel(x_ref, *refs, n_w, n_t):
    w_refs = refs[:n_w]
    wt_refs = refs[n_w:n_w + n_t]
    o_refs = refs[n_w + n_t:2 * n_w + n_t]
    ot_refs = refs[2 * n_w + n_t:]
    x = x_ref[...]
    xb = x.astype(BF16)
    for w_ref, o_ref in zip(w_refs, o_refs):
        ncol = w_ref.shape[1]
        for c0 in range(0, ncol, 512):
            c1 = min(c0 + 512, ncol)
            o_ref[:, c0:c1] = _dot(xb, w_ref[:, c0:c1]).astype(o_ref.dtype)
    for wt_ref, o_ref in zip(wt_refs, ot_refs):
        o_ref[...] = _dot_nt(wt_ref[...], x, precision=HIGHEST)


def _proj(x2, weights, out_dtypes, weights_t=()):
    n, d = x2.shape
    tm = min(PROJ_TM, n)
    in_specs = [pl.BlockSpec((tm, d), lambda i: (i, 0))]
    in_specs += [pl.BlockSpec(w.shape, lambda i: (0, 0)) for w in weights]
    in_specs += [pl.BlockSpec(w.shape, lambda i: (0, 0)) for w in weights_t]
    out_shape = [jax.ShapeDtypeStruct((n, w.shape[1]), dt) for w, dt in zip(weights, out_dtypes)]
    out_shape += [jax.ShapeDtypeStruct((w.shape[0], n), F32) for w in weights_t]
    out_specs = [pl.BlockSpec((tm, w.shape[1]), lambda i: (i, 0)) for w in weights]
    out_specs += [pl.BlockSpec((w.shape[0], tm), lambda i: (0, i)) for w in weights_t]
    return pl.pallas_call(
        functools.partial(_proj_kernel, n_w=len(weights), n_t=len(weights_t)),
        grid=(n // tm,),
        in_specs=in_specs,
        out_specs=out_specs,
        out_shape=out_shape,
        compiler_params=_cparams(("parallel",)),
        name="proj",
    )(x2, *weights, *weights_t)


def _outproj_ln_kernel(h_ref, w_ref, x_ref, g_ref, b_ref, o_ref):
    y = _dot(h_ref[...], w_ref[...])
    o_ref[...] = _layer_norm(DEEPNORM_ALPHA * x_ref[...] + y, g_ref[...], b_ref[...])


def _outproj_ln(h2, w, x2, g, b):
    n, d = x2.shape
    tm = min(PROJ_TM, n)
    return pl.pallas_call(
        _outproj_ln_kernel,
        grid=(n // tm,),
        in_specs=[
            pl.BlockSpec((tm, d), lambda i: (i, 0)),
            pl.BlockSpec((d, d), lambda i: (0, 0)),
            pl.BlockSpec((tm, d), lambda i: (i, 0)),
            pl.BlockSpec((1, d), lambda i: (0, 0)),
            pl.BlockSpec((1, d), lambda i: (0, 0)),
        ],
        out_specs=pl.BlockSpec((tm, d), lambda i: (i, 0)),
        out_shape=jax.ShapeDtypeStruct((n, d), F32),
        compiler_params=_cparams(("parallel",)),
        name="outproj_ln",
    )(h2, w, x2, g.reshape(1, d), b.reshape(1, d))


def _gelu_tanh(x):
    return 0.5 * x * (1.0 + jnp.tanh(np.sqrt(2.0 / np.pi).astype(np.float32) * (x + 0.044715 * (x * x * x))))


def _nsa_compress_kernel(kv_ref, wcat_ref, posa_ref, posb_ref, b1_ref, w2_ref, b2_ref, o_ref):
    n_row = kv_ref.shape[1] // CMP_STRIDE
    half = 2 * CMP_HIDDEN
    acc_a = jnp.zeros((n_row, half), F32)
    acc_b = jnp.zeros((n_row, half), F32)
    for r in range(CMP_STRIDE):
        xr = kv_ref[0, pl.ds(r, n_row, stride=CMP_STRIDE), :]
        w = wcat_ref[r]
        acc_a = acc_a + _dot((xr + posa_ref[r]).astype(BF16), w[:, :half])
        acc_b = acc_b + _dot((xr + posb_ref[r]).astype(BF16), w[:, half:])
    hidden = acc_a + pltpu.roll(acc_b, n_row - 1, 0) + b1_ref[...]
    hidden = _gelu_tanh(hidden)
    o_ref[0, 0] = (_dot(hidden.astype(BF16), w2_ref[...]) + b2_ref[...]).astype(o_ref.dtype)


def _nsa_compress(kvc, wcat, posa, posb, b1cat, w2cat, b2cat):
    bsz, t, _ = kvc.shape
    n_row = t // CMP_STRIDE
    return pl.pallas_call(
        _nsa_compress_kernel,
        grid=(bsz, NSA_KV_GROUPS),
        in_specs=[
            pl.BlockSpec((1, t, LANE), lambda b, g: (b, 0, g)),
            pl.BlockSpec(wcat.shape, lambda b, g: (0, 0, 0)),
            pl.BlockSpec(posa.shape, lambda b, g: (0, 0, 0)),
            pl.BlockSpec(posb.shape, lambda b, g: (0, 0, 0)),
            pl.BlockSpec(b1cat.shape, lambda b, g: (0, 0)),
            pl.BlockSpec(w2cat.shape, lambda b, g: (0, 0)),
            pl.BlockSpec(b2cat.shape, lambda b, g: (0, 0)),
        ],
        out_specs=pl.BlockSpec((1, 1, n_row, LANE), lambda b, g: (b, g, 0, 0)),
        out_shape=jax.ShapeDtypeStruct((bsz, NSA_KV_GROUPS, n_row, LANE), BF16),
        compiler_params=_cparams(("parallel", "parallel")),
        name="nsa_compress",
    )(kvc, wcat, posa, posb, b1cat, w2cat, b2cat)


def _nsa_attn_kernel(q_ref, kvc_ref, kvs_ref, kvw_ref, gate_ref, gb_ref, c2s_ref, o_ref, *, tq, tk, n_cmp):
    def tile(qi, carry):
        _nsa_attn_tile(qi, q_ref, kvc_ref, kvs_ref, kvw_ref, gate_ref, gb_ref, c2s_ref, o_ref, tq=tq, tk=tk, n_cmp=n_cmp)
        return carry
    lax.fori_loop(0, q_ref.shape[1] // tq, tile, 0)


def _nsa_attn_tile(qi, q_ref, kvc_ref, kvs_ref, kvw_ref, gate_ref, gb_ref, c2s_ref, o_ref, *, tq, tk, n_cmp):
    hpg = NSA_HPG
    t0 = pl.multiple_of(qi * tq, tq)
    t_col = t0 + lax.broadcasted_iota(jnp.int32, (tq, 1), 0)
    lane = lax.broadcasted_iota(jnp.int32, (1, LANE), 1)
    ones_lane = (lane == 0).astype(BF16)

    def with_ones(kv):
        return jnp.where(lane < NSA_HEAD_DIM, ones_lane, kv)

    gates = jax.nn.sigmoid(gate_ref[0, pl.ds(t0, tq), :].astype(F32) + gb_ref[0])
    gate_of = [[jnp.broadcast_to(gates[:, 3 * h + j:3 * h + j + 1], (tq, LANE)) for j in range(3)] for h in range(hpg)]

    span = WINDOW + tq
    w0 = pl.multiple_of(jnp.maximum(t0 - WINDOW, 0), tq)
    kvw = kvw_ref[0, pl.ds(w0, span), :]
    kvw1 = with_ones(kvw)
    kvc = kvc_ref[0, 0]
    n_c = kvc.shape[0]
    q_heads = []
    for h in range(hpg):
        pair_blk = q_ref[0, pl.ds(t0, tq), (h // 2) * LANE:(h // 2 + 1) * LANE].astype(F32)
        if h % 2:
            pair_blk = pltpu.roll(pair_blk, NSA_HEAD_DIM, 1)
        q_heads.append((jnp.where(lane < NSA_HEAD_DIM, pair_blk, 0.0) * NSA_HEAD_DIM ** -0.5).astype(BF16))
    s_c = [_dot_nt(q_heads[h], kvc) for h in range(hpg)]
    s_w = [_dot_nt(q_heads[h], kvw).astype(BF16) for h in range(hpg)]

    c_lane = lax.broadcasted_iota(jnp.int32, (1, n_c), 1)
    valid_c = (c_lane * CMP_STRIDE + (CMP_BLOCK - 1) <= t_col) & (c_lane < n_cmp)
    p_heads = []
    for h in range(hpg):
        sm = jnp.where(valid_c, s_c[h], NEG)
        m = jnp.max(sm, axis=-1, keepdims=True)
        p = jnp.where(valid_c, jnp.exp(sm - m), 0.0)
        l = jnp.sum(p, axis=-1, keepdims=True)
        p_heads.append(p / jnp.maximum(l, 1e-30))
    p_sum = p_heads[0]
    for h in range(1, hpg):
        p_sum = p_sum + p_heads[h]
    imp = _dot_nt(c2s_ref[...], p_sum, precision=HIGHEST)
    o_c = [_dot(p_heads[h].astype(BF16), kvc) for h in range(hpg)]

    lag = t_col - (w0 + lax.broadcasted_iota(jnp.int32, (1, span), 1))
    bias_w = jnp.where((lag >= 0) & (lag < WINDOW), 0.0, NEG).astype(BF16)
    o_w = []
    for h in range(hpg):
        sm = s_w[h] + bias_w
        o_w.append(_dot(jnp.exp(sm - jnp.max(sm, axis=-1, keepdims=True)), kvw1))
    partial = [gate_of[h][0] * o_c[h] + gate_of[h][2] * (o_w[h] / o_w[h][:, 0:1]) for h in range(hpg)]

    n_sel = imp.shape[0]
    blk = lax.broadcasted_iota(jnp.int32, (n_sel, 1), 0)
    cur = (t0 + lax.broadcasted_iota(jnp.int32, (1, tq), 1)) // SEL_BLOCK
    forced = (blk == 0) | (blk == cur) | (blk == cur - 1)
    score = jnp.where(forced, -NEG, jnp.where(blk <= cur, imp, NEG))
    rank = jnp.zeros((n_sel, tq), F32)
    for i in range(n_sel):
        s_i = score[i:i + 1, :]
        ahead = (s_i > score) | ((s_i == score) & (blk > i))
        rank = rank + ahead.astype(F32)
    member = ((rank < SEL_TOPK) & (score > 0.5 * NEG)).astype(BF16)

    blk_per_tile = tk // SEL_BLOCK
    key_lane = lax.broadcasted_iota(jnp.int32, (1, tk), 1)

    def sel_body(kt, carry):
        k0 = pl.multiple_of(kt * tk, tk)
        kv = kvs_ref[0, pl.ds(k0, tk), :]
        kv1 = with_ones(kv)
        expand = (key_lane // SEL_BLOCK + kt * blk_per_tile == blk).astype(BF16)
        valid = (_dot_tn(member, expand) > 0.5) & (k0 + key_lane <= t_col)
        bias = jnp.where(valid, 0.0, NEG).astype(BF16)
        s = [_dot_nt(q_heads[h], kv).astype(BF16) for h in range(hpg)]
        new = []
        for h in range(hpg):
            m_old, acc_old = carry[h]
            sm = s[h] + bias
            m_new = jnp.maximum(m_old, jnp.max(sm, axis=-1, keepdims=True).astype(F32))
            pv = _dot(jnp.exp(sm - m_new.astype(BF16)), kv1)
            new.append((m_new, jnp.exp(m_old - m_new) * acc_old + pv))
        return tuple(new)

    init = tuple((jnp.full((tq, 1), M_FLOOR, F32), jnp.zeros((tq, LANE), F32)) for _ in range(hpg))
    n_kt = (t0 + tq - 1) // tk + 1
    sel = lax.fori_loop(0, n_kt, sel_body, init)

    outs = []
    for h in range(hpg):
        o_s = sel[h][1]
        outs.append(partial[h] + gate_of[h][1] * (o_s / o_s[:, 0:1]))
    for j in range(hpg // 2):
        pair = jnp.where(lane < NSA_HEAD_DIM, pltpu.roll(outs[2 * j], NSA_HEAD_DIM, 1), outs[2 * j + 1])
        o_ref[0, pl.ds(t0, tq), j * LANE:(j + 1) * LANE] = pair.astype(o_ref.dtype)


def _nsa_attention(main, kvcmp, gate_b, c2s, bsz, t):
    tq = min(NSA_TQ, t)
    tk = min(NSA_TK, t)
    n_cmp = (t - CMP_BLOCK) // CMP_STRIDE + 1
    q_blocks = NSA_HEADS * NSA_HEAD_DIM // LANE
    g = NSA_KV_GROUPS
    kernel = functools.partial(_nsa_attn_kernel, tq=tq, tk=tk, n_cmp=n_cmp)
    return pl.pallas_call(
        kernel,
        grid=(bsz, g),
        in_specs=[
            pl.BlockSpec((1, t, NSA_HPG * NSA_HEAD_DIM), lambda b, gi: (b, 0, gi)),
            pl.BlockSpec((1, 1) + kvcmp.shape[2:], lambda b, gi: (b, gi, 0, 0)),
            pl.BlockSpec((1, t, LANE), lambda b, gi: (b, 0, q_blocks + gi)),
            pl.BlockSpec((1, t, LANE), lambda b, gi: (b, 0, q_blocks + g + gi)),
            pl.BlockSpec((1, t, LANE), lambda b, gi: (b, 0, q_blocks + 2 * g + gi)),
            pl.BlockSpec((1, 1, LANE), lambda b, gi: (gi, 0, 0)),
            pl.BlockSpec(c2s.shape, lambda b, gi: (0, 0)),
        ],
        out_specs=pl.BlockSpec((1, t, NSA_HPG * NSA_HEAD_DIM), lambda b, gi: (b, 0, gi)),
        out_shape=jax.ShapeDtypeStruct((bsz, t, D_MODEL), BF16),
        compiler_params=_cparams(("parallel", "parallel")),
        name="nsa_attention",
    )(main, kvcmp, main, main, main, gate_b, c2s)


def _nsa_mixer(x2, bsz, t, w_in, cmp_pos, cmp_w1, cmp_b1, cmp_w2, cmp_b2, gate_b, w_out, ln_g, ln_b):
    d, g, dh, hpg = D_MODEL, NSA_KV_GROUPS, NSA_HEAD_DIM, NSA_HPG
    kvw_ = NSA_KV_WIDTH

    def pair(k0):
        k = w_in[:, k0:k0 + kvw_].reshape(d, g, 1, dh)
        v = w_in[:, k0 + kvw_:k0 + 2 * kvw_].reshape(d, g, 1, dh)
        return jnp.concatenate([k, v], axis=2).reshape(d, g * LANE)

    wq = w_in[:, :d]
    n_gate = 3 * hpg
    wg = jnp.pad(w_in[:, d + 6 * kvw_:].reshape(d, g, n_gate), ((0, 0), (0, 0), (0, LANE - n_gate))).reshape(d, g * LANE)
    w_main = jnp.concatenate([wq, pair(d + 2 * kvw_), pair(d + 4 * kvw_), wg], axis=1).astype(BF16)
    w_kvc = pair(d).astype(BF16)
    main, kvc = _proj(x2, [w_main, w_kvc], [BF16, F32])

    half = CMP_BLOCK // 2
    w1 = cmp_w1.reshape(2, CMP_BLOCK, dh, CMP_HIDDEN)
    z = jnp.zeros((half, dh, CMP_HIDDEN), F32)
    top = jnp.concatenate([w1[0, :half], z, w1[0, half:], z], axis=2)
    bot = jnp.concatenate([z, w1[1, :half], z, w1[1, half:]], axis=2)
    wcat = jnp.concatenate([top, bot], axis=1).astype(BF16)
    pos = jnp.concatenate([cmp_pos[0], cmp_pos[1]], axis=-1)
    posa, posb = pos[:half, None, :], pos[half:, None, :]
    b1cat = cmp_b1.reshape(1, 2 * CMP_HIDDEN)
    zz = jnp.zeros((CMP_HIDDEN, dh), F32)
    w2cat = jnp.concatenate([jnp.concatenate([cmp_w2[0], zz], axis=1),
                             jnp.concatenate([zz, cmp_w2[1]], axis=1)], axis=0).astype(BF16)
    b2cat = cmp_b2.reshape(1, 2 * dh)
    kvcmp = _nsa_compress(kvc.reshape(bsz, t, g * LANE), wcat, posa, posb, b1cat, w2cat, b2cat)

    n_row = t // CMP_STRIDE
    n_sel = t // SEL_BLOCK
    cmp_start = np.arange(n_row) * CMP_STRIDE
    sel_start = np.arange(n_sel) * SEL_BLOCK
    overlap = (np.minimum(cmp_start[:, None] + CMP_BLOCK, sel_start[None, :] + SEL_BLOCK)
               - np.maximum(cmp_start[:, None], sel_start[None, :]))
    c2s = jnp.asarray((np.clip(overlap, 0, None) / CMP_STRIDE).astype(np.float32).T)
    gb = jnp.pad(gate_b.reshape(g, 1, n_gate), ((0, 0), (0, 0), (0, LANE - n_gate)))
    o = _nsa_attention(main.reshape(bsz, t, -1), kvcmp, gb, c2s, bsz, t)
    return _outproj_ln(o.reshape(bsz * t, d), w_out.astype(BF16), x2, ln_g, ln_b)


def _mlstm_kernel(q_ref, k_ref, v_ref, og_ref, gif_ref, gb_ref, cwq_ref, cwk_ref, cbq_ref, cbk_ref, ng_ref,
                  o_ref, pad_s, q_s, k_s, ct_s, *, chunk):
    t = q_ref.shape[1]
    dk, dv = MLSTM_QK_DIM, MLSTM_V_DIM
    head = pl.program_id(1)
    halo = 8

    def conv_silu(x_ref, w_ref, b_ref, dst, scale):
        pad_s[0:halo, :] = jnp.zeros((halo, dk), F32)
        pad_s[halo:halo + t, :] = x_ref[0]
        rows = min(t, 256)
        for r0 in range(0, t, rows):
            y = b_ref[...] + w_ref[0:1, :] * pad_s[pl.ds(halo + r0 - (MLSTM_CONV - 1), rows), :]
            for j in range(1, MLSTM_CONV):
                y = y + w_ref[j:j + 1, :] * pad_s[pl.ds(halo + r0 - (MLSTM_CONV - 1) + j, rows), :]
            y = y * jax.nn.sigmoid(y)
            dst[r0:r0 + rows, :] = (y * scale).astype(dst.dtype)

    conv_silu(q_ref, cwq_ref, cbq_ref, q_s, 1.0)
    conv_silu(k_ref, cwk_ref, cbk_ref, k_s, dk ** -0.5)
    ct_s[...] = jnp.zeros((dk, dv), F32)

    row = lax.broadcasted_iota(jnp.int32, (chunk, chunk), 0)
    col = lax.broadcasted_iota(jnp.int32, (chunk, chunk), 1)
    causal = col <= row
    eye = col == row
    upper = (row <= col).astype(F32)
    b_i = gb_ref[pl.ds(head, 1), :]
    b_f = gb_ref[pl.ds(MLSTM_HEADS + head, 1), :]
    norm_g = ng_ref[...]

    def body(c, carry):
        n_row, m_prev = carry
        r0 = pl.multiple_of(c * chunk, chunk)
        qc = q_s[pl.ds(r0, chunk), :]
        kc = k_s[pl.ds(r0, chunk), :]
        vc = v_ref[0, pl.ds(r0, chunk), :]
        li_row = gif_ref[0, c, pl.ds(head, 1), :] + b_i
        zf = gif_ref[0, c, pl.ds(MLSTM_HEADS + head, 1), :] + b_f
        lf_row = jnp.minimum(zf, 0.0) - jnp.log1p(jnp.exp(-jnp.abs(zf)))
        b_row = jnp.dot(lf_row, upper, precision=HIGHEST, preferred_element_type=F32)
        b_col = jnp.sum(jnp.where(causal, lf_row, 0.0), axis=-1, keepdims=True)
        li_col = jnp.sum(jnp.where(eye, li_row, 0.0), axis=-1, keepdims=True)
        d_mat = jnp.where(causal, b_col - b_row + li_row, NEG)
        m_t = jnp.maximum(b_col + m_prev, jnp.max(d_mat, axis=-1, keepdims=True))
        w_inter = jnp.exp(b_col + m_prev - m_t)
        s = _dot_nt(qc, kc) * jnp.exp(d_mat - m_t)
        num = _dot(s.astype(BF16), vc) + w_inter * _dot(qc, ct_s[...].astype(BF16))
        den = jnp.sum(s, axis=-1, keepdims=True) + w_inter * jnp.sum(qc.astype(F32) * n_row, axis=-1, keepdims=True)
        h = num / jnp.maximum(jnp.abs(den), jnp.exp(-m_t))
        b_end = b_col[chunk - 1:chunk, :]
        decay = b_end - b_col + li_col
        m_new = jnp.maximum(b_end + m_prev, jnp.max(decay, axis=0, keepdims=True))
        w_k = jnp.exp(decay - m_new)
        scale = jnp.exp(b_end + m_prev - m_new)
        ct_s[...] = scale * ct_s[...] + _dot_tn(kc, (vc.astype(F32) * w_k).astype(BF16))
        n_new = scale * n_row + jnp.sum(w_k * kc.astype(F32), axis=0, keepdims=True)
        hn = h * lax.rsqrt(jnp.mean(h * h, axis=-1, keepdims=True) + RMS_EPS) * norm_g
        og = og_ref[0, pl.ds(r0, chunk), :].astype(F32)
        o_ref[0, pl.ds(r0, chunk), :] = (hn * jax.nn.sigmoid(og)).astype(o_ref.dtype)
        return n_new, m_new

    lax.fori_loop(0, t // chunk, body, (jnp.zeros((1, dk), F32), jnp.zeros((1, 1), F32)))


def _mlstm_core(qk, v, og, gif, gate_b, conv_w, conv_b, norm_g, bsz, t, chunk):
    hh, dk, dv = MLSTM_HEADS, MLSTM_QK_DIM, MLSTM_V_DIM
    nc = t // chunk
    kernel = functools.partial(_mlstm_kernel, chunk=chunk)
    return pl.pallas_call(
        kernel,
        grid=(bsz, hh),
        in_specs=[
            pl.BlockSpec((1, t, dk), lambda b, h: (b, 0, h)),
            pl.BlockSpec((1, t, dk), lambda b, h: (b, 0, hh + h)),
            pl.BlockSpec((1, t, dv), lambda b, h: (b, 0, h)),
            pl.BlockSpec((1, t, dv), lambda b, h: (b, 0, h)),
            pl.BlockSpec((1, nc, 2 * hh, chunk), lambda b, h: (b, 0, 0, 0)),
            pl.BlockSpec((2 * hh, 1), lambda b, h: (0, 0)),
            pl.BlockSpec((MLSTM_CONV, dk), lambda b, h: (0, h)),
            pl.BlockSpec((MLSTM_CONV, dk), lambda b, h: (0, hh + h)),
            pl.BlockSpec((1, dk), lambda b, h: (0, h)),
            pl.BlockSpec((1, dk), lambda b, h: (0, hh + h)),
            pl.BlockSpec((1, dv), lambda b, h: (0, h)),
        ],
        out_specs=pl.BlockSpec((1, t, dv), lambda b, h: (b, 0, h)),
        out_shape=jax.ShapeDtypeStruct((bsz, t, hh * dv), BF16),
        scratch_shapes=[
            pltpu.VMEM((t + 8, dk), F32),
            pltpu.VMEM((t, dk), BF16),
            pltpu.VMEM((t, dk), BF16),
            pltpu.VMEM((dk, dv), F32),
        ],
        compiler_params=_cparams(("parallel", "parallel")),
        name="mlstm",
    )(qk, qk, v, og, gif, gate_b.reshape(2 * hh, 1), conv_w, conv_w, conv_b.reshape(1, -1), conv_b.reshape(1, -1),
      norm_g.reshape(1, -1))


def _mlstm_mixer(x2, bsz, t, w_in, conv_w, conv_b, gate_b, norm_g, w_out, ln_g, ln_b):
    d, hh = D_MODEL, MLSTM_HEADS
    qkw = 2 * hh * MLSTM_QK_DIM
    vw = hh * MLSTM_V_DIM
    chunk = min(MLSTM_L, t)
    w_qk = w_in[:, :qkw].astype(BF16)
    w_v = w_in[:, qkw:qkw + vw].astype(BF16)
    w_og = w_in[:, qkw + vw:qkw + vw + d].astype(BF16)
    w_gif_t = w_in[:, qkw + vw + d:].T
    qk, v, og, gif_t = _proj(x2, [w_qk, w_v, w_og], [F32, BF16, BF16], [w_gif_t])
    gif = gif_t.reshape(2 * hh, bsz, t // chunk, chunk).transpose(1, 2, 0, 3)
    h = _mlstm_core(qk.reshape(bsz, t, qkw), v.reshape(bsz, t, vw), og.reshape(bsz, t, d), gif, gate_b,
                    conv_w, conv_b, norm_g, bsz, t, chunk)
    return _outproj_ln(h.reshape(bsz * t, d), w_out.astype(BF16), x2, ln_g, ln_b)


def _hgrn_kernel(q_ref, f_ref, i_ref, g_ref, low_ref, ng_ref, o_ref, st_s, qd_s, kd_s, ke_s, dec_s, *,
                 layer_idx, heads):
    t = q_ref.shape[1]
    dh, chunk = HGRN_DIM, HGRN_CHUNK
    low = low_ref[...]
    e = jnp.exp(low - jnp.max(low, axis=0, keepdims=True))
    soft = e / jnp.sum(e, axis=0, keepdims=True)
    lb = jnp.zeros((1, heads * dh), F32)
    for r in range(1, layer_idx + 1):
        lb = lb + soft[r:r + 1, :]
    norm_g = ng_ref[...]
    st_s[...] = jnp.zeros(st_s.shape, F32)
    width = heads * dh
    sup = min(HGRN_SUPER, t)
    n_sub = sup // chunk

    in_chunk = lax.broadcasted_iota(jnp.int32, (sup, 1), 0) % chunk

    def prep(s, carry):
        r0 = pl.multiple_of(s * sup, sup)
        f = lb + (1.0 - lb) * jax.nn.sigmoid(f_ref[0, pl.ds(r0, sup), :])
        gcum = jnp.log(f)
        shift = 1
        while shift < chunk:
            gcum = gcum + jnp.where(in_chunk >= shift, pltpu.roll(gcum, shift, 0), 0.0)
            shift *= 2
        g_end = jnp.concatenate(
            [jnp.broadcast_to(gcum[(j + 1) * chunk - 1:(j + 1) * chunk, :], (chunk, width)) for j in range(n_sub)], axis=0)
        qv = q_ref[0, pl.ds(r0, sup), :].astype(F32)
        kk = 1.0 - f
        qd_s[pl.ds(r0, sup), :] = (qv * jax.nn.sigmoid(qv) * jnp.exp(gcum)).astype(BF16)
        kd_s[pl.ds(r0, sup), :] = (kk * jnp.exp(-gcum)).astype(BF16)
        ke_s[pl.ds(r0, sup), :] = (kk * jnp.exp(g_end - gcum)).astype(BF16)
        for j in range(n_sub):
            dec_s[s, j:j + 1, :] = jnp.exp(gcum[(j + 1) * chunk - 1:(j + 1) * chunk, :])
        return carry

    lax.fori_loop(0, t // sup, prep, 0)

    row = lax.broadcasted_iota(jnp.int32, (sup, sup), 0)
    col = lax.broadcasted_iota(jnp.int32, (sup, sup), 1)
    keep = (col <= row) & (col // chunk == row // chunk)

    def body(s, carry):
        r0 = pl.multiple_of(s * sup, sup)
        decay = dec_s[s]
        col_of = [slice(h * dh, (h + 1) * dh) for h in range(heads)]
        sub_of = [slice(j * chunk, (j + 1) * chunk) for j in range(n_sub)]
        q_dec = [qd_s[pl.ds(r0, sup), c] for c in col_of]
        vv = [i_ref[0, pl.ds(r0, sup), c] for c in col_of]
        score = [_dot_nt(q_dec[h], kd_s[pl.ds(r0, sup), col_of[h]]) for h in range(heads)]
        k_end = [ke_s[pl.ds(r0, sup), c] for c in col_of]
        kv = [[_dot_tn(vv[h][r], k_end[h][r]) for r in sub_of] for h in range(heads)]
        intra = [_dot(jnp.where(keep, score[h], 0.0).astype(BF16), vv[h]) for h in range(heads)]
        states = []
        for h in range(heads):
            st = st_s[h]
            per_sub = []
            for j in range(n_sub):
                per_sub.append(st.astype(BF16))
                st = st * decay[j:j + 1, col_of[h]] + kv[h][j]
            st_s[h] = st
            states.append(per_sub)
        for h in range(heads):
            inter = [_dot_nt(q_dec[h][sub_of[j]], states[h][j]) for j in range(n_sub)]
            o = intra[h] + jnp.concatenate(inter, axis=0)
            on = o * lax.rsqrt(jnp.mean(o * o, axis=-1, keepdims=True) + RMS_EPS) * norm_g[:, col_of[h]]
            gate = jax.nn.sigmoid(g_ref[0, pl.ds(r0, sup), col_of[h]].astype(F32))
            o_ref[0, pl.ds(r0, sup), col_of[h]] = (on * gate).astype(o_ref.dtype)
        return carry

    lax.fori_loop(0, t // sup, body, 0)


def _hgrn_core(q, f, i, g, lower, norm_g, bsz, t, layer_idx):
    heads = 4
    width = heads * HGRN_DIM
    n_grp = HGRN_HEADS // heads
    kernel = functools.partial(_hgrn_kernel, layer_idx=layer_idx, heads=heads)
    act = pl.BlockSpec((1, t, width), lambda b, j: (b, 0, j))
    return pl.pallas_call(
        kernel,
        grid=(bsz, n_grp),
        in_specs=[act, act, act, act,
                  pl.BlockSpec((DEPTH, width), lambda b, j: (0, j)),
                  pl.BlockSpec((1, width), lambda b, j: (0, j))],
        out_specs=act,
        out_shape=jax.ShapeDtypeStruct((bsz, t, D_MODEL), BF16),
        scratch_shapes=[pltpu.VMEM((heads, HGRN_DIM, HGRN_DIM), F32), pltpu.VMEM((t, width), BF16),
                        pltpu.VMEM((t, width), BF16), pltpu.VMEM((t, width), BF16),
                        pltpu.VMEM((max(t // HGRN_SUPER, 1), min(HGRN_SUPER, t) // HGRN_CHUNK, width), F32)],
        compiler_params=_cparams(("parallel", "parallel")),
        name="hgrn2",
    )(q, f, i, g, lower, norm_g.reshape(1, -1))


def _hgrn_mixer(x2, bsz, t, layer_idx, w_in, lower, norm_g, w_out, ln_g, ln_b):
    d = D_MODEL
    ws = [w_in[:, j * d:(j + 1) * d].astype(BF16) for j in range(4)]
    q, f, i, g = _proj(x2, ws, [BF16, F32, BF16, BF16])
    shp = (bsz, t, d)
    o = _hgrn_core(q.reshape(shp), f.reshape(shp), i.reshape(shp), g.reshape(shp), lower, norm_g, bsz, t, layer_idx)
    return _outproj_ln(o.reshape(bsz * t, d), w_out.astype(BF16), x2, ln_g, ln_b)


def _router_kernel(x_ref, rwt_ref, rb_ref, spos_ref, gate_ref, tinfo_ref, seg_ref, carry_s):
    @pl.when(pl.program_id(0) == 0)
    def _():
        carry_s[...] = jnp.zeros(carry_s.shape, F32)

    tm = x_ref.shape[0]
    ne = N_EXPERTS
    logits = _dot_nt(rwt_ref[...], x_ref[...], precision=HIGHEST) + rb_ref[...]
    e_iota = lax.broadcasted_iota(jnp.int32, (ne, tm), 0)
    work = logits
    vals, picks = [], []
    for k in range(TOP_K):
        mx = jnp.max(work, axis=0, keepdims=True)
        idx = jnp.min(jnp.where(work == mx, e_iota, ne), axis=0, keepdims=True)
        pick = e_iota == idx
        vals.append(mx)
        picks.append(pick)
        work = jnp.where(pick, -jnp.inf, work)
    exps = [jnp.exp(v - vals[0]) for v in vals]
    tot = exps[0]
    for k in range(1, TOP_K):
        tot = tot + exps[k]
    for k in range(TOP_K):
        gate_ref[k:k + 1, :] = exps[k] / tot
    hot = picks[0].astype(F32)
    for k in range(1, TOP_K):
        hot = hot + picks[k].astype(F32)
    before = (lax.broadcasted_iota(jnp.int32, (tm, tm), 0) < lax.broadcasted_iota(jnp.int32, (tm, tm), 1)).astype(BF16)
    prior = _dot(hot.astype(BF16), before)
    cnt = jnp.sum(hot, axis=1, keepdims=True)
    run = jnp.floor((cnt + (ROW_ALIGN - 1)) * (1.0 / ROW_ALIGN)) * ROW_ALIGN
    sub = lax.broadcasted_iota(jnp.int32, (ne, ne), 0)
    lan = lax.broadcasted_iota(jnp.int32, (ne, ne), 1)
    run_row = jnp.sum(jnp.where(sub == lan, run, 0.0), axis=0, keepdims=True)
    soff = jnp.sum(jnp.where(lan < sub, run_row, 0.0), axis=1, keepdims=True)
    for k in range(TOP_K):
        pos = jnp.sum(jnp.where(picks[k], prior + soff, 0.0), axis=0, keepdims=True)
        spos_ref[k:k + 1, :] = pos.astype(jnp.int32)
    n_big = jnp.floor(run * (1.0 / BIG_GROUP))
    sub = lax.broadcasted_iota(jnp.int32, (ne, LANE), 0)
    lan = lax.broadcasted_iota(jnp.int32, (ne, LANE), 1)
    field, le = lan // ne, lan % ne
    carry = carry_s[...]
    info = (jnp.where((field == 0) & (sub == le), carry, 0.0) + jnp.where((field == 1) & (sub == le), run, 0.0)
            + jnp.where((field == 2) & (sub < le), run, 0.0)
            + jnp.where((field == 3) & (le == 0), (run - n_big * BIG_GROUP) * (1.0 / ROW_ALIGN), 0.0)
            + jnp.where((field == 3) & (le == 1), n_big, 0.0))
    tinfo_ref[0] = jnp.sum(info, axis=0, keepdims=True).astype(jnp.int32)
    total = carry + run
    carry_s[...] = total
    seg_ref[...] = total.astype(jnp.int32)


def _router(x2, router_w, router_b):
    n, d = x2.shape
    tm = min(MOE_TILE, n)
    n_tiles = n // tm
    row = pl.BlockSpec((TOP_K, tm), lambda i: (0, i))
    return pl.pallas_call(
        _router_kernel,
        grid=(n_tiles,),
        in_specs=[
            pl.BlockSpec((tm, d), lambda i: (i, 0)),
            pl.BlockSpec((N_EXPERTS, d), lambda i: (0, 0)),
            pl.BlockSpec((N_EXPERTS, 1), lambda i: (0, 0)),
        ],
        out_specs=[row, row, pl.BlockSpec((1, 1, LANE), lambda i: (i, 0, 0)),
                   pl.BlockSpec((N_EXPERTS, 1), lambda i: (0, 0))],
        out_shape=[
            jax.ShapeDtypeStruct((TOP_K, n), jnp.int32),
            jax.ShapeDtypeStruct((TOP_K, n), F32),
            jax.ShapeDtypeStruct((n_tiles, 1, LANE), jnp.int32),
            jax.ShapeDtypeStruct((N_EXPERTS, 1), jnp.int32),
        ],
        scratch_shapes=[pltpu.VMEM((N_EXPERTS, 1), F32)],
        compiler_params=_cparams(("arbitrary",)),
        name="router",
    )(x2, router_w.T, router_b.reshape(N_EXPERTS, 1))


def _group_copy(src, s, dst, d, sem, rows=ROW_ALIGN):
    return pltpu.make_async_copy(src.at[pl.ds(pl.multiple_of(s, ROW_ALIGN), rows), :],
                                 dst.at[pl.ds(pl.multiple_of(d, ROW_ALIGN), rows), :], sem)


def _for_each_group(tinfo_ref, base_ref, fn):
    def per_expert(e, _):
        seg_row = base_ref[e] + tinfo_ref[0, 0, e]
        run = tinfo_ref[0, 0, N_EXPERTS + e]
        sorted_row = tinfo_ref[0, 0, 2 * N_EXPERTS + e]
        n_big = run // BIG_GROUP

        def big(j, _):
            fn(sorted_row + j * BIG_GROUP, seg_row + j * BIG_GROUP, BIG_GROUP)
            return 0

        def small(j, _):
            fn(sorted_row + n_big * BIG_GROUP + j * ROW_ALIGN, seg_row + n_big * BIG_GROUP + j * ROW_ALIGN, ROW_ALIGN)
            return 0

        lax.fori_loop(0, n_big, big, 0)
        return lax.fori_loop(0, (run - n_big * BIG_GROUP) // ROW_ALIGN, small, 0)
    lax.fori_loop(0, N_EXPERTS, per_expert, 0)


def _wait_groups(n_small, n_big, src, dst, sem):
    for count, rows in ((n_big, BIG_GROUP), (n_small, ROW_ALIGN)):
        def one(j, _, rows=rows):
            _group_copy(src, 0, dst, 0, sem, rows).wait()
            return 0
        lax.fori_loop(0, count, one, 0)


def _dispatch_kernel(base_ref, seg_ref, tinfo_ref, spos_ref, gate_ref, x_ref, xs_ref, hot_ref, sort2_s, zero_s, pending_s,
                     sems):
    tb, d = x_ref.shape
    s_rows = sort2_s.shape[1]
    step = pl.program_id(0)
    sort_s, sem = sort2_s.at[step % 2], sems.at[step % 2]
    prev_sem = sems.at[(step + 1) % 2]

    @pl.when(step == 0)
    def _():
        zero_s[...] = jnp.zeros(zero_s.shape, F32)
        for fill in (True, False):
            def per_expert(e, _):
                def per_group(r, _):
                    cp = _group_copy(zero_s, 0, xs_ref, r * ROW_ALIGN, sem)
                    cp.start() if fill else cp.wait()
                    return 0
                first = jnp.where(e < N_EXPERTS, base_ref[jnp.minimum(e, N_EXPERTS - 1)] + seg_ref[jnp.minimum(e, N_EXPERTS - 1)],
                                  base_ref[N_EXPERTS])
                last = jnp.where(e < N_EXPERTS, base_ref[jnp.minimum(e + 1, N_EXPERTS)], xs_ref.shape[0])
                return lax.fori_loop(first // ROW_ALIGN, last // ROW_ALIGN, per_group, 0)
            lax.fori_loop(0, N_EXPERTS + 1, per_expert, 0)

    xb = x_ref[...].astype(BF16)
    spos = spos_ref[...]
    gate = gate_ref[...]
    for r0 in range(0, s_rows, SORT_ROWS):
        r_iota = r0 + lax.broadcasted_iota(jnp.int32, (SORT_ROWS, tb), 0)
        hits = [spos[k:k + 1, :] == r_iota for k in range(TOP_K)]
        onehot = jnp.where(hits[0] | hits[1] | hits[2] | hits[3], 1.0, 0.0).astype(BF16)
        hot_ref[0, r0:r0 + SORT_ROWS, :] = onehot
        sort_s[r0:r0 + SORT_ROWS, :d] = _dot(onehot, xb)
        g_sel = jnp.zeros((SORT_ROWS, tb), F32)
        for k in range(TOP_K):
            g_sel = jnp.where(hits[k], gate[k:k + 1, :], g_sel)
        sort_s[r0:r0 + SORT_ROWS, d:] = jnp.broadcast_to(jnp.sum(g_sel, axis=1, keepdims=True), (SORT_ROWS, LANE))

    _for_each_group(tinfo_ref, base_ref, lambda s, r, rows: _group_copy(sort_s, s, xs_ref, r, sem, rows).start())
    n_small, n_big = tinfo_ref[0, 0, 3 * N_EXPERTS], tinfo_ref[0, 0, 3 * N_EXPERTS + 1]

    @pl.when(step > 0)
    def _():
        _wait_groups(pending_s[0], pending_s[1], sort_s, xs_ref, prev_sem)

    pending_s[0] = n_small
    pending_s[1] = n_big

    @pl.when(step == pl.num_programs(0) - 1)
    def _():
        _wait_groups(n_small, n_big, sort_s, xs_ref, sem)


def _tile_info_spec(n_tiles, ahead=0):
    return pl.BlockSpec((1, 1, LANE), lambda i, *_: (jnp.minimum(i + ahead, n_tiles - 1), 0, 0),
                        memory_space=pltpu.SMEM)


def _dispatch(x2, spos, gate, tinfo, base, seg, rows):
    n, d = x2.shape
    tb = min(MOE_TILE, n)
    row = pl.BlockSpec((TOP_K, tb), lambda i, *_: (0, i))
    return pl.pallas_call(
        _dispatch_kernel,
        grid_spec=pltpu.PrefetchScalarGridSpec(
            num_scalar_prefetch=2,
            grid=(n // tb,),
            in_specs=[_tile_info_spec(n // tb), row, row, pl.BlockSpec((tb, d), lambda i, *_: (i, 0))],
            out_specs=[pl.BlockSpec(memory_space=pl.ANY),
                       pl.BlockSpec((1, _sorted_rows(tb), tb), lambda i, *_: (i, 0, 0))],
            scratch_shapes=[pltpu.VMEM((2, _sorted_rows(tb), d + LANE), F32), pltpu.VMEM((ROW_ALIGN, d + LANE), F32),
                            pltpu.SMEM((2,), jnp.int32), pltpu.SemaphoreType.DMA((2,))],
        ),
        out_shape=[jax.ShapeDtypeStruct((rows, d + LANE), F32),
                   jax.ShapeDtypeStruct((n // tb, _sorted_rows(tb), tb), BF16)],
        compiler_params=_cparams(("arbitrary",)),
        name="moe_dispatch",
    )(base, seg, tinfo, spos, gate, x2)


def _expert_kernel(be_ref, nused_ref, x_ref, wgu_ref, bgu_ref, wd_ref, bd_ref, y_ref, wgu_s, wd_s):
    i = pl.program_id(0)
    used = i < nused_ref[0]

    @pl.when(used & ((i == 0) | (be_ref[i] != be_ref[jnp.maximum(i - 1, 0)])))
    def _():
        wgu_s[...] = wgu_ref[0, 0].astype(BF16)
        wd_s[...] = wd_ref[0, 0].astype(BF16)

    @pl.when(used)
    def _():
        d = y_ref.shape[1]
        xb = x_ref[:, :d].astype(BF16)
        h = _dot(xb, wgu_s[...]) + bgu_ref[0, 0]
        h_gate = jnp.minimum(h[:, :D_FF], SWIGLU_LIMIT)
        h_up = jnp.clip(h[:, D_FF:], -SWIGLU_LIMIT, SWIGLU_LIMIT)
        act = (h_up + 1.0) * h_gate * jax.nn.sigmoid(SWIGLU_ALPHA * h_gate)
        y_ref[...] = (_dot(act.astype(BF16), wd_s[...]) + bd_ref[0, 0]) * x_ref[:, d:d + 1]

    @pl.when(jnp.logical_not(used))
    def _():
        y_ref[...] = jnp.zeros(y_ref.shape, F32)


def _experts(xs, block_expert, n_used, layer, w_gu, b_gu, w_down, b_down):
    rows = xs.shape[0]
    d = D_MODEL
    n_blocks = rows // MOE_BLOCK
    f2 = w_gu.shape[3]
    depth = w_gu.shape[0]
    xmap = lambda i, be, nu: (jnp.minimum(i, nu[0] - 1), 0)
    return pl.pallas_call(
        _expert_kernel,
        grid_spec=pltpu.PrefetchScalarGridSpec(
            num_scalar_prefetch=2,
            grid=(n_blocks,),
            in_specs=[
                pl.BlockSpec((MOE_BLOCK, d + LANE), xmap),
                pl.BlockSpec((1, 1, d, f2), lambda i, be, nu: (layer, be[i], 0, 0)),
                pl.BlockSpec((1, 1, 1, f2), lambda i, be, nu: (layer, be[i], 0, 0)),
                pl.BlockSpec((1, 1, D_FF, d), lambda i, be, nu: (layer, be[i], 0, 0)),
                pl.BlockSpec((1, 1, 1, d), lambda i, be, nu: (layer, be[i], 0, 0)),
            ],
            out_specs=pl.BlockSpec((MOE_BLOCK, d), lambda i, be, nu: (i, 0)),
            scratch_shapes=[pltpu.VMEM((d, f2), BF16), pltpu.VMEM((D_FF, d), BF16)],
        ),
        out_shape=jax.ShapeDtypeStruct((rows, d), F32),
        compiler_params=_cparams(("arbitrary",)),
        name="moe_experts",
    )(block_expert, n_used, xs, w_gu, b_gu.reshape(depth, N_EXPERTS, 1, f2), w_down,
      b_down.reshape(depth, N_EXPERTS, 1, d))


def _combine_kernel(base_ref, tinfo_ref, tnext_ref, hot_ref, x_ref, g_ref, b_ref, y_ref, o_ref, ybuf2_s, sems):
    s_rows = ybuf2_s.shape[1]
    step = pl.program_id(0)
    ybuf_s, sem = ybuf2_s.at[step % 2], sems.at[step % 2]
    ynext_s, next_sem = ybuf2_s.at[(step + 1) % 2], sems.at[(step + 1) % 2]

    @pl.when(step == 0)
    def _():
        ybuf2_s[...] = jnp.zeros(ybuf2_s.shape, F32)
        _for_each_group(tinfo_ref, base_ref, lambda s, r, rows: _group_copy(y_ref, r, ybuf_s, s, sem, rows).start())

    @pl.when(step + 1 < pl.num_programs(0))
    def _():
        _for_each_group(tnext_ref, base_ref,
                        lambda s, r, rows: _group_copy(y_ref, r, ynext_s, s, next_sem, rows).start())

    n_split = COMBINE_SPLIT if s_rows % (COMBINE_SPLIT * LANE) == 0 else 1
    width = s_rows // n_split
    _wait_groups(tinfo_ref[0, 0, 3 * N_EXPERTS], tinfo_ref[0, 0, 3 * N_EXPERTS + 1], y_ref, ybuf_s, sem)
    moe = None
    for c in range(n_split):
        rows = slice(c * width, (c + 1) * width)
        part = _dot_tn(hot_ref[0, rows, :], ybuf_s[rows, :].astype(BF16))
        moe = part if moe is None else moe + part
    o_ref[...] = _layer_norm(DEEPNORM_ALPHA * x_ref[...] + moe, g_ref[...], b_ref[...])


def _combine(y, x2, hot, tinfo, base, ln_g, ln_b):
    n, d = x2.shape
    tb = min(MOE_TILE, n)
    s_rows = _sorted_rows(tb)
    return pl.pallas_call(
        _combine_kernel,
        grid_spec=pltpu.PrefetchScalarGridSpec(
            num_scalar_prefetch=1,
            grid=(n // tb,),
            in_specs=[
                _tile_info_spec(n // tb), _tile_info_spec(n // tb, ahead=1),
                pl.BlockSpec((1, s_rows, tb), lambda i, *_: (i, 0, 0)),
                pl.BlockSpec((tb, d), lambda i, *_: (i, 0)),
                pl.BlockSpec((1, d), lambda i, *_: (0, 0)),
                pl.BlockSpec((1, d), lambda i, *_: (0, 0)),
                pl.BlockSpec(memory_space=pl.ANY),
            ],
            out_specs=pl.BlockSpec((tb, d), lambda i, *_: (i, 0)),
            scratch_shapes=[pltpu.VMEM((2, s_rows, d), F32), pltpu.SemaphoreType.DMA((2,))],
        ),
        out_shape=jax.ShapeDtypeStruct((n, d), F32),
        compiler_params=_cparams(("arbitrary",)),
        name="moe_combine",
    )(base, tinfo, tinfo, hot, x2, ln_g.reshape(1, d), ln_b.reshape(1, d), y)


def _sorted_rows(tb):
    return TOP_K * tb + N_EXPERTS * ROW_ALIGN


def _moe_ffn_ln(x2, layer, router_w, router_b, w_gu, b_gu, w_down, b_down, ln_g, ln_b):
    n, _ = x2.shape
    n_tiles = n // min(MOE_TILE, n)
    spos, gate, tinfo, seg = _router(x2, router_w, router_b)
    seg = seg.reshape(N_EXPERTS)
    max_rows = n * TOP_K + n_tiles * N_EXPERTS * (ROW_ALIGN - 1) + N_EXPERTS * (MOE_BLOCK - 1)
    n_blocks = (max_rows + MOE_BLOCK - 1) // MOE_BLOCK
    padded = (seg + MOE_BLOCK - 1) // MOE_BLOCK * MOE_BLOCK
    padded_end = jnp.cumsum(padded)
    base = jnp.concatenate([jnp.zeros((1,), jnp.int32), padded_end]).astype(jnp.int32)
    n_used = (padded_end[-1:] // MOE_BLOCK).astype(jnp.int32)
    block_start = jnp.arange(n_blocks, dtype=jnp.int32) * MOE_BLOCK
    block_expert = jnp.sum(block_start[:, None] >= padded_end[None, :], axis=1).astype(jnp.int32)
    last_expert = jnp.sum(padded_end[-1] - 1 >= padded_end).astype(jnp.int32)
    block_expert = jnp.minimum(block_expert, last_expert)
    xs, hot = _dispatch(x2, spos, gate, tinfo, base, seg, n_blocks * MOE_BLOCK)
    y = _experts(xs, block_expert, n_used, layer, w_gu, b_gu, w_down, b_down)
    return _combine(y, x2, hot, tinfo, base, ln_g, ln_b)


def kernel(x, ln_g, ln_b, nsa_w_in, nsa_cmp_pos, nsa_cmp_w1, nsa_cmp_b1, nsa_cmp_w2, nsa_cmp_b2, nsa_gate_b, nsa_w_out, ml_w_in, ml_conv_w, ml_conv_b, ml_gate_b, ml_norm_g, ml_w_out, hg_w_in, hg_lower, hg_norm_g, hg_w_out, router_w, router_b, moe_w_gu, moe_b_gu, moe_w_down, moe_b_down):
    bsz, t, d = x.shape
    x2 = x.reshape(bsz * t, d)
    for layer in range(DEPTH):
        kind, slot = layer % N_MIXERS, layer // N_MIXERS
        if kind == 0:
            x2 = _nsa_mixer(x2, bsz, t, nsa_w_in[slot], nsa_cmp_pos[slot], nsa_cmp_w1[slot], nsa_cmp_b1[slot],
                            nsa_cmp_w2[slot], nsa_cmp_b2[slot], nsa_gate_b[slot], nsa_w_out[slot],
                            ln_g[layer, 0], ln_b[layer, 0])
        elif kind == 1:
            x2 = _mlstm_mixer(x2, bsz, t, ml_w_in[slot], ml_conv_w[slot], ml_conv_b[slot], ml_gate_b[slot],
                              ml_norm_g[slot], ml_w_out[slot], ln_g[layer, 0], ln_b[layer, 0])
        else:
            x2 = _hgrn_mixer(x2, bsz, t, layer, hg_w_in[slot], hg_lower, hg_norm_g[slot], hg_w_out[slot],
                             ln_g[layer, 0], ln_b[layer, 0])
        x2 = _moe_ffn_ln(x2, layer, router_w[layer], router_b[layer], moe_w_gu, moe_b_gu, moe_w_down, moe_b_down,
                         ln_g[layer, 1], ln_b[layer, 1])
    return x2.reshape(bsz, t, d)
```

```python
import functools

import numpy as np
import jax
import jax.numpy as jnp
from jax import lax
from jax.experimental import pallas as pl
from jax.experimental.pallas import tpu as pltpu

F32 = jnp.float32
BF16 = jnp.bfloat16
HIGHEST = lax.Precision.HIGHEST

D_MODEL = 1024
DEPTH = 4
N_MIXERS = 3

NSA_HEADS = 16
NSA_KV_GROUPS = 4
NSA_HEAD_DIM = 64
NSA_HPG = 4
NSA_KV_WIDTH = 256
CMP_BLOCK = 32
CMP_STRIDE = 16
CMP_HIDDEN = 256
SEL_BLOCK = 64
SEL_TOPK = 8
WINDOW = 512

MLSTM_HEADS = 4
MLSTM_QK_DIM = 128
MLSTM_V_DIM = 256
MLSTM_CONV = 4

HGRN_HEADS = 8
HGRN_DIM = 128
HGRN_CHUNK = 32

N_EXPERTS = 32
TOP_K = 4
D_FF = 1024
SWIGLU_LIMIT = 7.0
SWIGLU_ALPHA = 1.702

LN_EPS = 1e-5
RMS_EPS = 1e-6
DEEPNORM_ALPHA = (2 * DEPTH) ** 0.25
NEG = -1e30
M_FLOOR = -1e20

LANE = 128
VMEM_LIMIT = 56 * 1024 * 1024

PROJ_TM = 512
MOE_TILE = 512
MOE_BLOCK = 512
ROW_ALIGN = 8
BIG_GROUP = 32
SORT_ROWS = 128
COMBINE_SPLIT = 3
NSA_TQ = 256
NSA_TK = 512
MLSTM_L = 512
HGRN_SUPER = 128


def _cparams(sem):
    return pltpu.CompilerParams(dimension_semantics=sem, vmem_limit_bytes=VMEM_LIMIT)


def _dot(a, b):
    return jnp.dot(a, b, preferred_element_type=F32)


def _dot_nt(a, b, precision=None):
    return lax.dot_general(a, b, (((1,), (1,)), ((), ())), precision=precision, preferred_element_type=F32)


def _dot_tn(a, b):
    return lax.dot_general(a, b, (((0,), (0,)), ((), ())), preferred_element_type=F32)


def _layer_norm(z, g, b):
    mu = jnp.mean(z, axis=-1, keepdims=True)
    zc = z - mu
    var = jnp.mean(zc * zc, axis=-1, keepdims=True)
    return zc * lax.rsqrt(var + LN_EPS) * g + b


def _proj_kernel(x_ref, *refs, n_w, n_t):
    w_refs = refs[:n_w]
    wt_refs = refs[n_w:n_w + n_t]
    o_refs = refs[n_w + n_t:2 * n_w + n_t]
    ot_refs = refs[2 * n_w + n_t:]
    x = x_ref[...]
    xb = x.astype(BF16)
    for w_ref, o_ref in zip(w_refs, o_refs):
        ncol = w_ref.shape[1]
        for c0 in range(0, ncol, 512):
            c1 = min(c0 + 512, ncol)
            o_ref[:, c0:c1] = _dot(xb, w_ref[:, c0:c1]).astype(o_ref.dtype)
    for wt_ref, o_ref in zip(wt_refs, ot_refs):
        o_ref[...] = _dot_nt(wt_ref[...], x, precision=HIGHEST)


def _proj(x2, weights, out_dtypes, weights_t=()):
    n, d = x2.shape
    tm = min(PROJ_TM, n)
    in_specs = [pl.BlockSpec((tm, d), lambda i: (i, 0))]
    in_specs += [pl.BlockSpec(w.shape, lambda i: (0, 0)) for w in weights]
    in_specs += [pl.BlockSpec(w.shape, lambda i: (0, 0)) for w in weights_t]
    out_shape = [jax.ShapeDtypeStruct((n, w.shape[1]), dt) for w, dt in zip(weights, out_dtypes)]
    out_shape += [jax.ShapeDtypeStruct((w.shape[0], n), F32) for w in weights_t]
    out_specs = [pl.BlockSpec((tm, w.shape[1]), lambda i: (i, 0)) for w in weights]
    out_specs += [pl.BlockSpec((w.shape[0], tm), lambda i: (0, i)) for w in weights_t]
    return pl.pallas_call(
        functools.partial(_proj_kernel, n_w=len(weights), n_t=len(weights_t)),
        grid=(n // tm,),
        in_specs=in_specs,
        out_specs=out_specs,
        out_shape=out_shape,
        compiler_params=_cparams(("parallel",)),
        name="proj",
    )(x2, *weights, *weights_t)


def _outproj_ln_kernel(h_ref, w_ref, x_ref, g_ref, b_ref, o_ref):
    y = _dot(h_ref[...], w_ref[...])
    o_ref[...] = _layer_norm(DEEPNORM_ALPHA * x_ref[...] + y, g_ref[...], b_ref[...])


def _outproj_ln(h2, w, x2, g, b):
    n, d = x2.shape
    tm = min(PROJ_TM, n)
    return pl.pallas_call(
        _outproj_ln_kernel,
        grid=(n // tm,),
        in_specs=[
            pl.BlockSpec((tm, d), lambda i: (i, 0)),
            pl.BlockSpec((d, d), lambda i: (0, 0)),
            pl.BlockSpec((tm, d), lambda i: (i, 0)),
            pl.BlockSpec((1, d), lambda i: (0, 0)),
            pl.BlockSpec((1, d), lambda i: (0, 0)),
        ],
        out_specs=pl.BlockSpec((tm, d), lambda i: (i, 0)),
        out_shape=jax.ShapeDtypeStruct((n, d), F32),
        compiler_params=_cparams(("parallel",)),
        name="outproj_ln",
    )(h2, w, x2, g.reshape(1, d), b.reshape(1, d))


def _gelu_tanh(x):
    return 0.5 * x * (1.0 + jnp.tanh(np.sqrt(2.0 / np.pi).astype(np.float32) * (x + 0.044715 * (x * x * x))))


def _nsa_compress_kernel(kv_ref, wcat_ref, posa_ref, posb_ref, b1_ref, w2_ref, b2_ref, o_ref):
    n_row = kv_ref.shape[1] // CMP_STRIDE
    half = 2 * CMP_HIDDEN
    acc_a = jnp.zeros((n_row, half), F32)
    acc_b = jnp.zeros((n_row, half), F32)
    for r in range(CMP_STRIDE):
        xr = kv_ref[0, pl.ds(r, n_row, stride=CMP_STRIDE), :]
        w = wcat_ref[r]
        acc_a = acc_a + _dot((xr + posa_ref[r]).astype(BF16), w[:, :half])
        acc_b = acc_b + _dot((xr + posb_ref[r]).astype(BF16), w[:, half:])
    hidden = acc_a + pltpu.roll(acc_b, n_row - 1, 0) + b1_ref[...]
    hidden = _gelu_tanh(hidden)
    o_ref[0, 0] = (_dot(hidden.astype(BF16), w2_ref[...]) + b2_ref[...]).astype(o_ref.dtype)


def _nsa_compress(kvc, wcat, posa, posb, b1cat, w2cat, b2cat):
    bsz, t, _ = kvc.shape
    n_row = t // CMP_STRIDE
    return pl.pallas_call(
        _nsa_compress_kernel,
        grid=(bsz, NSA_KV_GROUPS),
        in_specs=[
            pl.BlockSpec((1, t, LANE), lambda b, g: (b, 0, g)),
            pl.BlockSpec(wcat.shape, lambda b, g: (0, 0, 0)),
            pl.BlockSpec(posa.shape, lambda b, g: (0, 0, 0)),
            pl.BlockSpec(posb.shape, lambda b, g: (0, 0, 0)),
            pl.BlockSpec(b1cat.shape, lambda b, g: (0, 0)),
            pl.BlockSpec(w2cat.shape, lambda b, g: (0, 0)),
            pl.BlockSpec(b2cat.shape, lambda b, g: (0, 0)),
        ],
        out_specs=pl.BlockSpec((1, 1, n_row, LANE), lambda b, g: (b, g, 0, 0)),
        out_shape=jax.ShapeDtypeStruct((bsz, NSA_KV_GROUPS, n_row, LANE), BF16),
        compiler_params=_cparams(("parallel", "parallel")),
        name="nsa_compress",
    )(kvc, wcat, posa, posb, b1cat, w2cat, b2cat)


def _nsa_attn_kernel(q_ref, kvc_ref, kvs_ref, kvw_ref, gate_ref, gb_ref, c2s_ref, o_ref, *, tq, tk, n_cmp):
    def tile(qi, carry):
        _nsa_attn_tile(qi, q_ref, kvc_ref, kvs_ref, kvw_ref, gate_ref, gb_ref, c2s_ref, o_ref, tq=tq, tk=tk, n_cmp=n_cmp)
        return carry
    lax.fori_loop(0, q_ref.shape[1] // tq, tile, 0)


def _nsa_attn_tile(qi, q_ref, kvc_ref, kvs_ref, kvw_ref, gate_ref, gb_ref, c2s_ref, o_ref, *, tq, tk, n_cmp):
    hpg = NSA_HPG
    t0 = pl.multiple_of(qi * tq, tq)
    t_col = t0 + lax.broadcasted_iota(jnp.int32, (tq, 1), 0)
    lane = lax.broadcasted_iota(jnp.int32, (1, LANE), 1)
    ones_lane = (lane == 0).astype(BF16)

    def with_ones(kv):
        return jnp.where(lane < NSA_HEAD_DIM, ones_lane, kv)

    gates = jax.nn.sigmoid(gate_ref[0, pl.ds(t0, tq), :].astype(F32) + gb_ref[0])
    gate_of = [[jnp.broadcast_to(gates[:, 3 * h + j:3 * h + j + 1], (tq, LANE)) for j in range(3)] for h in range(hpg)]

    span = WINDOW + tq
    w0 = pl.multiple_of(jnp.maximum(t0 - WINDOW, 0), tq)
    kvw = kvw_ref[0, pl.ds(w0, span), :]
    kvw1 = with_ones(kvw)
    kvc = kvc_ref[0, 0]
    n_c = kvc.shape[0]
    q_heads = []
    for h in range(hpg):
        pair_blk = q_ref[0, pl.ds(t0, tq), (h // 2) * LANE:(h // 2 + 1) * LANE].astype(F32)
        if h % 2:
            pair_blk = pltpu.roll(pair_blk, NSA_HEAD_DIM, 1)
        q_heads.append((jnp.where(lane < NSA_HEAD_DIM, pair_blk, 0.0) * NSA_HEAD_DIM ** -0.5).astype(BF16))
    s_c = [_dot_nt(q_heads[h], kvc) for h in range(hpg)]
    s_w = [_dot_nt(q_heads[h], kvw).astype(BF16) for h in range(hpg)]

    c_lane = lax.broadcasted_iota(jnp.int32, (1, n_c), 1)
    valid_c = (c_lane * CMP_STRIDE + (CMP_BLOCK - 1) <= t_col) & (c_lane < n_cmp)
    p_heads = []
    for h in range(hpg):
        sm = jnp.where(valid_c, s_c[h], NEG)
        m = jnp.max(sm, axis=-1, keepdims=True)
        p = jnp.where(valid_c, jnp.exp(sm - m), 0.0)
        l = jnp.sum(p, axis=-1, keepdims=True)
        p_heads.append(p / jnp.maximum(l, 1e-30))
    p_sum = p_heads[0]
    for h in range(1, hpg):
        p_sum = p_sum + p_heads[h]
    imp = _dot_nt(c2s_ref[...], p_sum, precision=HIGHEST)
    o_c = [_dot(p_heads[h].astype(BF16), kvc) for h in range(hpg)]

    lag = t_col - (w0 + lax.broadcasted_iota(jnp.int32, (1, span), 1))
    bias_w = jnp.where((lag >= 0) & (lag < WINDOW), 0.0, NEG).astype(BF16)
    o_w = []
    for h in range(hpg):
        sm = s_w[h] + bias_w
        o_w.append(_dot(jnp.exp(sm - jnp.max(sm, axis=-1, keepdims=True)), kvw1))
    partial = [gate_of[h][0] * o_c[h] + gate_of[h][2] * (o_w[h] / o_w[h][:, 0:1]) for h in range(hpg)]

    n_sel = imp.shape[0]
    blk = lax.broadcasted_iota(jnp.int32, (n_sel, 1), 0)
    cur = (t0 + lax.broadcasted_iota(jnp.int32, (1, tq), 1)) // SEL_BLOCK
    forced = (blk == 0) | (blk == cur) | (blk == cur - 1)
    score = jnp.where(forced, -NEG, jnp.where(blk <= cur, imp, NEG))
    rank = jnp.zeros((n_sel, tq), F32)
    for i in range(n_sel):
        s_i = score[i:i + 1, :]
        ahead = (s_i > score) | ((s_i == score) & (blk > i))
        rank = rank + ahead.astype(F32)
    member = ((rank < SEL_TOPK) & (score > 0.5 * NEG)).astype(BF16)

    blk_per_tile = tk // SEL_BLOCK
    key_lane = lax.broadcasted_iota(jnp.int32, (1, tk), 1)

    def sel_body(kt, carry):
        k0 = pl.multiple_of(kt * tk, tk)
        kv = kvs_ref[0, pl.ds(k0, tk), :]
        kv1 = with_ones(kv)
        expand = (key_lane // SEL_BLOCK + kt * blk_per_tile == blk).astype(BF16)
        valid = (_dot_tn(member, expand) > 0.5) & (k0 + key_lane <= t_col)
        bias = jnp.where(valid, 0.0, NEG).astype(BF16)
        s = [_dot_nt(q_heads[h], kv).astype(BF16) for h in range(hpg)]
        new = []
        for h in range(hpg):
            m_old, acc_old = carry[h]
            sm = s[h] + bias
            m_new = jnp.maximum(m_old, jnp.max(sm, axis=-1, keepdims=True).astype(F32))
            pv = _dot(jnp.exp(sm - m_new.astype(BF16)), kv1)
            new.append((m_new, jnp.exp(m_old - m_new) * acc_old + pv))
        return tuple(new)

    init = tuple((jnp.full((tq, 1), M_FLOOR, F32), jnp.zeros((tq, LANE), F32)) for _ in range(hpg))
    n_kt = (t0 + tq - 1) // tk + 1
    sel = lax.fori_loop(0, n_kt, sel_body, init)

    outs = []
    for h in range(hpg):
        o_s = sel[h][1]
        outs.append(partial[h] + gate_of[h][1] * (o_s / o_s[:, 0:1]))
    for j in range(hpg // 2):
        pair = jnp.where(lane < NSA_HEAD_DIM, pltpu.roll(outs[2 * j], NSA_HEAD_DIM, 1), outs[2 * j + 1])
        o_ref[0, pl.ds(t0, tq), j * LANE:(j + 1) * LANE] = pair.astype(o_ref.dtype)


def _nsa_attention(main, kvcmp, gate_b, c2s, bsz, t):
    tq = min(NSA_TQ, t)
    tk = min(NSA_TK, t)
    n_cmp = (t - CMP_BLOCK) // CMP_STRIDE + 1
    q_blocks = NSA_HEADS * NSA_HEAD_DIM // LANE
    g = NSA_KV_GROUPS
    kernel = functools.partial(_nsa_attn_kernel, tq=tq, tk=tk, n_cmp=n_cmp)
    return pl.pallas_call(
        kernel,
        grid=(bsz, g),
        in_specs=[
            pl.BlockSpec((1, t, NSA_HPG * NSA_HEAD_DIM), lambda b, gi: (b, 0, gi)),
            pl.BlockSpec((1, 1) + kvcmp.shape[2:], lambda b, gi: (b, gi, 0, 0)),
            pl.BlockSpec((1, t, LANE), lambda b, gi: (b, 0, q_blocks + gi)),
            pl.BlockSpec((1, t, LANE), lambda b, gi: (b, 0, q_blocks + g + gi)),
            pl.BlockSpec((1, t, LANE), lambda b, gi: (b, 0, q_blocks + 2 * g + gi)),
            pl.BlockSpec((1, 1, LANE), lambda b, gi: (gi, 0, 0)),
            pl.BlockSpec(c2s.shape, lambda b, gi: (0, 0)),
        ],
        out_specs=pl.BlockSpec((1, t, NSA_HPG * NSA_HEAD_DIM), lambda b, gi: (b, 0, gi)),
        out_shape=jax.ShapeDtypeStruct((bsz, t, D_MODEL), BF16),
        compiler_params=_cparams(("parallel", "parallel")),
        name="nsa_attention",
    )(main, kvcmp, main, main, main, gate_b, c2s)


def _nsa_mixer(x2, bsz, t, w_in, cmp_pos, cmp_w1, cmp_b1, cmp_w2, cmp_b2, gate_b, w_out, ln_g, ln_b):
    d, g, dh, hpg = D_MODEL, NSA_KV_GROUPS, NSA_HEAD_DIM, NSA_HPG
    kvw_ = NSA_KV_WIDTH

    def pair(k0):
        k = w_in[:, k0:k0 + kvw_].reshape(d, g, 1, dh)
        v = w_in[:, k0 + kvw_:k0 + 2 * kvw_].reshape(d, g, 1, dh)
        return jnp.concatenate([k, v], axis=2).reshape(d, g * LANE)

    wq = w_in[:, :d]
    n_gate = 3 * hpg
    wg = jnp.pad(w_in[:, d + 6 * kvw_:].reshape(d, g, n_gate), ((0, 0), (0, 0), (0, LANE - n_gate))).reshape(d, g * LANE)
    w_main = jnp.concatenate([wq, pair(d + 2 * kvw_), pair(d + 4 * kvw_), wg], axis=1).astype(BF16)
    w_kvc = pair(d).astype(BF16)
    main, kvc = _proj(x2, [w_main, w_kvc], [BF16, F32])

    half = CMP_BLOCK // 2
    w1 = cmp_w1.reshape(2, CMP_BLOCK, dh, CMP_HIDDEN)
    z = jnp.zeros((half, dh, CMP_HIDDEN), F32)
    top = jnp.concatenate([w1[0, :half], z, w1[0, half:], z], axis=2)
    bot = jnp.concatenate([z, w1[1, :half], z, w1[1, half:]], axis=2)
    wcat = jnp.concatenate([top, bot], axis=1).astype(BF16)
    pos = jnp.concatenate([cmp_pos[0], cmp_pos[1]], axis=-1)
    posa, posb = pos[:half, None, :], pos[half:, None, :]
    b1cat = cmp_b1.reshape(1, 2 * CMP_HIDDEN)
    zz = jnp.zeros((CMP_HIDDEN, dh), F32)
    w2cat = jnp.concatenate([jnp.concatenate([cmp_w2[0], zz], axis=1),
                             jnp.concatenate([zz, cmp_w2[1]], axis=1)], axis=0).astype(BF16)
    b2cat = cmp_b2.reshape(1, 2 * dh)
    kvcmp = _nsa_compress(kvc.reshape(bsz, t, g * LANE), wcat, posa, posb, b1cat, w2cat, b2cat)

    n_row = t // CMP_STRIDE
    n_sel = t // SEL_BLOCK
    cmp_start = np.arange(n_row) * CMP_STRIDE
    sel_start = np.arange(n_sel) * SEL_BLOCK
    overlap = (np.minimum(cmp_start[:, None] + CMP_BLOCK, sel_start[None, :] + SEL_BLOCK)
               - np.maximum(cmp_start[:, None], sel_start[None, :]))
    c2s = jnp.asarray((np.clip(overlap, 0, None) / CMP_STRIDE).astype(np.float32).T)
    gb = jnp.pad(gate_b.reshape(g, 1, n_gate), ((0, 0), (0, 0), (0, LANE - n_gate)))
    o = _nsa_attention(main.reshape(bsz, t, -1), kvcmp, gb, c2s, bsz, t)
    return _outproj_ln(o.reshape(bsz * t, d), w_out.astype(BF16), x2, ln_g, ln_b)


def _mlstm_kernel(q_ref, k_ref, v_ref, og_ref, gif_ref, gb_ref, cwq_ref, cwk_ref, cbq_ref, cbk_ref, ng_ref,
                  o_ref, pad_s, q_s, k_s, ct_s, *, chunk):
    t = q_ref.shape[1]
    dk, dv = MLSTM_QK_DIM, MLSTM_V_DIM
    head = pl.program_id(1)
    halo = 8

    def conv_silu(x_ref, w_ref, b_ref, dst, scale):
        pad_s[0:halo, :] = jnp.zeros((halo, dk), F32)
        pad_s[halo:halo + t, :] = x_ref[0]
        rows = min(t, 256)
        for r0 in range(0, t, rows):
            y = b_ref[...] + w_ref[0:1, :] * pad_s[pl.ds(halo + r0 - (MLSTM_CONV - 1), rows), :]
            for j in range(1, MLSTM_CONV):
                y = y + w_ref[j:j + 1, :] * pad_s[pl.ds(halo + r0 - (MLSTM_CONV - 1) + j, rows), :]
            y = y * jax.nn.sigmoid(y)
            dst[r0:r0 + rows, :] = (y * scale).astype(dst.dtype)

    conv_silu(q_ref, cwq_ref, cbq_ref, q_s, 1.0)
    conv_silu(k_ref, cwk_ref, cbk_ref, k_s, dk ** -0.5)
    ct_s[...] = jnp.zeros((dk, dv), F32)

    row = lax.broadcasted_iota(jnp.int32, (chunk, chunk), 0)
    col = lax.broadcasted_iota(jnp.int32, (chunk, chunk), 1)
    causal = col <= row
    eye = col == row
    b_i = gb_ref[pl.ds(head, 1), :]
    b_f = gb_ref[pl.ds(MLSTM_HEADS + head, 1), :]
    norm_g = ng_ref[...]

    def body(c, carry):
        n_row, m_prev = carry
        r0 = pl.multiple_of(c * chunk, chunk)
        qc = q_s[pl.ds(r0, chunk), :]
        kc = k_s[pl.ds(r0, chunk), :]
        vc = v_ref[0, pl.ds(r0, chunk), :]
        li_row = gif_ref[0, c, pl.ds(head, 1), :] + b_i
        zf = gif_ref[0, c, pl.ds(MLSTM_HEADS + head, 1), :] + b_f
        lf_row = jnp.minimum(zf, 0.0) - jnp.log1p(jnp.exp(-jnp.abs(zf)))
        b_col = jnp.sum(jnp.where(causal, lf_row, 0.0), axis=-1, keepdims=True)
        b_row = jnp.sum(jnp.where(eye, b_col, 0.0), axis=0, keepdims=True)
        li_col = jnp.sum(jnp.where(eye, li_row, 0.0), axis=-1, keepdims=True)
        d_mat = jnp.where(causal, b_col - b_row + li_row, NEG)
        m_t = jnp.maximum(b_col + m_prev, jnp.max(d_mat, axis=-1, keepdims=True))
        w_inter = jnp.exp(b_col + m_prev - m_t)
        s = _dot_nt(qc, kc) * jnp.exp(d_mat - m_t)
        num = _dot(s.astype(BF16), vc) + w_inter * _dot(qc, ct_s[...].astype(BF16))
        den = jnp.sum(s, axis=-1, keepdims=True) + w_inter * jnp.sum(qc.astype(F32) * n_row, axis=-1, keepdims=True)
        h = num / jnp.maximum(jnp.abs(den), jnp.exp(-m_t))
        b_end = b_col[chunk - 1:chunk, :]
        decay = b_end - b_col + li_col
        m_new = jnp.maximum(b_end + m_prev, jnp.max(decay, axis=0, keepdims=True))
        w_k = jnp.exp(decay - m_new)
        scale = jnp.exp(b_end + m_prev - m_new)
        ct_s[...] = scale * ct_s[...] + _dot_tn(kc, (vc.astype(F32) * w_k).astype(BF16))
        n_new = scale * n_row + jnp.sum(w_k * kc.astype(F32), axis=0, keepdims=True)
        hn = h * lax.rsqrt(jnp.mean(h * h, axis=-1, keepdims=True) + RMS_EPS) * norm_g
        og = og_ref[0, pl.ds(r0, chunk), :].astype(F32)
        o_ref[0, pl.ds(r0, chunk), :] = (hn * jax.nn.sigmoid(og)).astype(o_ref.dtype)
        return n_new, m_new

    lax.fori_loop(0, t // chunk, body, (jnp.zeros((1, dk), F32), jnp.zeros((1, 1), F32)))


def _mlstm_core(qk, v, og, gif, gate_b, conv_w, conv_b, norm_g, bsz, t, chunk):
    hh, dk, dv = MLSTM_HEADS, MLSTM_QK_DIM, MLSTM_V_DIM
    nc = t // chunk
    kernel = functools.partial(_mlstm_kernel, chunk=chunk)
    return pl.pallas_call(
        kernel,
        grid=(bsz, hh),
        in_specs=[
            pl.BlockSpec((1, t, dk), lambda b, h: (b, 0, h)),
            pl.BlockSpec((1, t, dk), lambda b, h: (b, 0, hh + h)),
            pl.BlockSpec((1, t, dv), lambda b, h: (b, 0, h)),
            pl.BlockSpec((1, t, dv), lambda b, h: (b, 0, h)),
            pl.BlockSpec((1, nc, 2 * hh, chunk), lambda b, h: (b, 0, 0, 0)),
            pl.BlockSpec((2 * hh, 1), lambda b, h: (0, 0)),
            pl.BlockSpec((MLSTM_CONV, dk), lambda b, h: (0, h)),
            pl.BlockSpec((MLSTM_CONV, dk), lambda b, h: (0, hh + h)),
            pl.BlockSpec((1, dk), lambda b, h: (0, h)),
            pl.BlockSpec((1, dk), lambda b, h: (0, hh + h)),
            pl.BlockSpec((1, dv), lambda b, h: (0, h)),
        ],
        out_specs=pl.BlockSpec((1, t, dv), lambda b, h: (b, 0, h)),
        out_shape=jax.ShapeDtypeStruct((bsz, t, hh * dv), BF16),
        scratch_shapes=[
            pltpu.VMEM((t + 8, dk), F32),
            pltpu.VMEM((t, dk), BF16),
            pltpu.VMEM((t, dk), BF16),
            pltpu.VMEM((dk, dv), F32),
        ],
        compiler_params=_cparams(("parallel", "parallel")),
        name="mlstm",
    )(qk, qk, v, og, gif, gate_b.reshape(2 * hh, 1), conv_w, conv_w, conv_b.reshape(1, -1), conv_b.reshape(1, -1),
      norm_g.reshape(1, -1))


def _mlstm_mixer(x2, bsz, t, w_in, conv_w, conv_b, gate_b, norm_g, w_out, ln_g, ln_b):
    d, hh = D_MODEL, MLSTM_HEADS
    qkw = 2 * hh * MLSTM_QK_DIM
    vw = hh * MLSTM_V_DIM
    chunk = min(MLSTM_L, t)
    w_qk = w_in[:, :qkw].astype(BF16)
    w_v = w_in[:, qkw:qkw + vw].astype(BF16)
    w_og = w_in[:, qkw + vw:qkw + vw + d].astype(BF16)
    w_gif_t = w_in[:, qkw + vw + d:].T
    qk, v, og, gif_t = _proj(x2, [w_qk, w_v, w_og], [F32, BF16, BF16], [w_gif_t])
    gif = gif_t.reshape(2 * hh, bsz, t // chunk, chunk).transpose(1, 2, 0, 3)
    h = _mlstm_core(qk.reshape(bsz, t, qkw), v.reshape(bsz, t, vw), og.reshape(bsz, t, d), gif, gate_b,
                    conv_w, conv_b, norm_g, bsz, t, chunk)
    return _outproj_ln(h.reshape(bsz * t, d), w_out.astype(BF16), x2, ln_g, ln_b)


def _hgrn_kernel(q_ref, f_ref, i_ref, g_ref, low_ref, ng_ref, o_ref, st_s, qd_s, kd_s, ke_s, dec_s, *,
                 layer_idx, heads):
    t = q_ref.shape[1]
    dh, chunk = HGRN_DIM, HGRN_CHUNK
    low = low_ref[...]
    e = jnp.exp(low - jnp.max(low, axis=0, keepdims=True))
    soft = e / jnp.sum(e, axis=0, keepdims=True)
    lb = jnp.zeros((1, heads * dh), F32)
    for r in range(1, layer_idx + 1):
        lb = lb + soft[r:r + 1, :]
    norm_g = ng_ref[...]
    st_s[...] = jnp.zeros(st_s.shape, F32)
    width = heads * dh
    sup = min(HGRN_SUPER, t)
    n_sub = sup // chunk

    in_chunk = lax.broadcasted_iota(jnp.int32, (sup, 1), 0) % chunk

    def prep(s, carry):
        r0 = pl.multiple_of(s * sup, sup)
        f = lb + (1.0 - lb) * jax.nn.sigmoid(f_ref[0, pl.ds(r0, sup), :])
        gcum = jnp.log(f)
        shift = 1
        while shift < chunk:
            gcum = gcum + jnp.where(in_chunk >= shift, pltpu.roll(gcum, shift, 0), 0.0)
            shift *= 2
        g_end = jnp.concatenate(
            [jnp.broadcast_to(gcum[(j + 1) * chunk - 1:(j + 1) * chunk, :], (chunk, width)) for j in range(n_sub)], axis=0)
        qv = q_ref[0, pl.ds(r0, sup), :].astype(F32)
        kk = 1.0 - f
        qd_s[pl.ds(r0, sup), :] = (qv * jax.nn.sigmoid(qv) * jnp.exp(gcum)).astype(BF16)
        kd_s[pl.ds(r0, sup), :] = (kk * jnp.exp(-gcum)).astype(BF16)
        ke_s[pl.ds(r0, sup), :] = (kk * jnp.exp(g_end - gcum)).astype(BF16)
        for j in range(n_sub):
            dec_s[s, j:j + 1, :] = jnp.exp(gcum[(j + 1) * chunk - 1:(j + 1) * chunk, :])
        return carry

    lax.fori_loop(0, t // sup, prep, 0)

    row = lax.broadcasted_iota(jnp.int32, (sup, sup), 0)
    col = lax.broadcasted_iota(jnp.int32, (sup, sup), 1)
    keep = (col <= row) & (col // chunk == row // chunk)

    def body(s, carry):
        r0 = pl.multiple_of(s * sup, sup)
        decay = dec_s[s]
        col_of = [slice(h * dh, (h + 1) * dh) for h in range(heads)]
        sub_of = [slice(j * chunk, (j + 1) * chunk) for j in range(n_sub)]
        q_dec = [qd_s[pl.ds(r0, sup), c] for c in col_of]
        vv = [i_ref[0, pl.ds(r0, sup), c] for c in col_of]
        score = [_dot_nt(q_dec[h], kd_s[pl.ds(r0, sup), col_of[h]]) for h in range(heads)]
        k_end = [ke_s[pl.ds(r0, sup), c] for c in col_of]
        kv = [[_dot_tn(vv[h][r], k_end[h][r]) for r in sub_of] for h in range(heads)]
        intra = [_dot(jnp.where(keep, score[h], 0.0).astype(BF16), vv[h]) for h in range(heads)]
        states = []
        for h in range(heads):
            st = st_s[h]
            per_sub = []
            for j in range(n_sub):
                per_sub.append(st.astype(BF16))
                st = st * decay[j:j + 1, col_of[h]] + kv[h][j]
            st_s[h] = st
            states.append(per_sub)
        for h in range(heads):
            inter = [_dot_nt(q_dec[h][sub_of[j]], states[h][j]) for j in range(n_sub)]
            o = intra[h] + jnp.concatenate(inter, axis=0)
            on = o * lax.rsqrt(jnp.mean(o * o, axis=-1, keepdims=True) + RMS_EPS) * norm_g[:, col_of[h]]
            gate = jax.nn.sigmoid(g_ref[0, pl.ds(r0, sup), col_of[h]].astype(F32))
            o_ref[0, pl.ds(r0, sup), col_of[h]] = (on * gate).astype(o_ref.dtype)
        return carry

    lax.fori_loop(0, t // sup, body, 0)


def _hgrn_core(q, f, i, g, lower, norm_g, bsz, t, layer_idx):
    heads = 4
    width = heads * HGRN_DIM
    n_grp = HGRN_HEADS // heads
    kernel = functools.partial(_hgrn_kernel, layer_idx=layer_idx, heads=heads)
    act = pl.BlockSpec((1, t, width), lambda b, j: (b, 0, j))
    return pl.pallas_call(
        kernel,
        grid=(bsz, n_grp),
        in_specs=[act, act, act, act,
                  pl.BlockSpec((DEPTH, width), lambda b, j: (0, j)),
                  pl.BlockSpec((1, width), lambda b, j: (0, j))],
        out_specs=act,
        out_shape=jax.ShapeDtypeStruct((bsz, t, D_MODEL), BF16),
        scratch_shapes=[pltpu.VMEM((heads, HGRN_DIM, HGRN_DIM), F32), pltpu.VMEM((t, width), BF16),
                        pltpu.VMEM((t, width), BF16), pltpu.VMEM((t, width), BF16),
                        pltpu.VMEM((max(t // HGRN_SUPER, 1), min(HGRN_SUPER, t) // HGRN_CHUNK, width), F32)],
        compiler_params=_cparams(("parallel", "parallel")),
        name="hgrn2",
    )(q, f, i, g, lower, norm_g.reshape(1, -1))


def _hgrn_mixer(x2, bsz, t, layer_idx, w_in, lower, norm_g, w_out, ln_g, ln_b):
    d = D_MODEL
    ws = [w_in[:, j * d:(j + 1) * d].astype(BF16) for j in range(4)]
    q, f, i, g = _proj(x2, ws, [BF16, F32, BF16, BF16])
    shp = (bsz, t, d)
    o = _hgrn_core(q.reshape(shp), f.reshape(shp), i.reshape(shp), g.reshape(shp), lower, norm_g, bsz, t, layer_idx)
    return _outproj_ln(o.reshape(bsz * t, d), w_out.astype(BF16), x2, ln_g, ln_b)


def _router_kernel(x_ref, rwt_ref, rb_ref, spos_ref, gate_ref, tinfo_ref, seg_ref, carry_s):
    @pl.when(pl.program_id(0) == 0)
    def _():
        carry_s[...] = jnp.zeros(carry_s.shape, F32)

    tm = x_ref.shape[0]
    ne = N_EXPERTS
    logits = _dot_nt(rwt_ref[...], x_ref[...], precision=HIGHEST) + rb_ref[...]
    e_iota = lax.broadcasted_iota(jnp.int32, (ne, tm), 0)
    work = logits
    vals, picks = [], []
    for k in range(TOP_K):
        mx = jnp.max(work, axis=0, keepdims=True)
        idx = jnp.min(jnp.where(work == mx, e_iota, ne), axis=0, keepdims=True)
        pick = e_iota == idx
        vals.append(mx)
        picks.append(pick)
        work = jnp.where(pick, -jnp.inf, work)
    exps = [jnp.exp(v - vals[0]) for v in vals]
    tot = exps[0]
    for k in range(1, TOP_K):
        tot = tot + exps[k]
    for k in range(TOP_K):
        gate_ref[k:k + 1, :] = exps[k] / tot
    hot = picks[0].astype(F32)
    for k in range(1, TOP_K):
        hot = hot + picks[k].astype(F32)
    before = (lax.broadcasted_iota(jnp.int32, (tm, tm), 0) < lax.broadcasted_iota(jnp.int32, (tm, tm), 1)).astype(BF16)
    prior = _dot(hot.astype(BF16), before)
    cnt = jnp.sum(hot, axis=1, keepdims=True)
    run = jnp.floor((cnt + (ROW_ALIGN - 1)) * (1.0 / ROW_ALIGN)) * ROW_ALIGN
    sub = lax.broadcasted_iota(jnp.int32, (ne, ne), 0)
    lan = lax.broadcasted_iota(jnp.int32, (ne, ne), 1)
    run_row = jnp.sum(jnp.where(sub == lan, run, 0.0), axis=0, keepdims=True)
    soff = jnp.sum(jnp.where(lan < sub, run_row, 0.0), axis=1, keepdims=True)
    for k in range(TOP_K):
        pos = jnp.sum(jnp.where(picks[k], prior + soff, 0.0), axis=0, keepdims=True)
        spos_ref[k:k + 1, :] = pos.astype(jnp.int32)
    n_big = jnp.floor(run * (1.0 / BIG_GROUP))
    sub = lax.broadcasted_iota(jnp.int32, (ne, LANE), 0)
    lan = lax.broadcasted_iota(jnp.int32, (ne, LANE), 1)
    field, le = lan // ne, lan % ne
    carry = carry_s[...]
    info = (jnp.where((field == 0) & (sub == le), carry, 0.0) + jnp.where((field == 1) & (sub == le), run, 0.0)
            + jnp.where((field == 2) & (sub < le), run, 0.0)
            + jnp.where((field == 3) & (le == 0), (run - n_big * BIG_GROUP) * (1.0 / ROW_ALIGN), 0.0)
            + jnp.where((field == 3) & (le == 1), n_big, 0.0))
    tinfo_ref[0] = jnp.sum(info, axis=0, keepdims=True).astype(jnp.int32)
    total = carry + run
    carry_s[...] = total
    seg_ref[...] = total.astype(jnp.int32)


def _router(x2, router_w, router_b):
    n, d = x2.shape
    tm = min(MOE_TILE, n)
    n_tiles = n // tm
    row = pl.BlockSpec((TOP_K, tm), lambda i: (0, i))
    return pl.pallas_call(
        _router_kernel,
        grid=(n_tiles,),
        in_specs=[
            pl.BlockSpec((tm, d), lambda i: (i, 0)),
            pl.BlockSpec((N_EXPERTS, d), lambda i: (0, 0)),
            pl.BlockSpec((N_EXPERTS, 1), lambda i: (0, 0)),
        ],
        out_specs=[row, row, pl.BlockSpec((1, 1, LANE), lambda i: (i, 0, 0)),
                   pl.BlockSpec((N_EXPERTS, 1), lambda i: (0, 0))],
        out_shape=[
            jax.ShapeDtypeStruct((TOP_K, n), jnp.int32),
            jax.ShapeDtypeStruct((TOP_K, n), F32),
            jax.ShapeDtypeStruct((n_tiles, 1, LANE), jnp.int32),
            jax.ShapeDtypeStruct((N_EXPERTS, 1), jnp.int32),
        ],
        scratch_shapes=[pltpu.VMEM((N_EXPERTS, 1), F32)],
        compiler_params=_cparams(("arbitrary",)),
        name="router",
    )(x2, router_w.T, router_b.reshape(N_EXPERTS, 1))


def _group_copy(src, s, dst, d, sem, rows=ROW_ALIGN):
    return pltpu.make_async_copy(src.at[pl.ds(pl.multiple_of(s, ROW_ALIGN), rows), :],
                                 dst.at[pl.ds(pl.multiple_of(d, ROW_ALIGN), rows), :], sem)


def _for_each_group(tinfo_ref, base_ref, fn):
    def per_expert(e, _):
        seg_row = base_ref[e] + tinfo_ref[0, 0, e]
        run = tinfo_ref[0, 0, N_EXPERTS + e]
        sorted_row = tinfo_ref[0, 0, 2 * N_EXPERTS + e]
        n_big = run // BIG_GROUP

        def big(j, _):
            fn(sorted_row + j * BIG_GROUP, seg_row + j * BIG_GROUP, BIG_GROUP)
            return 0

        def small(j, _):
            fn(sorted_row + n_big * BIG_GROUP + j * ROW_ALIGN, seg_row + n_big * BIG_GROUP + j * ROW_ALIGN, ROW_ALIGN)
            return 0

        lax.fori_loop(0, n_big, big, 0)
        return lax.fori_loop(0, (run - n_big * BIG_GROUP) // ROW_ALIGN, small, 0)
    lax.fori_loop(0, N_EXPERTS, per_expert, 0)


def _wait_groups(n_small, n_big, src, dst, sem):
    for count, rows in ((n_big, BIG_GROUP), (n_small, ROW_ALIGN)):
        def one(j, _, rows=rows):
            _group_copy(src, 0, dst, 0, sem, rows).wait()
            return 0
        lax.fori_loop(0, count, one, 0)


def _dispatch_kernel(base_ref, seg_ref, tinfo_ref, spos_ref, gate_ref, x_ref, xs_ref, hot_ref, sort2_s, zero_s, pending_s,
                     sems):
    tb, d = x_ref.shape
    s_rows = sort2_s.shape[1]
    step = pl.program_id(0)
    sort_s, sem = sort2_s.at[step % 2], sems.at[step % 2]
    prev_sem = sems.at[(step + 1) % 2]

    @pl.when(step == 0)
    def _():
        zero_s[...] = jnp.zeros(zero_s.shape, F32)
        for fill in (True, False):
            def per_expert(e, _):
                def per_group(r, _):
                    cp = _group_copy(zero_s, 0, xs_ref, r * ROW_ALIGN, sem)
                    cp.start() if fill else cp.wait()
                    return 0
                first = jnp.where(e < N_EXPERTS, base_ref[jnp.minimum(e, N_EXPERTS - 1)] + seg_ref[jnp.minimum(e, N_EXPERTS - 1)],
                                  base_ref[N_EXPERTS])
                last = jnp.where(e < N_EXPERTS, base_ref[jnp.minimum(e + 1, N_EXPERTS)], xs_ref.shape[0])
                return lax.fori_loop(first // ROW_ALIGN, last // ROW_ALIGN, per_group, 0)
            lax.fori_loop(0, N_EXPERTS + 1, per_expert, 0)

    xb = x_ref[...].astype(BF16)
    spos = spos_ref[...]
    gate = gate_ref[...]
    for r0 in range(0, s_rows, SORT_ROWS):
        r_iota = r0 + lax.broadcasted_iota(jnp.int32, (SORT_ROWS, tb), 0)
        hits = [spos[k:k + 1, :] == r_iota for k in range(TOP_K)]
        onehot = jnp.where(hits[0] | hits[1] | hits[2] | hits[3], 1.0, 0.0).astype(BF16)
        hot_ref[0, r0:r0 + SORT_ROWS, :] = onehot
        sort_s[r0:r0 + SORT_ROWS, :d] = _dot(onehot, xb)
        g_sel = jnp.zeros((SORT_ROWS, tb), F32)
        for k in range(TOP_K):
            g_sel = jnp.where(hits[k], gate[k:k + 1, :], g_sel)
        sort_s[r0:r0 + SORT_ROWS, d:] = jnp.broadcast_to(jnp.sum(g_sel, axis=1, keepdims=True), (SORT_ROWS, LANE))

    _for_each_group(tinfo_ref, base_ref, lambda s, r, rows: _group_copy(sort_s, s, xs_ref, r, sem, rows).start())
    n_small, n_big = tinfo_ref[0, 0, 3 * N_EXPERTS], tinfo_ref[0, 0, 3 * N_EXPERTS + 1]

    @pl.when(step > 0)
    def _():
        _wait_groups(pending_s[0], pending_s[1], sort_s, xs_ref, prev_sem)

    pending_s[0] = n_small
    pending_s[1] = n_big

    @pl.when(step == pl.num_programs(0) - 1)
    def _():
        _wait_groups(n_small, n_big, sort_s, xs_ref, sem)


def _tile_info_spec(n_tiles, ahead=0):
    return pl.BlockSpec((1, 1, LANE), lambda i, *_: (jnp.minimum(i + ahead, n_tiles - 1), 0, 0),
                        memory_space=pltpu.SMEM)


def _dispatch(x2, spos, gate, tinfo, base, seg, rows):
    n, d = x2.shape
    tb = min(MOE_TILE, n)
    row = pl.BlockSpec((TOP_K, tb), lambda i, *_: (0, i))
    return pl.pallas_call(
        _dispatch_kernel,
        grid_spec=pltpu.PrefetchScalarGridSpec(
            num_scalar_prefetch=2,
            grid=(n // tb,),
            in_specs=[_tile_info_spec(n // tb), row, row, pl.BlockSpec((tb, d), lambda i, *_: (i, 0))],
            out_specs=[pl.BlockSpec(memory_space=pl.ANY),
                       pl.BlockSpec((1, _sorted_rows(tb), tb), lambda i, *_: (i, 0, 0))],
            scratch_shapes=[pltpu.VMEM((2, _sorted_rows(tb), d + LANE), F32), pltpu.VMEM((ROW_ALIGN, d + LANE), F32),
                            pltpu.SMEM((2,), jnp.int32), pltpu.SemaphoreType.DMA((2,))],
        ),
        out_shape=[jax.ShapeDtypeStruct((rows, d + LANE), F32),
                   jax.ShapeDtypeStruct((n // tb, _sorted_rows(tb), tb), BF16)],
        compiler_params=_cparams(("arbitrary",)),
        name="moe_dispatch",
    )(base, seg, tinfo, spos, gate, x2)


def _expert_kernel(be_ref, nused_ref, x_ref, wgu_ref, bgu_ref, wd_ref, bd_ref, y_ref, wgu_s, wd_s):
    i = pl.program_id(0)
    used = i < nused_ref[0]

    @pl.when(used & ((i == 0) | (be_ref[i] != be_ref[jnp.maximum(i - 1, 0)])))
    def _():
        wgu_s[...] = wgu_ref[0, 0].astype(BF16)
        wd_s[...] = wd_ref[0, 0].astype(BF16)

    @pl.when(used)
    def _():
        d = y_ref.shape[1]
        xb = x_ref[:, :d].astype(BF16)
        h = _dot(xb, wgu_s[...]) + bgu_ref[0, 0]
        h_gate = jnp.minimum(h[:, :D_FF], SWIGLU_LIMIT)
        h_up = jnp.clip(h[:, D_FF:], -SWIGLU_LIMIT, SWIGLU_LIMIT)
        act = (h_up + 1.0) * h_gate * jax.nn.sigmoid(SWIGLU_ALPHA * h_gate)
        y_ref[...] = (_dot(act.astype(BF16), wd_s[...]) + bd_ref[0, 0]) * x_ref[:, d:d + 1]

    @pl.when(jnp.logical_not(used))
    def _():
        y_ref[...] = jnp.zeros(y_ref.shape, F32)


def _experts(xs, block_expert, n_used, layer, w_gu, b_gu, w_down, b_down):
    rows = xs.shape[0]
    d = D_MODEL
    n_blocks = rows // MOE_BLOCK
    f2 = w_gu.shape[3]
    depth = w_gu.shape[0]
    xmap = lambda i, be, nu: (jnp.minimum(i, nu[0] - 1), 0)
    return pl.pallas_call(
        _expert_kernel,
        grid_spec=pltpu.PrefetchScalarGridSpec(
            num_scalar_prefetch=2,
            grid=(n_blocks,),
            in_specs=[
                pl.BlockSpec((MOE_BLOCK, d + LANE), xmap),
                pl.BlockSpec((1, 1, d, f2), lambda i, be, nu: (layer, be[i], 0, 0)),
                pl.BlockSpec((1, 1, 1, f2), lambda i, be, nu: (layer, be[i], 0, 0)),
                pl.BlockSpec((1, 1, D_FF, d), lambda i, be, nu: (layer, be[i], 0, 0)),
                pl.BlockSpec((1, 1, 1, d), lambda i, be, nu: (layer, be[i], 0, 0)),
            ],
            out_specs=pl.BlockSpec((MOE_BLOCK, d), lambda i, be, nu: (i, 0)),
            scratch_shapes=[pltpu.VMEM((d, f2), BF16), pltpu.VMEM((D_FF, d), BF16)],
        ),
        out_shape=jax.ShapeDtypeStruct((rows, d), F32),
        compiler_params=_cparams(("arbitrary",)),
        name="moe_experts",
    )(block_expert, n_used, xs, w_gu, b_gu.reshape(depth, N_EXPERTS, 1, f2), w_down,
      b_down.reshape(depth, N_EXPERTS, 1, d))


def _combine_kernel(base_ref, tinfo_ref, tnext_ref, hot_ref, x_ref, g_ref, b_ref, y_ref, o_ref, ybuf2_s, sems):
    s_rows = ybuf2_s.shape[1]
    step = pl.program_id(0)
    ybuf_s, sem = ybuf2_s.at[step % 2], sems.at[step % 2]
    ynext_s, next_sem = ybuf2_s.at[(step + 1) % 2], sems.at[(step + 1) % 2]

    @pl.when(step == 0)
    def _():
        ybuf2_s[...] = jnp.zeros(ybuf2_s.shape, F32)
        _for_each_group(tinfo_ref, base_ref, lambda s, r, rows: _group_copy(y_ref, r, ybuf_s, s, sem, rows).start())

    @pl.when(step + 1 < pl.num_programs(0))
    def _():
        _for_each_group(tnext_ref, base_ref,
                        lambda s, r, rows: _group_copy(y_ref, r, ynext_s, s, next_sem, rows).start())

    n_split = COMBINE_SPLIT if s_rows % (COMBINE_SPLIT * LANE) == 0 else 1
    width = s_rows // n_split
    _wait_groups(tinfo_ref[0, 0, 3 * N_EXPERTS], tinfo_ref[0, 0, 3 * N_EXPERTS + 1], y_ref, ybuf_s, sem)
    moe = None
    for c in range(n_split):
        rows = slice(c * width, (c + 1) * width)
        part = _dot_tn(hot_ref[0, rows, :], ybuf_s[rows, :].astype(BF16))
        moe = part if moe is None else moe + part
    o_ref[...] = _layer_norm(DEEPNORM_ALPHA * x_ref[...] + moe, g_ref[...], b_ref[...])


def _combine(y, x2, hot, tinfo, base, ln_g, ln_b):
    n, d = x2.shape
    tb = min(MOE_TILE, n)
    s_rows = _sorted_rows(tb)
    return pl.pallas_call(
        _combine_kernel,
        grid_spec=pltpu.PrefetchScalarGridSpec(
            num_scalar_prefetch=1,
            grid=(n // tb,),
            in_specs=[
                _tile_info_spec(n // tb), _tile_info_spec(n // tb, ahead=1),
                pl.BlockSpec((1, s_rows, tb), lambda i, *_: (i, 0, 0)),
                pl.BlockSpec((tb, d), lambda i, *_: (i, 0)),
                pl.BlockSpec((1, d), lambda i, *_: (0, 0)),
                pl.BlockSpec((1, d), lambda i, *_: (0, 0)),
                pl.BlockSpec(memory_space=pl.ANY),
            ],
            out_specs=pl.BlockSpec((tb, d), lambda i, *_: (i, 0)),
            scratch_shapes=[pltpu.VMEM((2, s_rows, d), F32), pltpu.SemaphoreType.DMA((2,))],
        ),
        out_shape=jax.ShapeDtypeStruct((n, d), F32),
        compiler_params=_cparams(("arbitrary",)),
        name="moe_combine",
    )(base, tinfo, tinfo, hot, x2, ln_g.reshape(1, d), ln_b.reshape(1, d), y)


def _sorted_rows(tb):
    return TOP_K * tb + N_EXPERTS * ROW_ALIGN


def _moe_ffn_ln(x2, layer, router_w, router_b, w_gu, b_gu, w_down, b_down, ln_g, ln_b):
    n, _ = x2.shape
    n_tiles = n // min(MOE_TILE, n)
    spos, gate, tinfo, seg = _router(x2, router_w, router_b)
    seg = seg.reshape(N_EXPERTS)
    max_rows = n * TOP_K + n_tiles * N_EXPERTS * (ROW_ALIGN - 1) + N_EXPERTS * (MOE_BLOCK - 1)
    n_blocks = (max_rows + MOE_BLOCK - 1) // MOE_BLOCK
    padded = (seg + MOE_BLOCK - 1) // MOE_BLOCK * MOE_BLOCK
    padded_end = jnp.cumsum(padded)
    base = jnp.concatenate([jnp.zeros((1,), jnp.int32), padded_end]).astype(jnp.int32)
    n_used = (padded_end[-1:] // MOE_BLOCK).astype(jnp.int32)
    block_start = jnp.arange(n_blocks, dtype=jnp.int32) * MOE_BLOCK
    block_expert = jnp.sum(block_start[:, None] >= padded_end[None, :], axis=1).astype(jnp.int32)
    last_expert = jnp.sum(padded_end[-1] - 1 >= padded_end).astype(jnp.int32)
    block_expert = jnp.minimum(block_expert, last_expert)
    xs, hot = _dispatch(x2, spos, gate, tinfo, base, seg, n_blocks * MOE_BLOCK)
    y = _experts(xs, block_expert, n_used, layer, w_gu, b_gu, w_down, b_down)
    return _combine(y, x2, hot, tinfo, base, ln_g, ln_b)


def kernel(x, ln_g, ln_b, nsa_w_in, nsa_cmp_pos, nsa_cmp_w1, nsa_cmp_b1, nsa_cmp_w2, nsa_cmp_b2, nsa_gate_b, nsa_w_out, ml_w_in, ml_conv_w, ml_conv_b, ml_gate_b, ml_norm_g, ml_w_out, hg_w_in, hg_lower, hg_norm_g, hg_w_out, router_w, router_b, moe_w_gu, moe_b_gu, moe_w_down, moe_b_down):
    bsz, t, d = x.shape
    x2 = x.reshape(bsz * t, d)
    for layer in range(DEPTH):
        kind, slot = layer % N_MIXERS, layer // N_MIXERS
        if kind == 0:
            x2 = _nsa_mixer(x2, bsz, t, nsa_w_in[slot], nsa_cmp_pos[slot], nsa_cmp_w1[slot], nsa_cmp_b1[slot],
                            nsa_cmp_w2[slot], nsa_cmp_b2[slot], nsa_gate_b[slot], nsa_w_out[slot],
                            ln_g[layer, 0], ln_b[layer, 0])
        elif kind == 1:
            x2 = _mlstm_mixer(x2, bsz, t, ml_w_in[slot], ml_conv_w[slot], ml_conv_b[slot], ml_gate_b[slot],
                              ml_norm_g[slot], ml_w_out[slot], ln_g[layer, 0], ln_b[layer, 0])
        else:
            x2 = _hgrn_mixer(x2, bsz, t, layer, hg_w_in[slot], hg_lower, hg_norm_g[slot], hg_w_out[slot],
                             ln_g[layer, 0], ln_b[layer, 0])
        x2 = _moe_ffn_ln(x2, layer, router_w[layer], router_b[layer], moe_w_gu, moe_b_gu, moe_w_down, moe_b_down,
                         ln_g[layer, 1], ln_b[layer, 1])
    return x2.reshape(bsz, t, d)
```

```python
import functools

import numpy as np
import jax
import jax.numpy as jnp
from jax import lax
from jax.experimental import pallas as pl
from jax.experimental.pallas import tpu as pltpu

F32 = jnp.float32
BF16 = jnp.bfloat16
HIGHEST = lax.Precision.HIGHEST

D_MODEL = 1024
DEPTH = 4
N_MIXERS = 3

NSA_HEADS = 16
NSA_KV_GROUPS = 4
NSA_HEAD_DIM = 64
NSA_HPG = 4
NSA_KV_WIDTH = 256
CMP_BLOCK = 32
CMP_STRIDE = 16
CMP_HIDDEN = 256
SEL_BLOCK = 64
SEL_TOPK = 8
WINDOW = 512

MLSTM_HEADS = 4
MLSTM_QK_DIM = 128
MLSTM_V_DIM = 256
MLSTM_CONV = 4

HGRN_HEADS = 8
HGRN_DIM = 128
HGRN_CHUNK = 32

N_EXPERTS = 32
TOP_K = 4
D_FF = 1024
SWIGLU_LIMIT = 7.0
SWIGLU_ALPHA = 1.702

LN_EPS = 1e-5
RMS_EPS = 1e-6
DEEPNORM_ALPHA = (2 * DEPTH) ** 0.25
NEG = -1e30
M_FLOOR = -1e20

LANE = 128
VMEM_LIMIT = 56 * 1024 * 1024

PROJ_TM = 512
MOE_TILE = 512
MOE_BLOCK = 512
ROW_ALIGN = 8
BIG_GROUP = 32
SORT_ROWS = 128
COMBINE_SPLIT = 3
NSA_TQ = 256
NSA_TK = 512
MLSTM_L = 512
HGRN_SUPER = 128


def _cparams(sem):
    return pltpu.CompilerParams(dimension_semantics=sem, vmem_limit_bytes=VMEM_LIMIT)


def _dot(a, b):
    return jnp.dot(a, b, preferred_element_type=F32)


def _dot_nt(a, b, precision=None):
    return lax.dot_general(a, b, (((1,), (1,)), ((), ())), precision=precision, preferred_element_type=F32)


def _dot_tn(a, b):
    return lax.dot_general(a, b, (((0,), (0,)), ((), ())), preferred_element_type=F32)


def _layer_norm(z, g, b):
    mu = jnp.mean(z, axis=-1, keepdims=True)
    zc = z - mu
    var = jnp.mean(zc * zc, axis=-1, keepdims=True)
    return zc * lax.rsqrt(var + LN_EPS) * g + b


def _proj_kernel(x_ref, *refs, n_w, n_t):
    w_refs = refs[:n_w]
    wt_refs = refs[n_w:n_w + n_t]
    o_refs = refs[n_w + n_t:2 * n_w + n_t]
    ot_refs = refs[2 * n_w + n_t:]
    x = x_ref[...]
    xb = x.astype(BF16)
    for w_ref, o_ref in zip(w_refs, o_refs):
        ncol = w_ref.shape[1]
        for c0 in range(0, ncol, 512):
            c1 = min(c0 + 512, ncol)
            o_ref[:, c0:c1] = _dot(xb, w_ref[:, c0:c1]).astype(o_ref.dtype)
    for wt_ref, o_ref in zip(wt_refs, ot_refs):
        o_ref[...] = _dot_nt(wt_ref[...], x, precision=HIGHEST)


def _proj(x2, weights, out_dtypes, weights_t=()):
    n, d = x2.shape
    tm = min(PROJ_TM, n)
    in_specs = [pl.BlockSpec((tm, d), lambda i: (i, 0))]
    in_specs += [pl.BlockSpec(w.shape, lambda i: (0, 0)) for w in weights]
    in_specs += [pl.BlockSpec(w.shape, lambda i: (0, 0)) for w in weights_t]
    out_shape = [jax.ShapeDtypeStruct((n, w.shape[1]), dt) for w, dt in zip(weights, out_dtypes)]
    out_shape += [jax.ShapeDtypeStruct((w.shape[0], n), F32) for w in weights_t]
    out_specs = [pl.BlockSpec((tm, w.shape[1]), lambda i: (i, 0)) for w in weights]
    out_specs += [pl.BlockSpec((w.shape[0], tm), lambda i: (0, i)) for w in weights_t]
    return pl.pallas_call(
        functools.partial(_proj_kernel, n_w=len(weights), n_t=len(weights_t)),
        grid=(n // tm,),
        in_specs=in_specs,
        out_specs=out_specs,
        out_shape=out_shape,
        compiler_params=_cparams(("parallel",)),
        name="proj",
    )(x2, *weights, *weights_t)


def _outproj_ln_kernel(h_ref, w_ref, x_ref, g_ref, b_ref, o_ref):
    y = _dot(h_ref[...], w_ref[...])
    o_ref[...] = _layer_norm(DEEPNORM_ALPHA * x_ref[...] + y, g_ref[...], b_ref[...])


def _outproj_ln(h2, w, x2, g, b):
    n, d = x2.shape
    tm = min(PROJ_TM, n)
    return pl.pallas_call(
        _outproj_ln_kernel,
        grid=(n // tm,),
        in_specs=[
            pl.BlockSpec((tm, d), lambda i: (i, 0)),
            pl.BlockSpec((d, d), lambda i: (0, 0)),
            pl.BlockSpec((tm, d), lambda i: (i, 0)),
            pl.BlockSpec((1, d), lambda i: (0, 0)),
            pl.BlockSpec((1, d), lambda i: (0, 0)),
        ],
        out_specs=pl.BlockSpec((tm, d), lambda i: (i, 0)),
        out_shape=jax.ShapeDtypeStruct((n, d), F32),
        compiler_params=_cparams(("parallel",)),
        name="outproj_ln",
    )(h2, w, x2, g.reshape(1, d), b.reshape(1, d))


def _gelu_tanh(x):
    return 0.5 * x * (1.0 + jnp.tanh(np.sqrt(2.0 / np.pi).astype(np.float32) * (x + 0.044715 * (x * x * x))))


def _nsa_compress_kernel(kv_ref, wcat_ref, posa_ref, posb_ref, b1_ref, w2_ref, b2_ref, o_ref):
    n_row = kv_ref.shape[1] // CMP_STRIDE
    half = 2 * CMP_HIDDEN
    acc_a = jnp.zeros((n_row, half), F32)
    acc_b = jnp.zeros((n_row, half), F32)
    for r in range(CMP_STRIDE):
        xr = kv_ref[0, pl.ds(r, n_row, stride=CMP_STRIDE), :]
        w = wcat_ref[r]
        acc_a = acc_a + _dot((xr + posa_ref[r]).astype(BF16), w[:, :half])
        acc_b = acc_b + _dot((xr + posb_ref[r]).astype(BF16), w[:, half:])
    hidden = acc_a + pltpu.roll(acc_b, n_row - 1, 0) + b1_ref[...]
    hidden = _gelu_tanh(hidden)
    o_ref[0, 0] = (_dot(hidden.astype(BF16), w2_ref[...]) + b2_ref[...]).astype(o_ref.dtype)


def _nsa_compress(kvc, wcat, posa, posb, b1cat, w2cat, b2cat):
    bsz, t, _ = kvc.shape
    n_row = t // CMP_STRIDE
    return pl.pallas_call(
        _nsa_compress_kernel,
        grid=(bsz, NSA_KV_GROUPS),
        in_specs=[
            pl.BlockSpec((1, t, LANE), lambda b, g: (b, 0, g)),
            pl.BlockSpec(wcat.shape, lambda b, g: (0, 0, 0)),
            pl.BlockSpec(posa.shape, lambda b, g: (0, 0, 0)),
            pl.BlockSpec(posb.shape, lambda b, g: (0, 0, 0)),
            pl.BlockSpec(b1cat.shape, lambda b, g: (0, 0)),
            pl.BlockSpec(w2cat.shape, lambda b, g: (0, 0)),
            pl.BlockSpec(b2cat.shape, lambda b, g: (0, 0)),
        ],
        out_specs=pl.BlockSpec((1, 1, n_row, LANE), lambda b, g: (b, g, 0, 0)),
        out_shape=jax.ShapeDtypeStruct((bsz, NSA_KV_GROUPS, n_row, LANE), BF16),
        compiler_params=_cparams(("parallel", "parallel")),
        name="nsa_compress",
    )(kvc, wcat, posa, posb, b1cat, w2cat, b2cat)


def _nsa_attn_kernel(q_ref, kvc_ref, kvs_ref, kvw_ref, gate_ref, gb_ref, c2s_ref, o_ref, *, tq, tk, n_cmp):
    def tile(qi, carry):
        _nsa_attn_tile(qi, q_ref, kvc_ref, kvs_ref, kvw_ref, gate_ref, gb_ref, c2s_ref, o_ref, tq=tq, tk=tk, n_cmp=n_cmp)
        return carry
    lax.fori_loop(0, q_ref.shape[1] // tq, tile, 0)


def _nsa_attn_tile(qi, q_ref, kvc_ref, kvs_ref, kvw_ref, gate_ref, gb_ref, c2s_ref, o_ref, *, tq, tk, n_cmp):
    hpg = NSA_HPG
    t0 = pl.multiple_of(qi * tq, tq)
    t_col = t0 + lax.broadcasted_iota(jnp.int32, (tq, 1), 0)
    lane = lax.broadcasted_iota(jnp.int32, (1, LANE), 1)
    ones_lane = (lane == 0).astype(BF16)

    def with_ones(kv):
        return jnp.where(lane < NSA_HEAD_DIM, ones_lane, kv)

    gates = jax.nn.sigmoid(gate_ref[0, pl.ds(t0, tq), :].astype(F32) + gb_ref[0])
    gate_of = [[jnp.broadcast_to(gates[:, 3 * h + j:3 * h + j + 1], (tq, LANE)) for j in range(3)] for h in range(hpg)]

    span = WINDOW + tq
    w0 = pl.multiple_of(jnp.maximum(t0 - WINDOW, 0), tq)
    kvw = kvw_ref[0, pl.ds(w0, span), :]
    kvw1 = with_ones(kvw)
    kvc = kvc_ref[0, 0]
    n_c = kvc.shape[0]
    q_heads = []
    for h in range(hpg):
        pair_blk = q_ref[0, pl.ds(t0, tq), (h // 2) * LANE:(h // 2 + 1) * LANE].astype(F32)
        if h % 2:
            pair_blk = pltpu.roll(pair_blk, NSA_HEAD_DIM, 1)
        q_heads.append((jnp.where(lane < NSA_HEAD_DIM, pair_blk, 0.0) * NSA_HEAD_DIM ** -0.5).astype(BF16))
    s_c = [_dot_nt(q_heads[h], kvc) for h in range(hpg)]
    s_w = [_dot_nt(q_heads[h], kvw).astype(BF16) for h in range(hpg)]

    c_lane = lax.broadcasted_iota(jnp.int32, (1, n_c), 1)
    valid_c = (c_lane * CMP_STRIDE + (CMP_BLOCK - 1) <= t_col) & (c_lane < n_cmp)
    p_heads = []
    for h in range(hpg):
        sm = jnp.where(valid_c, s_c[h], NEG)
        m = jnp.max(sm, axis=-1, keepdims=True)
        p = jnp.where(valid_c, jnp.exp(sm - m), 0.0)
        l = jnp.sum(p, axis=-1, keepdims=True)
        p_heads.append(p / jnp.maximum(l, 1e-30))
    p_sum = p_heads[0]
    for h in range(1, hpg):
        p_sum = p_sum + p_heads[h]
    imp = _dot_nt(c2s_ref[...], p_sum, precision=HIGHEST)
    o_c = [_dot(p_heads[h].astype(BF16), kvc) for h in range(hpg)]

    lag = t_col - (w0 + lax.broadcasted_iota(jnp.int32, (1, span), 1))
    bias_w = jnp.where((lag >= 0) & (lag < WINDOW), 0.0, NEG).astype(BF16)
    o_w = []
    for h in range(hpg):
        sm = s_w[h] + bias_w
        o_w.append(_dot(jnp.exp(sm - jnp.max(sm, axis=-1, keepdims=True)), kvw1))
    partial = [gate_of[h][0] * o_c[h] + gate_of[h][2] * (o_w[h] / o_w[h][:, 0:1]) for h in range(hpg)]

    n_sel = imp.shape[0]
    blk = lax.broadcasted_iota(jnp.int32, (n_sel, 1), 0)
    cur = (t0 + lax.broadcasted_iota(jnp.int32, (1, tq), 1)) // SEL_BLOCK
    forced = (blk == 0) | (blk == cur) | (blk == cur - 1)
    score = jnp.where(forced, -NEG, jnp.where(blk <= cur, imp, NEG))
    rank = jnp.zeros((n_sel, tq), F32)
    for i in range(n_sel):
        s_i = score[i:i + 1, :]
        ahead = (s_i > score) | ((s_i == score) & (blk > i))
        rank = rank + ahead.astype(F32)
    member = ((rank < SEL_TOPK) & (score > 0.5 * NEG)).astype(BF16)

    blk_per_tile = tk // SEL_BLOCK
    key_lane = lax.broadcasted_iota(jnp.int32, (1, tk), 1)

    def sel_body(kt, carry):
        k0 = pl.multiple_of(kt * tk, tk)
        kv = kvs_ref[0, pl.ds(k0, tk), :]
        kv1 = with_ones(kv)
        expand = (key_lane // SEL_BLOCK + kt * blk_per_tile == blk).astype(BF16)
        valid = (_dot_tn(member, expand) > 0.5) & (k0 + key_lane <= t_col)
        bias = jnp.where(valid, 0.0, NEG).astype(BF16)
        s = [_dot_nt(q_heads[h], kv).astype(BF16) for h in range(hpg)]
        new = []
        for h in range(hpg):
            m_old, acc_old = carry[h]
            sm = s[h] + bias
            m_new = jnp.maximum(m_old, jnp.max(sm, axis=-1, keepdims=True).astype(F32))
            pv = _dot(jnp.exp(sm - m_new.astype(BF16)), kv1)
            new.append((m_new, jnp.exp(m_old - m_new) * acc_old + pv))
        return tuple(new)

    init = tuple((jnp.full((tq, 1), M_FLOOR, F32), jnp.zeros((tq, LANE), F32)) for _ in range(hpg))
    n_kt = (t0 + tq - 1) // tk + 1
    sel = lax.fori_loop(0, n_kt, sel_body, init)

    outs = []
    for h in range(hpg):
        o_s = sel[h][1]
        outs.append(partial[h] + gate_of[h][1] * (o_s / o_s[:, 0:1]))
    for j in range(hpg // 2):
        pair = jnp.where(lane < NSA_HEAD_DIM, pltpu.roll(outs[2 * j], NSA_HEAD_DIM, 1), outs[2 * j + 1])
        o_ref[0, pl.ds(t0, tq), j * LANE:(j + 1) * LANE] = pair.astype(o_ref.dtype)


def _nsa_attention(main, kvcmp, gate_b, c2s, bsz, t):
    tq = min(NSA_TQ, t)
    tk = min(NSA_TK, t)
    n_cmp = (t - CMP_BLOCK) // CMP_STRIDE + 1
    q_blocks = NSA_HEADS * NSA_HEAD_DIM // LANE
    g = NSA_KV_GROUPS
    kernel = functools.partial(_nsa_attn_kernel, tq=tq, tk=tk, n_cmp=n_cmp)
    return pl.pallas_call(
        kernel,
        grid=(bsz, g),
        in_specs=[
            pl.BlockSpec((1, t, NSA_HPG * NSA_HEAD_DIM), lambda b, gi: (b, 0, gi)),
            pl.BlockSpec((1, 1) + kvcmp.shape[2:], lambda b, gi: (b, gi, 0, 0)),
            pl.BlockSpec((1, t, LANE), lambda b, gi: (b, 0, q_blocks + gi)),
            pl.BlockSpec((1, t, LANE), lambda b, gi: (b, 0, q_blocks + g + gi)),
            pl.BlockSpec((1, t, LANE), lambda b, gi: (b, 0, q_blocks + 2 * g + gi)),
            pl.BlockSpec((1, 1, LANE), lambda b, gi: (gi, 0, 0)),
            pl.BlockSpec(c2s.shape, lambda b, gi: (0, 0)),
        ],
        out_specs=pl.BlockSpec((1, t, NSA_HPG * NSA_HEAD_DIM), lambda b, gi: (b, 0, gi)),
        out_shape=jax.ShapeDtypeStruct((bsz, t, D_MODEL), BF16),
        compiler_params=_cparams(("parallel", "parallel")),
        name="nsa_attention",
    )(main, kvcmp, main, main, main, gate_b, c2s)


def _nsa_mixer(x2, bsz, t, w_in, cmp_pos, cmp_w1, cmp_b1, cmp_w2, cmp_b2, gate_b, w_out, ln_g, ln_b):
    d, g, dh, hpg = D_MODEL, NSA_KV_GROUPS, NSA_HEAD_DIM, NSA_HPG
    kvw_ = NSA_KV_WIDTH

    def pair(k0):
        k = w_in[:, k0:k0 + kvw_].reshape(d, g, 1, dh)
        v = w_in[:, k0 + kvw_:k0 + 2 * kvw_].reshape(d, g, 1, dh)
        return jnp.concatenate([k, v], axis=2).reshape(d, g * LANE)

    wq = w_in[:, :d]
    n_gate = 3 * hpg
    wg = jnp.pad(w_in[:, d + 6 * kvw_:].reshape(d, g, n_gate), ((0, 0), (0, 0), (0, LANE - n_gate))).reshape(d, g * LANE)
    w_main = jnp.concatenate([wq, pair(d + 2 * kvw_), pair(d + 4 * kvw_), wg], axis=1).astype(BF16)
    w_kvc = pair(d).astype(BF16)
    main, kvc = _proj(x2, [w_main, w_kvc], [BF16, F32])

    half = CMP_BLOCK // 2
    w1 = cmp_w1.reshape(2, CMP_BLOCK, dh, CMP_HIDDEN)
    z = jnp.zeros((half, dh, CMP_HIDDEN), F32)
    top = jnp.concatenate([w1[0, :half], z, w1[0, half:], z], axis=2)
    bot = jnp.concatenate([z, w1[1, :half], z, w1[1, half:]], axis=2)
    wcat = jnp.concatenate([top, bot], axis=1).astype(BF16)
    pos = jnp.concatenate([cmp_pos[0], cmp_pos[1]], axis=-1)
    posa, posb = pos[:half, None, :], pos[half:, None, :]
    b1cat = cmp_b1.reshape(1, 2 * CMP_HIDDEN)
    zz = jnp.zeros((CMP_HIDDEN, dh), F32)
    w2cat = jnp.concatenate([jnp.concatenate([cmp_w2[0], zz], axis=1),
                             jnp.concatenate([zz, cmp_w2[1]], axis=1)], axis=0).astype(BF16)
    b2cat = cmp_b2.reshape(1, 2 * dh)
    kvcmp = _nsa_compress(kvc.reshape(bsz, t, g * LANE), wcat, posa, posb, b1cat, w2cat, b2cat)

    n_row = t // CMP_STRIDE
    n_sel = t // SEL_BLOCK
    cmp_start = np.arange(n_row) * CMP_STRIDE
    sel_start = np.arange(n_sel) * SEL_BLOCK
    overlap = (np.minimum(cmp_start[:, None] + CMP_BLOCK, sel_start[None, :] + SEL_BLOCK)
               - np.maximum(cmp_start[:, None], sel_start[None, :]))
    c2s = jnp.asarray((np.clip(overlap, 0, None) / CMP_STRIDE).astype(np.float32).T)
    gb = jnp.pad(gate_b.reshape(g, 1, n_gate), ((0, 0), (0, 0), (0, LANE - n_gate)))
    o = _nsa_attention(main.reshape(bsz, t, -1), kvcmp, gb, c2s, bsz, t)
    return _outproj_ln(o.reshape(bsz * t, d), w_out.astype(BF16), x2, ln_g, ln_b)


def _mlstm_kernel(q_ref, k_ref, v_ref, og_ref, gif_ref, gb_ref, cwq_ref, cwk_ref, cbq_ref, cbk_ref, ng_ref,
                  o_ref, pad_s, q_s, k_s, ct_s, *, chunk):
    t = q_ref.shape[1]
    dk, dv = MLSTM_QK_DIM, MLSTM_V_DIM
    head = pl.program_id(1)
    halo = 8

    def conv_silu(x_ref, w_ref, b_ref, dst, scale):
        pad_s[0:halo, :] = jnp.zeros((halo, dk), F32)
        pad_s[halo:halo + t, :] = x_ref[0]
        rows = min(t, 256)
        for r0 in range(0, t, rows):
            y = b_ref[...] + w_ref[0:1, :] * pad_s[pl.ds(halo + r0 - (MLSTM_CONV - 1), rows), :]
            for j in range(1, MLSTM_CONV):
                y = y + w_ref[j:j + 1, :] * pad_s[pl.ds(halo + r0 - (MLSTM_CONV - 1) + j, rows), :]
            y = y * jax.nn.sigmoid(y)
            dst[r0:r0 + rows, :] = (y * scale).astype(dst.dtype)

    conv_silu(q_ref, cwq_ref, cbq_ref, q_s, 1.0)
    conv_silu(k_ref, cwk_ref, cbk_ref, k_s, dk ** -0.5)
    ct_s[...] = jnp.zeros((dk, dv), F32)

    row = lax.broadcasted_iota(jnp.int32, (chunk, chunk), 0)
    col = lax.broadcasted_iota(jnp.int32, (chunk, chunk), 1)
    causal = col <= row
    eye = col == row
    b_i = gb_ref[pl.ds(head, 1), :]
    b_f = gb_ref[pl.ds(MLSTM_HEADS + head, 1), :]
    norm_g = ng_ref[...]

    def body(c, carry):
        n_row, m_prev = carry
        r0 = pl.multiple_of(c * chunk, chunk)
        qc = q_s[pl.ds(r0, chunk), :]
        kc = k_s[pl.ds(r0, chunk), :]
        vc = v_ref[0, pl.ds(r0, chunk), :]
        li_row = gif_ref[0, c, pl.ds(head, 1), :] + b_i
        zf = gif_ref[0, c, pl.ds(MLSTM_HEADS + head, 1), :] + b_f
        lf_row = jnp.minimum(zf, 0.0) - jnp.log1p(jnp.exp(-jnp.abs(zf)))
        b_col = jnp.sum(jnp.where(causal, lf_row, 0.0), axis=-1, keepdims=True)
        b_row = jnp.sum(jnp.where(eye, b_col, 0.0), axis=0, keepdims=True)
        li_col = jnp.sum(jnp.where(eye, li_row, 0.0), axis=-1, keepdims=True)
        d_mat = jnp.where(causal, b_col - b_row + li_row, NEG)
        m_t = jnp.maximum(b_col + m_prev, jnp.max(d_mat, axis=-1, keepdims=True))
        w_inter = jnp.exp(b_col + m_prev - m_t)
        s = _dot_nt(qc, kc) * jnp.exp(d_mat - m_t)
        num = _dot(s.astype(BF16), vc) + w_inter * _dot(qc, ct_s[...].astype(BF16))
        den = jnp.sum(s, axis=-1, keepdims=True) + w_inter * jnp.sum(qc.astype(F32) * n_row, axis=-1, keepdims=True)
        h = num / jnp.maximum(jnp.abs(den), jnp.exp(-m_t))
        b_end = b_col[chunk - 1:chunk, :]
        decay = b_end - b_col + li_col
        m_new = jnp.maximum(b_end + m_prev, jnp.max(decay, axis=0, keepdims=True))
        w_k = jnp.exp(decay - m_new)
        scale = jnp.exp(b_end + m_prev - m_new)
        ct_s[...] = scale * ct_s[...] + _dot_tn(kc, (vc.astype(F32) * w_k).astype(BF16))
        n_new = scale * n_row + jnp.sum(w_k * kc.astype(F32), axis=0, keepdims=True)
        hn = h * lax.rsqrt(jnp.mean(h * h, axis=-1, keepdims=True) + RMS_EPS) * norm_g
        og = og_ref[0, pl.ds(r0, chunk), :].astype(F32)
        o_ref[0, pl.ds(r0, chunk), :] = (hn * jax.nn.sigmoid(og)).astype(o_ref.dtype)
        return n_new, m_new

    lax.fori_loop(0, t // chunk, body, (jnp.zeros((1, dk), F32), jnp.zeros((1, 1), F32)))


def _mlstm_core(qk, v, og, gif, gate_b, conv_w, conv_b, norm_g, bsz, t, chunk):
    hh, dk, dv = MLSTM_HEADS, MLSTM_QK_DIM, MLSTM_V_DIM
    nc = t // chunk
    kernel = functools.partial(_mlstm_kernel, chunk=chunk)
    return pl.pallas_call(
        kernel,
        grid=(bsz, hh),
        in_specs=[
            pl.BlockSpec((1, t, dk), lambda b, h: (b, 0, h)),
            pl.BlockSpec((1, t, dk), lambda b, h: (b, 0, hh + h)),
            pl.BlockSpec((1, t, dv), lambda b, h: (b, 0, h)),
            pl.BlockSpec((1, t, dv), lambda b, h: (b, 0, h)),
            pl.BlockSpec((1, nc, 2 * hh, chunk), lambda b, h: (b, 0, 0, 0)),
            pl.BlockSpec((2 * hh, 1), lambda b, h: (0, 0)),
            pl.BlockSpec((MLSTM_CONV, dk), lambda b, h: (0, h)),
            pl.BlockSpec((MLSTM_CONV, dk), lambda b, h: (0, hh + h)),
            pl.BlockSpec((1, dk), lambda b, h: (0, h)),
            pl.BlockSpec((1, dk), lambda b, h: (0, hh + h)),
            pl.BlockSpec((1, dv), lambda b, h: (0, h)),
        ],
        out_specs=pl.BlockSpec((1, t, dv), lambda b, h: (b, 0, h)),
        out_shape=jax.ShapeDtypeStruct((bsz, t, hh * dv), BF16),
        scratch_shapes=[
            pltpu.VMEM((t + 8, dk), F32),
            pltpu.VMEM((t, dk), BF16),
            pltpu.VMEM((t, dk), BF16),
            pltpu.VMEM((dk, dv), F32),
        ],
        compiler_params=_cparams(("parallel", "parallel")),
        name="mlstm",
    )(qk, qk, v, og, gif, gate_b.reshape(2 * hh, 1), conv_w, conv_w, conv_b.reshape(1, -1), conv_b.reshape(1, -1),
      norm_g.reshape(1, -1))


def _mlstm_mixer(x2, bsz, t, w_in, conv_w, conv_b, gate_b, norm_g, w_out, ln_g, ln_b):
    d, hh = D_MODEL, MLSTM_HEADS
    qkw = 2 * hh * MLSTM_QK_DIM
    vw = hh * MLSTM_V_DIM
    chunk = min(MLSTM_L, t)
    w_qk = w_in[:, :qkw].astype(BF16)
    w_v = w_in[:, qkw:qkw + vw].astype(BF16)
    w_og = w_in[:, qkw + vw:qkw + vw + d].astype(BF16)
    w_gif_t = w_in[:, qkw + vw + d:].T
    qk, v, og, gif_t = _proj(x2, [w_qk, w_v, w_og], [F32, BF16, BF16], [w_gif_t])
    gif = gif_t.reshape(2 * hh, bsz, t // chunk, chunk).transpose(1, 2, 0, 3)
    h = _mlstm_core(qk.reshape(bsz, t, qkw), v.reshape(bsz, t, vw), og.reshape(bsz, t, d), gif, gate_b,
                    conv_w, conv_b, norm_g, bsz, t, chunk)
    return _outproj_ln(h.reshape(bsz * t, d), w_out.astype(BF16), x2, ln_g, ln_b)


def _hgrn_kernel(q_ref, f_ref, i_ref, g_ref, low_ref, ng_ref, o_ref, st_s, qd_s, kd_s, ke_s, dec_s, *,
                 layer_idx, heads):
    t = q_ref.shape[1]
    dh, chunk = HGRN_DIM, HGRN_CHUNK
    low = low_ref[...]
    e = jnp.exp(low - jnp.max(low, axis=0, keepdims=True))
    soft = e / jnp.sum(e, axis=0, keepdims=True)
    lb = jnp.zeros((1, heads * dh), F32)
    for r in range(1, layer_idx + 1):
        lb = lb + soft[r:r + 1, :]
    norm_g = ng_ref[...]
    st_s[...] = jnp.zeros(st_s.shape, F32)
    width = heads * dh
    sup = min(HGRN_SUPER, t)
    n_sub = sup // chunk

    in_chunk = lax.broadcasted_iota(jnp.int32, (sup, 1), 0) % chunk

    def prep(s, carry):
        r0 = pl.multiple_of(s * sup, sup)
        f = lb + (1.0 - lb) * jax.nn.sigmoid(f_ref[0, pl.ds(r0, sup), :])
        gcum = jnp.log(f)
        shift = 1
        while shift < chunk:
            gcum = gcum + jnp.where(in_chunk >= shift, pltpu.roll(gcum, shift, 0), 0.0)
            shift *= 2
        g_end = jnp.concatenate(
            [jnp.broadcast_to(gcum[(j + 1) * chunk - 1:(j + 1) * chunk, :], (chunk, width)) for j in range(n_sub)], axis=0)
        qv = q_ref[0, pl.ds(r0, sup), :].astype(F32)
        kk = 1.0 - f
        qd_s[pl.ds(r0, sup), :] = (qv * jax.nn.sigmoid(qv) * jnp.exp(gcum)).astype(BF16)
        kd_s[pl.ds(r0, sup), :] = (kk * jnp.exp(-gcum)).astype(BF16)
        ke_s[pl.ds(r0, sup), :] = (kk * jnp.exp(g_end - gcum)).astype(BF16)
        for j in range(n_sub):
            dec_s[s, j:j + 1, :] = jnp.exp(gcum[(j + 1) * chunk - 1:(j + 1) * chunk, :])
        return carry

    lax.fori_loop(0, t // sup, prep, 0)

    row = lax.broadcasted_iota(jnp.int32, (sup, sup), 0)
    col = lax.broadcasted_iota(jnp.int32, (sup, sup), 1)
    keep = (col <= row) & (col // chunk == row // chunk)

    def body(s, carry):
        r0 = pl.multiple_of(s * sup, sup)
        decay = dec_s[s]
        col_of = [slice(h * dh, (h + 1) * dh) for h in range(heads)]
        sub_of = [slice(j * chunk, (j + 1) * chunk) for j in range(n_sub)]
        q_dec = [qd_s[pl.ds(r0, sup), c] for c in col_of]
        vv = [i_ref[0, pl.ds(r0, sup), c] for c in col_of]
        score = [_dot_nt(q_dec[h], kd_s[pl.ds(r0, sup), col_of[h]]) for h in range(heads)]
        k_end = [ke_s[pl.ds(r0, sup), c] for c in col_of]
        kv = [[_dot_tn(vv[h][r], k_end[h][r]) for r in sub_of] for h in range(heads)]
        intra = [_dot(jnp.where(keep, score[h], 0.0).astype(BF16), vv[h]) for h in range(heads)]
        states = []
        for h in range(heads):
            st = st_s[h]
            per_sub = []
            for j in range(n_sub):
                per_sub.append(st.astype(BF16))
                st = st * decay[j:j + 1, col_of[h]] + kv[h][j]
            st_s[h] = st
            states.append(per_sub)
        for h in range(heads):
            inter = [_dot_nt(q_dec[h][sub_of[j]], states[h][j]) for j in range(n_sub)]
            o = intra[h] + jnp.concatenate(inter, axis=0)
            on = o * lax.rsqrt(jnp.mean(o * o, axis=-1, keepdims=True) + RMS_EPS) * norm_g[:, col_of[h]]
            gate = jax.nn.sigmoid(g_ref[0, pl.ds(r0, sup), col_of[h]].astype(F32))
            o_ref[0, pl.ds(r0, sup), col_of[h]] = (on * gate).astype(o_ref.dtype)
        return carry

    lax.fori_loop(0, t // sup, body, 0)


def _hgrn_core(q, f, i, g, lower, norm_g, bsz, t, layer_idx):
    heads = 4
    width = heads * HGRN_DIM
    n_grp = HGRN_HEADS // heads
    kernel = functools.partial(_hgrn_kernel, layer_idx=layer_idx, heads=heads)
    act = pl.BlockSpec((1, t, width), lambda b, j: (b, 0, j))
    return pl.pallas_call(
        kernel,
        grid=(bsz, n_grp),
        in_specs=[act, act, act, act,
                  pl.BlockSpec((DEPTH, width), lambda b, j: (0, j)),
                  pl.BlockSpec((1, width), lambda b, j: (0, j))],
        out_specs=act,
        out_shape=jax.ShapeDtypeStruct((bsz, t, D_MODEL), BF16),
        scratch_shapes=[pltpu.VMEM((heads, HGRN_DIM, HGRN_DIM), F32), pltpu.VMEM((t, width), BF16),
                        pltpu.VMEM((t, width), BF16), pltpu.VMEM((t, width), BF16),
                        pltpu.VMEM((max(t // HGRN_SUPER, 1), min(HGRN_SUPER, t) // HGRN_CHUNK, width), F32)],
        compiler_params=_cparams(("parallel", "parallel")),
        name="hgrn2",
    )(q, f, i, g, lower, norm_g.reshape(1, -1))


def _hgrn_mixer(x2, bsz, t, layer_idx, w_in, lower, norm_g, w_out, ln_g, ln_b):
    d = D_MODEL
    ws = [w_in[:, j * d:(j + 1) * d].astype(BF16) for j in range(4)]
    q, f, i, g = _proj(x2, ws, [BF16, F32, BF16, BF16])
    shp = (bsz, t, d)
    o = _hgrn_core(q.reshape(shp), f.reshape(shp), i.reshape(shp), g.reshape(shp), lower, norm_g, bsz, t, layer_idx)
    return _outproj_ln(o.reshape(bsz * t, d), w_out.astype(BF16), x2, ln_g, ln_b)


def _router_kernel(x_ref, rwt_ref, rb_ref, spos_ref, gate_ref, tinfo_ref, seg_ref, carry_s):
    @pl.when(pl.program_id(0) == 0)
    def _():
        carry_s[...] = jnp.zeros(carry_s.shape, F32)

    tm = x_ref.shape[0]
    ne = N_EXPERTS
    logits = _dot_nt(rwt_ref[...], x_ref[...], precision=HIGHEST) + rb_ref[...]
    e_iota = lax.broadcasted_iota(jnp.int32, (ne, tm), 0)
    work = logits
    vals, picks = [], []
    for k in range(TOP_K):
        mx = jnp.max(work, axis=0, keepdims=True)
        idx = jnp.min(jnp.where(work == mx, e_iota, ne), axis=0, keepdims=True)
        pick = e_iota == idx
        vals.append(mx)
        picks.append(pick)
        work = jnp.where(pick, -jnp.inf, work)
    exps = [jnp.exp(v - vals[0]) for v in vals]
    tot = exps[0]
    for k in range(1, TOP_K):
        tot = tot + exps[k]
    for k in range(TOP_K):
        gate_ref[k:k + 1, :] = exps[k] / tot
    hot = picks[0].astype(F32)
    for k in range(1, TOP_K):
        hot = hot + picks[k].astype(F32)
    before = (lax.broadcasted_iota(jnp.int32, (tm, tm), 0) < lax.broadcasted_iota(jnp.int32, (tm, tm), 1)).astype(BF16)
    prior = _dot(hot.astype(BF16), before)
    cnt = jnp.sum(hot, axis=1, keepdims=True)
    run = jnp.floor((cnt + (ROW_ALIGN - 1)) * (1.0 / ROW_ALIGN)) * ROW_ALIGN
    sub = lax.broadcasted_iota(jnp.int32, (ne, ne), 0)
    lan = lax.broadcasted_iota(jnp.int32, (ne, ne), 1)
    run_row = jnp.sum(jnp.where(sub == lan, run, 0.0), axis=0, keepdims=True)
    soff = jnp.sum(jnp.where(lan < sub, run_row, 0.0), axis=1, keepdims=True)
    for k in range(TOP_K):
        pos = jnp.sum(jnp.where(picks[k], prior + soff, 0.0), axis=0, keepdims=True)
        spos_ref[k:k + 1, :] = pos.astype(jnp.int32)
    n_big = jnp.floor(run * (1.0 / BIG_GROUP))
    sub = lax.broadcasted_iota(jnp.int32, (ne, LANE), 0)
    lan = lax.broadcasted_iota(jnp.int32, (ne, LANE), 1)
    field, le = lan // ne, lan % ne
    carry = carry_s[...]
    info = (jnp.where((field == 0) & (sub == le), carry, 0.0) + jnp.where((field == 1) & (sub == le), run, 0.0)
            + jnp.where((field == 2) & (sub < le), run, 0.0)
            + jnp.where((field == 3) & (le == 0), (run - n_big * BIG_GROUP) * (1.0 / ROW_ALIGN), 0.0)
            + jnp.where((field == 3) & (le == 1), n_big, 0.0))
    tinfo_ref[0] = jnp.sum(info, axis=0, keepdims=True).astype(jnp.int32)
    total = carry + run
    carry_s[...] = total
    seg_ref[...] = total.astype(jnp.int32)


def _router(x2, router_w, router_b):
    n, d = x2.shape
    tm = min(MOE_TILE, n)
    n_tiles = n // tm
    row = pl.BlockSpec((TOP_K, tm), lambda i: (0, i))
    return pl.pallas_call(
        _router_kernel,
        grid=(n_tiles,),
        in_specs=[
            pl.BlockSpec((tm, d), lambda i: (i, 0)),
            pl.BlockSpec((N_EXPERTS, d), lambda i: (0, 0)),
            pl.BlockSpec((N_EXPERTS, 1), lambda i: (0, 0)),
        ],
        out_specs=[row, row, pl.BlockSpec((1, 1, LANE), lambda i: (i, 0, 0)),
                   pl.BlockSpec((N_EXPERTS, 1), lambda i: (0, 0))],
        out_shape=[
            jax.ShapeDtypeStruct((TOP_K, n), jnp.int32),
            jax.ShapeDtypeStruct((TOP_K, n), F32),
            jax.ShapeDtypeStruct((n_tiles, 1, LANE), jnp.int32),
            jax.ShapeDtypeStruct((N_EXPERTS, 1), jnp.int32),
        ],
        scratch_shapes=[pltpu.VMEM((N_EXPERTS, 1), F32)],
        compiler_params=_cparams(("arbitrary",)),
        name="router",
    )(x2, router_w.T, router_b.reshape(N_EXPERTS, 1))


def _group_copy(src, s, dst, d, sem, rows=ROW_ALIGN):
    return pltpu.make_async_copy(src.at[pl.ds(pl.multiple_of(s, ROW_ALIGN), rows), :],
                                 dst.at[pl.ds(pl.multiple_of(d, ROW_ALIGN), rows), :], sem)


def _for_each_group(tinfo_ref, base_ref, fn):
    def per_expert(e, _):
        seg_row = base_ref[e] + tinfo_ref[0, 0, e]
        run = tinfo_ref[0, 0, N_EXPERTS + e]
        sorted_row = tinfo_ref[0, 0, 2 * N_EXPERTS + e]
        n_big = run // BIG_GROUP

        def big(j, _):
            fn(sorted_row + j * BIG_GROUP, seg_row + j * BIG_GROUP, BIG_GROUP, 0)
            return 0

        def small(j, _):
            fn(sorted_row + n_big * BIG_GROUP + j * ROW_ALIGN, seg_row + n_big * BIG_GROUP + j * ROW_ALIGN, ROW_ALIGN, 1)
            return 0

        lax.fori_loop(0, n_big, big, 0)
        return lax.fori_loop(0, (run - n_big * BIG_GROUP) // ROW_ALIGN, small, 0)
    lax.fori_loop(0, N_EXPERTS, per_expert, 0)


def _wait_groups(n_small, n_big, src, dst, sem):
    for count, rows in ((n_big, BIG_GROUP), (n_small, ROW_ALIGN)):
        def one(j, _, rows=rows):
            _group_copy(src, 0, dst, 0, sem, rows).wait()
            return 0
        lax.fori_loop(0, count, one, 0)


def _dispatch_kernel(base_ref, seg_ref, tinfo_ref, spos_ref, gate_ref, x_ref, xs_ref, hot_ref, sort2_s, zero_s, pending_s,
                     sems):
    tb, d = x_ref.shape
    s_rows = sort2_s.shape[1]
    step = pl.program_id(0)
    sort_s, sem = sort2_s.at[step % 2], sems.at[step % 2]
    prev_sem = sems.at[(step + 1) % 2]

    @pl.when(step == 0)
    def _():
        zero_s[...] = jnp.zeros(zero_s.shape, F32)
        for fill in (True, False):
            def per_expert(e, _):
                def per_group(r, _):
                    cp = _group_copy(zero_s, 0, xs_ref, r * ROW_ALIGN, sem)
                    cp.start() if fill else cp.wait()
                    return 0
                first = jnp.where(e < N_EXPERTS, base_ref[jnp.minimum(e, N_EXPERTS - 1)] + seg_ref[jnp.minimum(e, N_EXPERTS - 1)],
                                  base_ref[N_EXPERTS])
                last = jnp.where(e < N_EXPERTS, base_ref[jnp.minimum(e + 1, N_EXPERTS)], xs_ref.shape[0])
                return lax.fori_loop(first // ROW_ALIGN, last // ROW_ALIGN, per_group, 0)
            lax.fori_loop(0, N_EXPERTS + 1, per_expert, 0)

    xb = x_ref[...].astype(BF16)
    spos = spos_ref[...]
    gate = gate_ref[...]
    for r0 in range(0, s_rows, SORT_ROWS):
        r_iota = r0 + lax.broadcasted_iota(jnp.int32, (SORT_ROWS, tb), 0)
        hits = [spos[k:k + 1, :] == r_iota for k in range(TOP_K)]
        onehot = jnp.where(hits[0] | hits[1] | hits[2] | hits[3], 1.0, 0.0).astype(BF16)
        hot_ref[0, r0:r0 + SORT_ROWS, :] = onehot
        sort_s[r0:r0 + SORT_ROWS, :d] = _dot(onehot, xb)
        g_sel = jnp.zeros((SORT_ROWS, tb), F32)
        for k in range(TOP_K):
            g_sel = jnp.where(hits[k], gate[k:k + 1, :], g_sel)
        sort_s[r0:r0 + SORT_ROWS, d:] = jnp.broadcast_to(jnp.sum(g_sel, axis=1, keepdims=True), (SORT_ROWS, LANE))

    _for_each_group(tinfo_ref, base_ref,
                    lambda s, r, rows, prio: _group_copy(sort_s, s, xs_ref, r, sem, rows).start(priority=prio))
    n_small, n_big = tinfo_ref[0, 0, 3 * N_EXPERTS], tinfo_ref[0, 0, 3 * N_EXPERTS + 1]

    @pl.when(step > 0)
    def _():
        _wait_groups(pending_s[0], pending_s[1], sort_s, xs_ref, prev_sem)

    pending_s[0] = n_small
    pending_s[1] = n_big

    @pl.when(step == pl.num_programs(0) - 1)
    def _():
        _wait_groups(n_small, n_big, sort_s, xs_ref, sem)


def _tile_info_spec(n_tiles, ahead=0):
    return pl.BlockSpec((1, 1, LANE), lambda i, *_: (jnp.minimum(i + ahead, n_tiles - 1), 0, 0),
                        memory_space=pltpu.SMEM)


def _dispatch(x2, spos, gate, tinfo, base, seg, rows):
    n, d = x2.shape
    tb = min(MOE_TILE, n)
    row = pl.BlockSpec((TOP_K, tb), lambda i, *_: (0, i))
    return pl.pallas_call(
        _dispatch_kernel,
        grid_spec=pltpu.PrefetchScalarGridSpec(
            num_scalar_prefetch=2,
            grid=(n // tb,),
            in_specs=[_tile_info_spec(n // tb), row, row, pl.BlockSpec((tb, d), lambda i, *_: (i, 0))],
            out_specs=[pl.BlockSpec(memory_space=pl.ANY),
                       pl.BlockSpec((1, _sorted_rows(tb), tb), lambda i, *_: (i, 0, 0))],
            scratch_shapes=[pltpu.VMEM((2, _sorted_rows(tb), d + LANE), F32), pltpu.VMEM((ROW_ALIGN, d + LANE), F32),
                            pltpu.SMEM((2,), jnp.int32), pltpu.SemaphoreType.DMA((2,))],
        ),
        out_shape=[jax.ShapeDtypeStruct((rows, d + LANE), F32),
                   jax.ShapeDtypeStruct((n // tb, _sorted_rows(tb), tb), BF16)],
        compiler_params=_cparams(("arbitrary",)),
        name="moe_dispatch",
    )(base, seg, tinfo, spos, gate, x2)


def _expert_kernel(be_ref, nused_ref, x_ref, wgu_ref, bgu_ref, wd_ref, bd_ref, y_ref, wgu_s, wd_s):
    i = pl.program_id(0)
    used = i < nused_ref[0]

    @pl.when(used & ((i == 0) | (be_ref[i] != be_ref[jnp.maximum(i - 1, 0)])))
    def _():
        wgu_s[...] = wgu_ref[0, 0].astype(BF16)
        wd_s[...] = wd_ref[0, 0].astype(BF16)

    @pl.when(used)
    def _():
        d = y_ref.shape[1]
        xb = x_ref[:, :d].astype(BF16)
        h = _dot(xb, wgu_s[...]) + bgu_ref[0, 0]
        h_gate = jnp.minimum(h[:, :D_FF], SWIGLU_LIMIT)
        h_up = jnp.clip(h[:, D_FF:], -SWIGLU_LIMIT, SWIGLU_LIMIT)
        act = (h_up + 1.0) * h_gate * jax.nn.sigmoid(SWIGLU_ALPHA * h_gate)
        y_ref[...] = (_dot(act.astype(BF16), wd_s[...]) + bd_ref[0, 0]) * x_ref[:, d:d + 1]

    @pl.when(jnp.logical_not(used))
    def _():
        y_ref[...] = jnp.zeros(y_ref.shape, F32)


def _experts(xs, block_expert, n_used, layer, w_gu, b_gu, w_down, b_down):
    rows = xs.shape[0]
    d = D_MODEL
    n_blocks = rows // MOE_BLOCK
    f2 = w_gu.shape[3]
    depth = w_gu.shape[0]
    xmap = lambda i, be, nu: (jnp.minimum(i, nu[0] - 1), 0)
    return pl.pallas_call(
        _expert_kernel,
        grid_spec=pltpu.PrefetchScalarGridSpec(
            num_scalar_prefetch=2,
            grid=(n_blocks,),
            in_specs=[
                pl.BlockSpec((MOE_BLOCK, d + LANE), xmap),
                pl.BlockSpec((1, 1, d, f2), lambda i, be, nu: (layer, be[i], 0, 0)),
                pl.BlockSpec((1, 1, 1, f2), lambda i, be, nu: (layer, be[i], 0, 0)),
                pl.BlockSpec((1, 1, D_FF, d), lambda i, be, nu: (layer, be[i], 0, 0)),
                pl.BlockSpec((1, 1, 1, d), lambda i, be, nu: (layer, be[i], 0, 0)),
            ],
            out_specs=pl.BlockSpec((MOE_BLOCK, d), lambda i, be, nu: (i, 0)),
            scratch_shapes=[pltpu.VMEM((d, f2), BF16), pltpu.VMEM((D_FF, d), BF16)],
        ),
        out_shape=jax.ShapeDtypeStruct((rows, d), F32),
        compiler_params=_cparams(("arbitrary",)),
        name="moe_experts",
    )(block_expert, n_used, xs, w_gu, b_gu.reshape(depth, N_EXPERTS, 1, f2), w_down,
      b_down.reshape(depth, N_EXPERTS, 1, d))


def _combine_kernel(base_ref, tinfo_ref, tnext_ref, hot_ref, x_ref, g_ref, b_ref, y_ref, o_ref, ybuf2_s, sems):
    s_rows = ybuf2_s.shape[1]
    step = pl.program_id(0)
    ybuf_s, sem = ybuf2_s.at[step % 2], sems.at[step % 2]
    ynext_s, next_sem = ybuf2_s.at[(step + 1) % 2], sems.at[(step + 1) % 2]

    @pl.when(step == 0)
    def _():
        ybuf2_s[...] = jnp.zeros(ybuf2_s.shape, F32)
        _for_each_group(tinfo_ref, base_ref,
                        lambda s, r, rows, prio: _group_copy(y_ref, r, ybuf_s, s, sem, rows).start(priority=prio))

    @pl.when(step + 1 < pl.num_programs(0))
    def _():
        _for_each_group(tnext_ref, base_ref,
                        lambda s, r, rows, prio: _group_copy(y_ref, r, ynext_s, s, next_sem, rows).start(priority=prio))

    n_split = COMBINE_SPLIT if s_rows % (COMBINE_SPLIT * LANE) == 0 else 1
    width = s_rows // n_split
    _wait_groups(tinfo_ref[0, 0, 3 * N_EXPERTS], tinfo_ref[0, 0, 3 * N_EXPERTS + 1], y_ref, ybuf_s, sem)
    moe = None
    for c in range(n_split):
        rows = slice(c * width, (c + 1) * width)
        part = _dot_tn(hot_ref[0, rows, :], ybuf_s[rows, :].astype(BF16))
        moe = part if moe is None else moe + part
    o_ref[...] = _layer_norm(DEEPNORM_ALPHA * x_ref[...] + moe, g_ref[...], b_ref[...])


def _combine(y, x2, hot, tinfo, base, ln_g, ln_b):
    n, d = x2.shape
    tb = min(MOE_TILE, n)
    s_rows = _sorted_rows(tb)
    return pl.pallas_call(
        _combine_kernel,
        grid_spec=pltpu.PrefetchScalarGridSpec(
            num_scalar_prefetch=1,
            grid=(n // tb,),
            in_specs=[
                _tile_info_spec(n // tb), _tile_info_spec(n // tb, ahead=1),
                pl.BlockSpec((1, s_rows, tb), lambda i, *_: (i, 0, 0)),
                pl.BlockSpec((tb, d), lambda i, *_: (i, 0)),
                pl.BlockSpec((1, d), lambda i, *_: (0, 0)),
                pl.BlockSpec((1, d), lambda i, *_: (0, 0)),
                pl.BlockSpec(memory_space=pl.ANY),
            ],
            out_specs=pl.BlockSpec((tb, d), lambda i, *_: (i, 0)),
            scratch_shapes=[pltpu.VMEM((2, s_rows, d), F32), pltpu.SemaphoreType.DMA((2,))],
        ),
        out_shape=jax.ShapeDtypeStruct((n, d), F32),
        compiler_params=_cparams(("arbitrary",)),
        name="moe_combine",
    )(base, tinfo, tinfo, hot, x2, ln_g.reshape(1, d), ln_b.reshape(1, d), y)


def _sorted_rows(tb):
    return TOP_K * tb + N_EXPERTS * ROW_ALIGN


def _moe_ffn_ln(x2, layer, router_w, router_b, w_gu, b_gu, w_down, b_down, ln_g, ln_b):
    n, _ = x2.shape
    n_tiles = n // min(MOE_TILE, n)
    spos, gate, tinfo, seg = _router(x2, router_w, router_b)
    seg = seg.reshape(N_EXPERTS)
    max_rows = n * TOP_K + n_tiles * N_EXPERTS * (ROW_ALIGN - 1) + N_EXPERTS * (MOE_BLOCK - 1)
    n_blocks = (max_rows + MOE_BLOCK - 1) // MOE_BLOCK
    padded = (seg + MOE_BLOCK - 1) // MOE_BLOCK * MOE_BLOCK
    padded_end = jnp.cumsum(padded)
    base = jnp.concatenate([jnp.zeros((1,), jnp.int32), padded_end]).astype(jnp.int32)
    n_used = (padded_end[-1:] // MOE_BLOCK).astype(jnp.int32)
    block_start = jnp.arange(n_blocks, dtype=jnp.int32) * MOE_BLOCK
    block_expert = jnp.sum(block_start[:, None] >= padded_end[None, :], axis=1).astype(jnp.int32)
    last_expert = jnp.sum(padded_end[-1] - 1 >= padded_end).astype(jnp.int32)
    block_expert = jnp.minimum(block_expert, last_expert)
    xs, hot = _dispatch(x2, spos, gate, tinfo, base, seg, n_blocks * MOE_BLOCK)
    y = _experts(xs, block_expert, n_used, layer, w_gu, b_gu, w_down, b_down)
    return _combine(y, x2, hot, tinfo, base, ln_g, ln_b)


def kernel(x, ln_g, ln_b, nsa_w_in, nsa_cmp_pos, nsa_cmp_w1, nsa_cmp_b1, nsa_cmp_w2, nsa_cmp_b2, nsa_gate_b, nsa_w_out, ml_w_in, ml_conv_w, ml_conv_b, ml_gate_b, ml_norm_g, ml_w_out, hg_w_in, hg_lower, hg_norm_g, hg_w_out, router_w, router_b, moe_w_gu, moe_b_gu, moe_w_down, moe_b_down):
    bsz, t, d = x.shape
    x2 = x.reshape(bsz * t, d)
    for layer in range(DEPTH):
        kind, slot = layer % N_MIXERS, layer // N_MIXERS
        if kind == 0:
            x2 = _nsa_mixer(x2, bsz, t, nsa_w_in[slot], nsa_cmp_pos[slot], nsa_cmp_w1[slot], nsa_cmp_b1[slot],
                            nsa_cmp_w2[slot], nsa_cmp_b2[slot], nsa_gate_b[slot], nsa_w_out[slot],
                            ln_g[layer, 0], ln_b[layer, 0])
        elif kind == 1:
            x2 = _mlstm_mixer(x2, bsz, t, ml_w_in[slot], ml_conv_w[slot], ml_conv_b[slot], ml_gate_b[slot],
                              ml_norm_g[slot], ml_w_out[slot], ln_g[layer, 0], ln_b[layer, 0])
        else:
            x2 = _hgrn_mixer(x2, bsz, t, layer, hg_w_in[slot], hg_lower, hg_norm_g[slot], hg_w_out[slot],
                             ln_g[layer, 0], ln_b[layer, 0])
        x2 = _moe_ffn_ln(x2, layer, router_w[layer], router_b[layer], moe_w_gu, moe_b_gu, moe_w_down, moe_b_down,
                         ln_g[layer, 1], ln_b[layer, 1])
    return x2.reshape(bsz, t, d)
```
